```python
import math
import jax, jax.numpy as jnp
from jax import lax
import numpy as np

D_MODEL = 1024
BATCH = 2
SEQ = 8192
DEPTH = 2
DEC_BATCH = 128
DEC_SEQ = 1
PAST_LEN = 2048
PAGE_SIZE = 128

N_EVEN = (DEPTH + 1) // 2
N_ODD = DEPTH // 2
MIX_WIDTH = D_MODEL
A_WIDTH = MIX_WIDTH // 2
A_GROUPS = 4
A_GROUP_DIM = A_WIDTH // A_GROUPS
A_CHUNK = 128
B_HEADS = 4
B_HEAD_DIM = (MIX_WIDTH // 2) // B_HEADS
B_WIDTH = B_HEADS * B_HEAD_DIM
B_CONV = 4
B_CHUNK = 64
C_HEADS = 8
C_HEAD_DIM = MIX_WIDTH // (2 * C_HEADS)
C_VDIM = 2 * C_HEAD_DIM
C_QBLOCK = 128
D_FF = ((8 * D_MODEL // 3 + 127) // 128) * 128
FFN_CONV = 3
EPS = 1e-6

EVEN_IN = 2 * A_WIDTH + 4 * B_WIDTH + 2 * B_HEADS
EVEN_SPLITS = [A_WIDTH, 2 * A_WIDTH, 2 * A_WIDTH + 3 * B_WIDTH, 2 * A_WIDTH + 4 * B_WIDTH, 2 * A_WIDTH + 4 * B_WIDTH + B_HEADS]
ODD_IN = 3 * MIX_WIDTH

kernel_name = "hybrid_gmlp_gdn_diffattn_step"


def rmsnorm(x, g):
    xf = x.astype(jnp.float32)
    y = xf * lax.rsqrt(jnp.mean(xf * xf, axis=-1, keepdims=True) + EPS)
    return (y * g.astype(jnp.float32)).astype(x.dtype)


def l2norm(x):
    xf = x.astype(jnp.float32)
    return (xf * lax.rsqrt(jnp.sum(xf * xf, axis=-1, keepdims=True) + EPS)).astype(x.dtype)


def causal_dwconv(x, w, buf):
    width = w.shape[0]
    T = x.shape[1]
    xx = jnp.concatenate([buf.astype(x.dtype), x], axis=1)
    y = sum(xx[:, j:j + T] * w[j] for j in range(width))
    return y, xx[:, -(width - 1):]


def chunk_spatial_gate(u, v, w_s, b_s):
    Bn, T, _ = u.shape
    c = min(T, A_CHUNK)
    n = T // c
    vg = v.reshape(Bn, n, c, A_GROUPS, A_GROUP_DIM)
    w = jnp.tril(w_s[:, :c, :c])
    mixed = jnp.einsum('gts,bnsgd->bntgd', w, vg) + b_s[:, :c].T[None, None, :, :, None]
    return u * mixed.reshape(Bn, T, A_WIDTH)


def gated_delta_rule(q, k, v, g, beta, s0):
    Bn, T, H, DK = q.shape
    DV = v.shape[-1]
    c = min(T, B_CHUNK)
    n = T // c

    def blocks(a):
        a = a.astype(jnp.float32).reshape((Bn, n, c) + a.shape[2:])
        return jnp.swapaxes(jnp.moveaxis(a, 1, 0), 2, 3)

    xs = tuple(blocks(a) for a in (q, k, v, g, beta))
    incl = jnp.tril(jnp.ones((c, c), bool))
    strict = jnp.tril(jnp.ones((c, c), bool), -1)
    eye = jnp.eye(c, dtype=jnp.float32)

    def step(S, blk):
        qc, kc, vc, gc, bc = blk
        G = jnp.cumsum(gc, axis=-1)
        diff = G[..., :, None] - G[..., None, :]
        decay = jnp.where(incl, jnp.exp(jnp.where(incl, diff, 0.0)), 0.0)
        kk = jnp.einsum('bhtd,bhid->bhti', kc, kc)
        A = jnp.where(strict, bc[..., :, None] * decay * kk, 0.0)
        eG = jnp.exp(G)
        rhs = bc[..., None] * (vc - eG[..., None] * jnp.einsum('bhtd,bhde->bhte', kc, S))
        U = lax.linalg.triangular_solve(A + eye, rhs, left_side=True, lower=True, unit_diagonal=True)
        qk = jnp.einsum('bhtd,bhid->bhti', qc, kc) * decay
        o = eG[..., None] * jnp.einsum('bhtd,bhde->bhte', qc, S) + jnp.einsum('bhti,bhie->bhte', qk, U)
        w_end = jnp.exp(G[..., -1:] - G)
        S_new = eG[..., -1, None, None] * S + jnp.einsum('bhid,bhie->bhde', kc * w_end[..., None], U)
        return S_new, o

    S_fin, o = lax.scan(step, s0.astype(jnp.float32), xs)
    o = jnp.swapaxes(jnp.moveaxis(o, 0, 1), 2, 3).reshape(Bn, T, H, DV)
    return o.astype(v.dtype), S_fin


def diff_attention_block(q, k, v, q_pos, k_pos, lam, slopes):
    s = jnp.einsum('bqhcd,bkhcd->bhcqk', q, k).astype(jnp.float32)
    dist = (q_pos[:, None] - k_pos[None, :]).astype(jnp.float32)
    s = s - slopes[None, :, None, None, None] * dist
    s = jnp.where(dist >= 0, s, -jnp.inf)
    p = jax.nn.softmax(s, axis=-1)
    w = p[:, :, 0] - lam * p[:, :, 1]
    return jnp.einsum('bhqk,bkhe->bqhe', w.astype(v.dtype), v)


def prompt_attention(q, k, v, lam, slopes):
    Bn, T = q.shape[:2]
    nq = T // C_QBLOCK
    qb = jnp.moveaxis(q.reshape(Bn, nq, C_QBLOCK, C_HEADS, 2, C_HEAD_DIM), 1, 0)
    k_pos = jnp.arange(T)

    def one(args):
        qi, i = args
        q_pos = i * C_QBLOCK + jnp.arange(C_QBLOCK)
        return diff_attention_block(qi, k, v, q_pos, k_pos, lam, slopes)

    out = lax.map(one, (qb, jnp.arange(nq)))
    return jnp.moveaxis(out, 0, 1).reshape(Bn, T, C_HEADS, C_VDIM)


def conv_ffn(h, w_gate, w_up, w_conv, b_conv, w_down, buf):
    g = h @ w_gate
    g, new_buf = causal_dwconv(g, w_conv, buf)
    a = jax.nn.gelu(g + b_conv) * (h @ w_up)
    return a @ w_down, new_buf


def setup_inputs(seed: int = 0) -> dict:
    key = jax.random.key(seed)
    keys = iter(jax.random.split(key, 48))
    f32 = jnp.float32

    def nrm(shape, scale):
        return jax.random.normal(next(keys), shape, f32) * scale

    def gain(shape):
        return 1.0 + nrm(shape, 0.05)

    n_pages = PAST_LEN // PAGE_SIZE
    n_pool = (5 * DEC_BATCH * n_pages + 3) // 4
    perm = jax.random.permutation(next(keys), n_pool)
    page_table = perm[:DEC_BATCH * n_pages].reshape(DEC_BATCH, n_pages).astype(jnp.int32)
    dt = jnp.exp(jax.random.uniform(next(keys), (N_EVEN, B_HEADS), f32, math.log(1e-3), math.log(1e-1)))
    return {
        "x_prompt": nrm((BATCH, SEQ, D_MODEL), 1.0),
        "x_sample": nrm((DEC_BATCH, DEC_SEQ, D_MODEL), 1.0),
        "cache_k": nrm((N_ODD, n_pool, PAGE_SIZE, C_HEADS, 2, C_HEAD_DIM), 1.0),
        "cache_v": nrm((N_ODD, n_pool, PAGE_SIZE, C_HEADS, C_VDIM), 1.0),
        "page_table": page_table,
        "state_gdn": nrm((N_EVEN, DEC_BATCH, B_HEADS, B_HEAD_DIM, B_HEAD_DIM), 0.1),
        "state_gdn_conv": nrm((N_EVEN, DEC_BATCH, B_CONV - 1, 3 * B_WIDTH), 1.0),
        "state_ffn_conv": nrm((DEPTH, DEC_BATCH, FFN_CONV - 1, D_FF), 1.0),
        "norm_mix_pre": gain((DEPTH, D_MODEL)),
        "norm_mix_post": gain((DEPTH, D_MODEL)),
        "norm_ffn_pre": gain((DEPTH, D_MODEL)),
        "norm_ffn_post": gain((DEPTH, D_MODEL)),
        "w_in_even": nrm((N_EVEN, D_MODEL, EVEN_IN), D_MODEL ** -0.5),
        "a_v_norm": gain((N_EVEN, A_WIDTH)),
        "a_w_s": nrm((N_EVEN, A_GROUPS, A_CHUNK, A_CHUNK), A_CHUNK ** -0.5),
        "a_b_s": 1.0 + nrm((N_EVEN, A_GROUPS, A_CHUNK), 0.02),
        "b_conv_w": nrm((N_EVEN, B_CONV, 3 * B_WIDTH), B_CONV ** -0.5),
        "b_a_log": jnp.log(jax.random.uniform(next(keys), (N_EVEN, B_HEADS), f32, 1.0, 16.0)),
        "b_dt_bias": dt + jnp.log(-jnp.expm1(-dt)),
        "b_out_norm": gain((N_EVEN, B_HEAD_DIM)),
        "w_out_even": nrm((N_EVEN, MIX_WIDTH, D_MODEL), MIX_WIDTH ** -0.5),
        "w_in_odd": nrm((N_ODD, D_MODEL, ODD_IN), D_MODEL ** -0.5),
        "c_lambda": nrm((N_ODD, 4, C_HEAD_DIM), 0.1),
        "c_subln": gain((N_ODD, C_VDIM)),
        "w_out_odd": nrm((N_ODD, MIX_WIDTH, D_MODEL), MIX_WIDTH ** -0.5),
        "w_ffn_gate": nrm((DEPTH, D_MODEL, D_FF), D_MODEL ** -0.5),
        "w_ffn_up": nrm((DEPTH, D_MODEL, D_FF), D_MODEL ** -0.5),
        "w_ffn_conv": nrm((DEPTH, FFN_CONV, D_FF), FFN_CONV ** -0.5),
        "b_ffn_conv": nrm((DEPTH, D_FF), 0.01),
        "w_ffn_down": nrm((DEPTH, D_FF, D_MODEL), D_FF ** -0.5),
    }


def reference(x_prompt, x_sample, cache_k, cache_v, page_table, state_gdn, state_gdn_conv, state_ffn_conv,
              norm_mix_pre, norm_mix_post, norm_ffn_pre, norm_ffn_post,
              w_in_even, a_v_norm, a_w_s, a_b_s, b_conv_w, b_a_log, b_dt_bias, b_out_norm, w_out_even,
              w_in_odd, c_lambda, c_subln, w_out_odd,
              w_ffn_gate, w_ffn_up, w_ffn_conv, b_ffn_conv, w_ffn_down):
    f32 = jnp.float32
    slopes = jnp.exp2(-8.0 * jnp.arange(1, C_HEADS + 1, dtype=f32) / C_HEADS)

    def run_group(x, sample):
        Bn, T, _ = x.shape
        gdn_s, gdn_c, chunk_v, k_rows, v_rows, ffn_b = [], [], [], [], [], []
        for layer in range(DEPTH):
            j = layer // 2
            h = rmsnorm(x, norm_mix_pre[layer])
            if layer % 2 == 0:
                proj = h @ w_in_even[j]
                a_u, a_v, b_qkv, b_z, b_beta, b_alpha = jnp.split(proj, EVEN_SPLITS, axis=-1)
                a_u = jax.nn.gelu(a_u)
                a_v = rmsnorm(jax.nn.gelu(a_v), a_v_norm[j])
                a_out = chunk_spatial_gate(a_u, a_v, a_w_s[j], a_b_s[j])
                if sample:
                    conv0, s0 = state_gdn_conv[j], state_gdn[j]
                else:
                    conv0 = jnp.zeros((Bn, B_CONV - 1, 3 * B_WIDTH), x.dtype)
                    s0 = jnp.zeros((Bn, B_HEADS, B_HEAD_DIM, B_HEAD_DIM), f32)
                qkv, conv_new = causal_dwconv(b_qkv, b_conv_w[j], conv0)
                qkv = jax.nn.silu(qkv).reshape(Bn, T, 3, B_HEADS, B_HEAD_DIM)
                q = l2norm(qkv[:, :, 0]) * (B_HEAD_DIM ** -0.5)
                k = l2norm(qkv[:, :, 1])
                v = qkv[:, :, 2]
                beta = jax.nn.sigmoid(b_beta.astype(f32))
                g = -jnp.exp(b_a_log[j].astype(f32)) * jax.nn.softplus(b_alpha.astype(f32) + b_dt_bias[j].astype(f32))
                o, s_new = gated_delta_rule(q, k, v, g, beta, s0)
                o = rmsnorm(o, b_out_norm[j]) * jax.nn.silu(b_z.reshape(Bn, T, B_HEADS, B_HEAD_DIM))
                mix = jnp.concatenate([a_out, o.reshape(Bn, T, B_WIDTH)], axis=-1) @ w_out_even[j]
                gdn_s.append(s_new)
                gdn_c.append(conv_new)
                if sample:
                    chunk_v.append(a_v)
            else:
                proj = h @ w_in_odd[j]
                q, k, v = jnp.split(proj, 3, axis=-1)
                q = q.reshape(Bn, T, C_HEADS, 2, C_HEAD_DIM) * (C_HEAD_DIM ** -0.5)
                k = k.reshape(Bn, T, C_HEADS, 2, C_HEAD_DIM)
                v = v.reshape(Bn, T, C_HEADS, C_VDIM)
                lam_init = 0.8 - 0.6 * math.exp(-0.3 * layer)
                lq = c_lambda[j].astype(f32)
                lam = jnp.exp(jnp.sum(lq[0] * lq[1])) - jnp.exp(jnp.sum(lq[2] * lq[3])) + lam_init
                if sample:
                    n_past = page_table.shape[1] * PAGE_SIZE
                    k_past = cache_k[j, page_table].reshape(Bn, n_past, C_HEADS, 2, C_HEAD_DIM).astype(k.dtype)
                    v_past = cache_v[j, page_table].reshape(Bn, n_past, C_HEADS, C_VDIM).astype(v.dtype)
                    att = diff_attention_block(q, jnp.concatenate([k_past, k], axis=1),
                                               jnp.concatenate([v_past, v], axis=1),
                                               n_past + jnp.arange(T), jnp.arange(n_past + T), lam, slopes)
                else:
                    att = prompt_attention(q, k, v, lam, slopes)
                att = rmsnorm(att, c_subln[j]) * (1.0 - lam_init)
                mix = att.reshape(Bn, T, MIX_WIDTH) @ w_out_odd[j]
                k_rows.append(k)
                v_rows.append(v)
            x = x + rmsnorm(mix, norm_mix_post[layer])
            h = rmsnorm(x, norm_ffn_pre[layer])
            buf0 = state_ffn_conv[layer] if sample else jnp.zeros((Bn, FFN_CONV - 1, D_FF), x.dtype)
            f, buf_new = conv_ffn(h, w_ffn_gate[layer], w_ffn_up[layer], w_ffn_conv[layer], b_ffn_conv[layer],
                                  w_ffn_down[layer], buf0)
            ffn_b.append(buf_new)
            x = x + rmsnorm(f, norm_ffn_post[layer])
        return x, gdn_s, gdn_c, chunk_v, k_rows, v_rows, ffn_b

    y_p, gs_p, gc_p, _, k_p, v_p, fb_p = run_group(x_prompt, False)
    y_s, gs_s, gc_s, cv_s, k_s, v_s, fb_s = run_group(x_sample, True)
    return (y_p, y_s,
            jnp.stack(gs_p), jnp.stack(gs_s),
            jnp.stack(gc_p), jnp.stack(gc_s),
            jnp.stack(cv_s),
            jnp.stack(k_p), jnp.stack(v_p),
            jnp.stack(k_s), jnp.stack(v_s),
            jnp.stack(fb_p), jnp.stack(fb_s))
```

```python
import functools
import math

import jax
import jax.numpy as jnp
from jax import lax
from jax.experimental import pallas as pl
from jax.experimental.pallas import tpu as pltpu

F32 = jnp.float32
BF16 = jnp.bfloat16
EPS = 1e-6

A_GROUPS = 4
A_CHUNK = 128
B_HEADS = 4
B_CONV = 4
GDN_STEP = 128
GDN_CHUNK = 64
C_HEADS = 8
FFN_CONV = 3
NEG_BIG = -1e30

VMEM_LIMIT_BYTES = 56 * 1024 * 1024
HIGHEST = lax.Precision.HIGHEST


def _cparams(sem):
    return pltpu.CompilerParams(dimension_semantics=sem, vmem_limit_bytes=VMEM_LIMIT_BYTES)


def _gelu(x):
    return 0.5 * x * (1.0 + jnp.tanh(0.7978845608028654 * (x + 0.044715 * (x * x * x))))


def _sigmoid(x):
    return 1.0 / (1.0 + jnp.exp(-x))


def _softplus(x):
    return jnp.maximum(x, 0.0) + jnp.log(1.0 + jnp.exp(-jnp.abs(x)))


def _rms(x, gain):
    return x * lax.rsqrt(jnp.mean(x * x, axis=-1, keepdims=True) + EPS) * gain


def _dot(a, b, precision=None):
    return jnp.dot(a, b, preferred_element_type=F32, precision=precision)


def _dot_nt(a, b, precision=None):
    return lax.dot_general(a, b, (((1,), (1,)), ((), ())), preferred_element_type=F32,
                           precision=precision)


def _const_spec(shape):
    nd = len(shape)
    return pl.BlockSpec(shape, lambda *_: (0,) * nd, pipeline_mode=pl.Buffered(1))


def _even_in_kernel(x_ref, gpre_ref, w_ref, wba_ref, avn_ref, ws_ref, bs_ref,
                    aout_ref, v_ref, qkv_ref, z_ref, ba_ref, *, chunk, aw, bw):
    tm = x_ref.shape[0]
    xb = _rms(x_ref[...], gpre_ref[...]).astype(BF16)
    u = _gelu(_dot(xb, w_ref[:, 0:aw]))
    v = _rms(_gelu(_dot(xb, w_ref[:, aw:2 * aw])), avn_ref[...])
    v_ref[...] = v
    qkv_ref[...] = _dot(xb, w_ref[:, 2 * aw:2 * aw + 3 * bw])
    zz = _dot(xb, w_ref[:, 2 * aw + 3 * bw:2 * aw + 4 * bw])
    z_ref[...] = (zz * _sigmoid(zz)).astype(BF16)
    ba_ref[...] = _dot(xb, wba_ref[...])
    gd = aw // A_GROUPS
    if chunk == 1:
        aout_ref[...] = (u * (v * ws_ref[...] + bs_ref[...])).astype(BF16)
    else:
        row = lax.broadcasted_iota(jnp.int32, (chunk, chunk), 0)
        col = lax.broadcasted_iota(jnp.int32, (chunk, chunk), 1)
        for g in range(A_GROUPS):
            wt = jnp.where(col <= row, ws_ref[g], 0.0).astype(BF16)
            bcol = bs_ref[:, g:g + 1]
            for n in range(tm // chunk):
                rs = slice(n * chunk, (n + 1) * chunk)
                cs = slice(g * gd, (g + 1) * gd)
                mixed = _dot(wt, v[rs, cs].astype(BF16)) + bcol
                aout_ref[rs, cs] = (u[rs, cs] * mixed).astype(BF16)


def _even_in(x, gpre, w_main, w_ba, avn, ws, bs, *, chunk, tm):
    m, d = x.shape
    aw = avn.shape[-1]
    bw = (w_main.shape[1] - 2 * aw) // 4
    row = lambda i: (i, 0)
    return pl.pallas_call(
        functools.partial(_even_in_kernel, chunk=chunk, aw=aw, bw=bw),
        grid=(m // tm,),
        in_specs=[pl.BlockSpec((tm, d), row), _const_spec(gpre.shape), _const_spec(w_main.shape),
                  _const_spec(w_ba.shape), _const_spec(avn.shape), _const_spec(ws.shape),
                  _const_spec(bs.shape)],
        out_specs=[pl.BlockSpec((tm, aw), row), pl.BlockSpec((tm, aw), row),
                   pl.BlockSpec((tm, 3 * bw), row), pl.BlockSpec((tm, bw), row),
                   pl.BlockSpec((tm, 128), row)],
        out_shape=[jax.ShapeDtypeStruct((m, aw), BF16), jax.ShapeDtypeStruct((m, aw), F32),
                   jax.ShapeDtypeStruct((m, 3 * bw), F32), jax.ShapeDtypeStruct((m, bw), BF16),
                   jax.ShapeDtypeStruct((m, 128), F32)],
        compiler_params=_cparams(("parallel",)),
        name="even_in",
    )(x, gpre, w_main, w_ba, avn, ws, bs)


def _gdn_prompt_kernel(qkv_ref, ba_ref, z_ref, cw_ref, prm_ref, onorm_ref,
                       o_ref, st_ref,
                       ext_ref, wm_ref, u0_ref, qe_ref, qk_ref, kwt_ref, egl_ref, *, bw):
    i = pl.program_id(1)
    n = GDN_STEP
    c = GDN_CHUNK
    hd = bw // B_HEADS

    @pl.when(i == 0)
    def _():
        st_ref[...] = jnp.zeros_like(st_ref)
        ext_ref[n:n + 8, :] = jnp.zeros((8, ext_ref.shape[1]), F32)
        wm_ref[...] = jnp.zeros_like(wm_ref)
        u0_ref[...] = jnp.zeros_like(u0_ref)
        qe_ref[...] = jnp.zeros_like(qe_ref)
        qk_ref[...] = jnp.zeros_like(qk_ref)
        kwt_ref[...] = jnp.zeros_like(kwt_ref)
        egl_ref[...] = jnp.zeros_like(egl_ref)

    zero_half = jnp.zeros((c, hd), F32)
    for h in range(B_HEADS):
        s = st_ref[0, h]
        sb = s.astype(BF16)
        u_a = u0_ref[h, 0:c, :] - _dot(wm_ref[h, 0:c, :], sb)
        uf = jnp.concatenate([u_a, zero_half], axis=0).astype(BF16)
        o_a = _dot(qe_ref[h, 0:c, :], sb) + _dot(qk_ref[h, 0:c, :], uf)
        s = egl_ref[0:1, B_HEADS + h:B_HEADS + h + 1] * s + _dot(kwt_ref[h], uf)
        sb = s.astype(BF16)
        u_b = u0_ref[h, c:n, :] - _dot(wm_ref[h, c:n, :], sb)
        uf = jnp.concatenate([zero_half, u_b], axis=0).astype(BF16)
        o_b = _dot(qe_ref[h, c:n, :], sb) + _dot(qk_ref[h, c:n, :], uf)
        s = egl_ref[c:c + 1, B_HEADS + h:B_HEADS + h + 1] * s + _dot(kwt_ref[h], uf)
        st_ref[0, h] = s
        o = jnp.concatenate([o_a, o_b], axis=0)
        cs = slice(h * hd, (h + 1) * hd)
        o_ref[0, :, cs] = (_rms(o, onorm_ref[...]) * z_ref[0, :, cs].astype(F32)).astype(BF16)

    ext_ref[5:8, :] = ext_ref[n + 5:n + 8, :]
    cur = qkv_ref[0]
    ext_ref[8:n + 8, :] = cur
    y = (cw_ref[0:1, :] * ext_ref[5:n + 5, :] + cw_ref[1:2, :] * ext_ref[6:n + 6, :]
         + cw_ref[2:3, :] * ext_ref[7:n + 7, :] + cw_ref[3:4, :] * cur)
    y = y * _sigmoid(y)

    ba = ba_ref[0]
    beta = _sigmoid(ba)
    gfull = -jnp.exp(prm_ref[1:2, :]) * _softplus(ba + prm_ref[0:1, :])
    row = lax.broadcasted_iota(jnp.int32, (n, n), 0)
    col = lax.broadcasted_iota(jnp.int32, (n, n), 1)
    same = (row < c) == (col < c)
    incl = same & (col <= row)
    strict = same & (col < row)
    gcum = _dot(jnp.where(incl, 1.0, 0.0), gfull, HIGHEST)
    glast = _dot(jnp.where(same, 1.0, 0.0), gfull, HIGHEST)
    egl_ref[...] = jnp.exp(glast)
    gcum_t = gcum.T
    eye = jnp.where(row == col, 1.0, 0.0)

    for h in range(B_HEADS):
        q = y[:, h * hd:(h + 1) * hd]
        k = y[:, bw + h * hd:bw + (h + 1) * hd]
        v = y[:, 2 * bw + h * hd:2 * bw + (h + 1) * hd]
        q = q * lax.rsqrt(jnp.sum(q * q, axis=-1, keepdims=True) + EPS) * (hd ** -0.5)
        k = k * lax.rsqrt(jnp.sum(k * k, axis=-1, keepdims=True) + EPS)
        bcol = beta[:, h:h + 1]
        gcol = gcum[:, B_HEADS + h:B_HEADS + h + 1]
        grow = gcum_t[B_HEADS + h:B_HEADS + h + 1, :]
        glcol = glast[:, B_HEADS + h:B_HEADS + h + 1]
        decay = jnp.where(incl, jnp.exp(jnp.where(incl, gcol - grow, 0.0)), 0.0)
        kb = k.astype(BF16)
        kk = _dot_nt(kb, kb)
        x = -jnp.where(strict, bcol * decay * kk, 0.0)
        p = eye + x
        for _ in range(int(math.log2(c)) - 1):
            x = _dot(x, x, HIGHEST)
            p = p + _dot(p, x, HIGHEST)
        eg = jnp.exp(gcol)
        wm_ref[h] = _dot(p, bcol * eg * k, HIGHEST).astype(BF16)
        u0_ref[h] = _dot(p, bcol * v, HIGHEST)
        qe_ref[h] = (q * eg).astype(BF16)
        qk_ref[h] = (_dot_nt(q.astype(BF16), kb) * decay).astype(BF16)
        kwt_ref[h] = (k * jnp.exp(glcol - gcol)).T.astype(BF16)


def _gdn_prompt(qkv, ba, z, cw, prm, onorm):
    b, t, c3 = qkv.shape
    bw = c3 // 3
    hd = bw // B_HEADS
    n = GDN_STEP
    nt = t // n
    cur = lambda bi, i: (bi, jnp.minimum(i, nt - 1), 0)
    prev = lambda bi, i: (bi, jnp.maximum(i - 1, 0), 0)
    return pl.pallas_call(
        functools.partial(_gdn_prompt_kernel, bw=bw),
        grid=(b, nt + 1),
        in_specs=[pl.BlockSpec((1, n, c3), cur), pl.BlockSpec((1, n, 128), cur),
                  pl.BlockSpec((1, n, bw), prev), _const_spec(cw.shape), _const_spec(prm.shape),
                  _const_spec(onorm.shape)],
        out_specs=[pl.BlockSpec((1, n, bw), prev),
                   pl.BlockSpec((1, B_HEADS, hd, hd), lambda bi, i: (bi, 0, 0, 0))],
        out_shape=[jax.ShapeDtypeStruct((b, t, bw), BF16),
                   jax.ShapeDtypeStruct((b, B_HEADS, hd, hd), F32)],
        scratch_shapes=[pltpu.VMEM((n + 8, c3), F32),
                        pltpu.VMEM((B_HEADS, n, hd), BF16), pltpu.VMEM((B_HEADS, n, hd), F32),
                        pltpu.VMEM((B_HEADS, n, hd), BF16), pltpu.VMEM((B_HEADS, n, n), BF16),
                        pltpu.VMEM((B_HEADS, hd, n), BF16), pltpu.VMEM((n, 128), F32)],
        compiler_params=_cparams(("arbitrary", "arbitrary")),
        name="gdn_prompt",
    )(qkv, ba, z, cw, prm, onorm)


def _gdn_step_kernel(qkv_ref, cst_ref, ba_ref, z_ref, st_ref, cw_ref, prm_ref, onorm_ref,
                     o_ref, sto_ref, *, bw):
    nb = qkv_ref.shape[0]
    hd = bw // B_HEADS
    c3 = 3 * bw
    y = (cw_ref[0:1, :] * cst_ref[:, 0:c3] + cw_ref[1:2, :] * cst_ref[:, c3:2 * c3]
         + cw_ref[2:3, :] * cst_ref[:, 2 * c3:3 * c3] + cw_ref[3:4, :] * qkv_ref[...])
    y = y * _sigmoid(y)
    ba = ba_ref[...]
    beta = _sigmoid(ba)
    eg_all = jnp.exp(-jnp.exp(prm_ref[1:2, :]) * _softplus(ba + prm_ref[0:1, :]))
    row = lax.broadcasted_iota(jnp.int32, (hd, hd), 0)
    col = lax.broadcasted_iota(jnp.int32, (hd, hd), 1)
    eye = jnp.where(row == col, 1.0, 0.0)
    for h in range(B_HEADS):
        q = y[:, h * hd:(h + 1) * hd]
        k = y[:, bw + h * hd:bw + (h + 1) * hd]
        v = y[:, 2 * bw + h * hd:2 * bw + (h + 1) * hd]
        q = q * lax.rsqrt(jnp.sum(q * q, axis=-1, keepdims=True) + EPS) * (hd ** -0.5)
        k = k * lax.rsqrt(jnp.sum(k * k, axis=-1, keepdims=True) + EPS)
        qk = jnp.sum(q * k, axis=-1, keepdims=True)
        k_t = _dot_nt(eye, k, HIGHEST)
        q_t = _dot_nt(eye, q, HIGHEST)
        o_rows = []
        for j in range(nb):
            s = st_ref[j, h]
            eg = eg_all[j:j + 1, B_HEADS + h:B_HEADS + h + 1]
            bt = beta[j:j + 1, h:h + 1]
            kc = k_t[:, j:j + 1]
            ks = jnp.sum(kc * s, axis=0, keepdims=True)
            qs = jnp.sum(q_t[:, j:j + 1] * s, axis=0, keepdims=True)
            u = bt * (v[j:j + 1, :] - eg * ks)
            o_rows.append(eg * qs + qk[j:j + 1, :] * u)
            sto_ref[j, h] = eg * s + kc * u
        o = jnp.concatenate(o_rows, axis=0)
        cs = slice(h * hd, (h + 1) * hd)
        o_ref[:, cs] = (_rms(o, onorm_ref[...]) * z_ref[:, cs].astype(F32)).astype(BF16)


def _gdn_step(qkv, cst, ba, z, state, cw, prm, onorm, *, nb):
    m, c3 = qkv.shape
    bw = c3 // 3
    hd = bw // B_HEADS
    row = lambda i: (i, 0)
    st = lambda i: (i, 0, 0, 0)
    return pl.pallas_call(
        functools.partial(_gdn_step_kernel, bw=bw),
        grid=(m // nb,),
        in_specs=[pl.BlockSpec((nb, c3), row), pl.BlockSpec((nb, 3 * c3), row),
                  pl.BlockSpec((nb, 128), row), pl.BlockSpec((nb, bw), row),
                  pl.BlockSpec((nb, B_HEADS, hd, hd), st), _const_spec(cw.shape),
                  _const_spec(prm.shape), _const_spec(onorm.shape)],
        out_specs=[pl.BlockSpec((nb, bw), row), pl.BlockSpec((nb, B_HEADS, hd, hd), st)],
        out_shape=[jax.ShapeDtypeStruct((m, bw), BF16),
                   jax.ShapeDtypeStruct((m, B_HEADS, hd, hd), F32)],
        compiler_params=_cparams(("parallel",)),
        name="gdn_step",
    )(qkv, cst, ba, z, state, cw, prm, onorm)


def _mix_out_kernel(*refs, n_in):
    ins = refs[:n_in]
    x_ref, w_ref, gpost_ref, gffn_ref, x1_ref, h2_ref = refs[n_in:]
    mix = None
    r0 = 0
    for a_ref in ins:
        kk = a_ref.shape[1]
        part = _dot(a_ref[...], w_ref[r0:r0 + kk, :])
        mix = part if mix is None else mix + part
        r0 += kk
    x1 = x_ref[...] + _rms(mix, gpost_ref[...])
    x1_ref[...] = x1
    h2_ref[...] = _rms(x1, gffn_ref[...]).astype(BF16)


def _mix_out(ins, x, w, gpost, gffn, *, tm):
    m, d = x.shape
    row = lambda i: (i, 0)
    return pl.pallas_call(
        functools.partial(_mix_out_kernel, n_in=len(ins)),
        grid=(m // tm,),
        in_specs=[pl.BlockSpec((tm, a.shape[1]), row) for a in ins]
        + [pl.BlockSpec((tm, d), row), _const_spec(w.shape), _const_spec(gpost.shape),
           _const_spec(gffn.shape)],
        out_specs=[pl.BlockSpec((tm, d), row), pl.BlockSpec((tm, d), row)],
        out_shape=[jax.ShapeDtypeStruct((m, d), F32), jax.ShapeDtypeStruct((m, d), BF16)],
        compiler_params=_cparams(("parallel",)),
        name="mix_out",
    )(*ins, x, w, gpost, gffn)


def _ffn_kernel(h_ref, x_ref, wg_ref, wu_ref, wd_ref, wc_ref, bc_ref, gpost_ref, *rest,
                tiles_per_seq, fc):
    tm = h_ref.shape[0]
    ff = wg_ref.shape[1]
    if tiles_per_seq is None:
        hist_ref, x2_ref, gout_ref = rest
    else:
        x2_ref, gout_ref, carry_ref, gbuf_ref = rest

        @pl.when(pl.program_id(0) % tiles_per_seq == 0)
        def _():
            carry_ref[...] = jnp.zeros_like(carry_ref)

    hb = h_ref[...]
    acc = None
    for c0 in range(0, ff, fc):
        cs = slice(c0, c0 + fc)
        g = _dot(hb, wg_ref[:, cs])
        up = _dot(hb, wu_ref[:, cs])
        if tiles_per_seq is None:
            gout_ref[:, cs] = g
            sh2 = hist_ref[:, cs]
            sh1 = hist_ref[:, ff + c0:ff + c0 + fc]
        else:
            gbuf_ref[0:8, cs] = carry_ref[:, cs]
            gbuf_ref[8:tm + 8, cs] = g
            sh2 = gbuf_ref[6:tm + 6, cs]
            sh1 = gbuf_ref[7:tm + 7, cs]
            tail = gbuf_ref[tm:tm + 8, cs]
            carry_ref[:, cs] = tail
            gout_ref[0, :, cs] = tail
        conv = (wc_ref[0:1, cs] * sh2 + wc_ref[1:2, cs] * sh1 + wc_ref[2:3, cs] * g
                + bc_ref[:, cs])
        a = (_gelu(conv) * up).astype(BF16)
        part = _dot(a, wd_ref[cs, :])
        acc = part if acc is None else acc + part
    x2_ref[...] = x_ref[...] + _rms(acc, gpost_ref[...])


def _ffn(h2, x1, wg, wu, wd, wc, bc, gpost, *, tm, tiles_per_seq=None, hist=None, fc=256):
    m, d = x1.shape
    ff = wg.shape[1]
    row = lambda i: (i, 0)
    in_specs = [pl.BlockSpec((tm, d), row), pl.BlockSpec((tm, d), row), _const_spec(wg.shape),
                _const_spec(wu.shape), _const_spec(wd.shape), _const_spec(wc.shape),
                _const_spec(bc.shape), _const_spec(gpost.shape)]
    args = [h2, x1, wg, wu, wd, wc, bc, gpost]
    if tiles_per_seq is None:
        in_specs.append(pl.BlockSpec((tm, 2 * ff), row))
        args.append(hist)
        out_specs = [pl.BlockSpec((tm, d), row), pl.BlockSpec((tm, ff), row)]
        out_shape = [jax.ShapeDtypeStruct((m, d), F32), jax.ShapeDtypeStruct((m, ff), F32)]
        scratch = []
    else:
        out_specs = [pl.BlockSpec((tm, d), row), pl.BlockSpec((1, 8, ff), lambda i: (i, 0, 0))]
        out_shape = [jax.ShapeDtypeStruct((m, d), F32),
                     jax.ShapeDtypeStruct((m // tm, 8, ff), F32)]
        scratch = [pltpu.VMEM((8, ff), F32), pltpu.VMEM((tm + 8, ff), F32)]
    return pl.pallas_call(
        functools.partial(_ffn_kernel, tiles_per_seq=tiles_per_seq, fc=fc),
        grid=(m // tm,),
        in_specs=in_specs, out_specs=out_specs, out_shape=out_shape, scratch_shapes=scratch,
        compiler_params=_cparams(("arbitrary",)),
        name="conv_ffn",
    )(*args)


def _odd_in_kernel(x_ref, gpre_ref, w_ref, *outs, head_major, qscale):
    d = x_ref.shape[1]
    xb = _rms(x_ref[...], gpre_ref[...]).astype(BF16)
    q = _dot(xb, w_ref[:, 0:d]) * qscale
    k = _dot(xb, w_ref[:, d:2 * d])
    v = _dot(xb, w_ref[:, 2 * d:3 * d])
    if head_major:
        qh_ref, kh_ref, vh_ref, k_ref, v_ref = outs
        k_ref[...] = k
        v_ref[...] = v
        hw = d // C_HEADS
        for h in range(C_HEADS):
            cs = slice(h * hw, (h + 1) * hw)
            qh_ref[0, h] = q[:, cs].astype(BF16)
            kh_ref[0, h] = k[:, cs].astype(BF16)
            vh_ref[0, h] = v[:, cs].astype(BF16)
    else:
        q_ref, k_ref, v_ref = outs
        q_ref[...] = q
        k_ref[...] = k
        v_ref[...] = v


def _odd_in(x, gpre, w, *, tm, seq_len=None):
    m, d = x.shape
    qscale = (d // (2 * C_HEADS)) ** -0.5
    row = lambda i: (i, 0)
    head_major = seq_len is not None
    if head_major:
        tps = seq_len // tm
        hw = d // C_HEADS
        hm = lambda i: (i // tps, 0, i % tps, 0)
        out_specs = [pl.BlockSpec((1, C_HEADS, tm, hw), hm)] * 3 + [pl.BlockSpec((tm, d), row)] * 2
        out_shape = ([jax.ShapeDtypeStruct((m // seq_len, C_HEADS, seq_len, hw), BF16)] * 3
                     + [jax.ShapeDtypeStruct((m, d), F32)] * 2)
    else:
        out_specs = [pl.BlockSpec((tm, d), row)] * 3
        out_shape = [jax.ShapeDtypeStruct((m, d), F32)] * 3
    return pl.pallas_call(
        functools.partial(_odd_in_kernel, head_major=head_major, qscale=qscale),
        grid=(m // tm,),
        in_specs=[pl.BlockSpec((tm, d), row), _const_spec(gpre.shape), _const_spec(w.shape)],
        out_specs=out_specs, out_shape=out_shape,
        compiler_params=_cparams(("parallel",)),
        name="odd_in",
    )(x, gpre, w)


def _lambda_value(lam_ref, lam_init):
    l = lam_ref[...]
    a = jnp.sum(l[0:1, :] * l[1:2, :], axis=-1, keepdims=True)
    b = jnp.sum(l[2:3, :] * l[3:4, :], axis=-1, keepdims=True)
    return jnp.exp(a) - jnp.exp(b) + lam_init


def _attn_prompt_kernel(slope_ref, q_ref, k_ref, v_ref, lam_ref, subln_ref, o_ref,
                        qs_ref, m_ref, l_ref, acc_ref, *, tq, tk, lam_init):
    h = pl.program_id(1)
    iq = pl.program_id(2)
    hw = q_ref.shape[3]
    hd = hw // 2
    r = 2 * tq
    slope = slope_ref[h]

    q = q_ref[0, 0]
    lane = lax.broadcasted_iota(jnp.int32, (tq, hw), 1)
    qs_ref[0:tq, :] = jnp.where(lane < hd, q, jnp.zeros_like(q))
    qs_ref[tq:r, :] = jnp.where(lane >= hd, q, jnp.zeros_like(q))
    m_ref[...] = jnp.full_like(m_ref, NEG_BIG)
    l_ref[...] = jnp.zeros_like(l_ref)
    acc_ref[...] = jnp.zeros_like(acc_ref)

    bias = slope * lax.broadcasted_iota(jnp.int32, (1, tk), 1).astype(F32)

    def step(kc, masked):
        k0 = pl.multiple_of(kc * tk, tk)
        kb = k_ref[0, 0, pl.ds(k0, tk), :]
        vb = v_ref[0, 0, pl.ds(k0, tk), :]
        s = _dot_nt(qs_ref[...], kb) + bias
        if masked:
            rowi = lax.broadcasted_iota(jnp.int32, (r, tk), 0)
            qpos = iq * tq + jnp.where(rowi >= tq, rowi - tq, rowi)
            kpos = k0 + lax.broadcasted_iota(jnp.int32, (r, tk), 1)
            s = jnp.where(kpos <= qpos, s, NEG_BIG)
        cshift = slope * k0.astype(F32)
        m_old = m_ref[...]
        m_new = jnp.maximum(m_old, jnp.max(s, axis=-1, keepdims=True) + cshift)
        p = jnp.exp(s - (m_new - cshift))
        alpha = jnp.exp(m_old - m_new)
        l_ref[...] = alpha * l_ref[...] + jnp.sum(p, axis=-1, keepdims=True)
        acc_ref[...] = alpha * acc_ref[...] + _dot(p.astype(BF16), vb)
        m_ref[...] = m_new

    n_full = (iq * tq) // tk
    n_all = ((iq + 1) * tq + tk - 1) // tk

    def full_body(kc, carry):
        step(kc, False)
        return carry

    def diag_body(kc, carry):
        step(kc, True)
        return carry

    lax.fori_loop(0, n_full, full_body, 0)
    lax.fori_loop(n_full, n_all, diag_body, 0)

    lam = _lambda_value(lam_ref, lam_init)
    o1 = acc_ref[0:tq, :] / l_ref[0:tq, :]
    o2 = acc_ref[tq:r, :] / l_ref[tq:r, :]
    att = o1 - lam * o2
    o_ref[0] = (_rms(att, subln_ref[...]) * (1.0 - lam_init)).astype(BF16)


def _attn_prompt(slopes, qh, kh, vh, lam_p, subln, *, tq, tk, lam_init):
    b, nh, t, hw = qh.shape
    kv = lambda bi, h, iq: (bi, h, 0, 0)
    return pl.pallas_call(
        functools.partial(_attn_prompt_kernel, tq=tq, tk=tk, lam_init=lam_init),
        grid=(b, nh, t // tq),
        in_specs=[pl.BlockSpec(memory_space=pltpu.SMEM),
                  pl.BlockSpec((1, 1, tq, hw), lambda bi, h, iq: (bi, h, iq, 0)),
                  pl.BlockSpec((1, 1, t, hw), kv), pl.BlockSpec((1, 1, t, hw), kv),
                  _const_spec(lam_p.shape), _const_spec(subln.shape)],
        out_specs=pl.BlockSpec((1, tq, hw), lambda bi, h, iq: (bi, iq, h)),
        out_shape=jax.ShapeDtypeStruct((b, t, nh * hw), BF16),
        scratch_shapes=[pltpu.VMEM((2 * tq, hw), BF16), pltpu.VMEM((2 * tq, 1), F32),
                        pltpu.VMEM((2 * tq, 1), F32), pltpu.VMEM((2 * tq, hw), F32)],
        compiler_params=_cparams(("parallel", "parallel", "arbitrary")),
        name="attn_prompt",
    )(slopes, qh, kh, vh, lam_p, subln)


def _attn_decode_kernel(pt_ref, q_ref, kn_ref, vn_ref, slope_ref, expand_ref, lam_ref, subln_ref,
                        *rest, pg, n_past, lam_init):
    k_refs = rest[0:pg]
    v_refs = rest[pg:2 * pg]
    o_ref, m_ref, l_ref, acc_ref = rest[2 * pg:]
    j = pl.program_id(1)
    nj = pl.num_programs(1)
    _, d, page = k_refs[0].shape
    nh = C_HEADS
    nr = 2 * nh
    hd = d // nr
    slope = slope_ref[:, 0:1]

    rowi = lax.broadcasted_iota(jnp.int32, (nr, d), 0)
    coli = lax.broadcasted_iota(jnp.int32, (nr, d), 1)
    lo = (jnp.where(rowi >= nh, rowi - nh, rowi) * 2 + jnp.where(rowi >= nh, 1, 0)) * hd
    qbd = jnp.where((coli >= lo) & (coli < lo + hd), q_ref[0], 0.0)
    rowe = lax.broadcasted_iota(jnp.int32, (nr, nh * page), 0)
    cole = lax.broadcasted_iota(jnp.int32, (nr, nh * page), 1)
    own_head = (cole & (nh - 1)) == jnp.where(rowe >= nh, rowe - nh, rowe)

    @pl.when(j == 0)
    def _():
        m_ref[...] = jnp.full_like(m_ref, NEG_BIG)
        l_ref[...] = jnp.zeros_like(l_ref)
        acc_ref[...] = jnp.zeros_like(acc_ref)

    qb = qbd.astype(BF16)
    tpos = lax.broadcasted_iota(jnp.int32, (1, page), 1)
    for p_i in range(pg):
        kb = k_refs[p_i][0].astype(BF16)
        vb = v_refs[p_i][0].astype(BF16)
        dist = (n_past - ((j * pg + p_i) * page + tpos)).astype(F32)
        s = _dot(qb, kb) - slope * dist
        m_old = m_ref[...]
        m_new = jnp.maximum(m_old, jnp.max(s, axis=-1, keepdims=True))
        p = jnp.exp(s - m_new)
        alpha = jnp.exp(m_old - m_new)
        l_ref[...] = alpha * l_ref[...] + jnp.sum(p, axis=-1, keepdims=True)
        pexp = _dot(p.astype(BF16), expand_ref[...])
        pbig = jnp.where(own_head, pexp, 0.0).astype(BF16)
        acc_ref[...] = alpha * acc_ref[...] + _dot(pbig, vb)
        m_ref[...] = m_new

    @pl.when(j == nj - 1)
    def _():
        s = jnp.sum(qbd * kn_ref[0], axis=-1, keepdims=True)
        m_old = m_ref[...]
        m_new = jnp.maximum(m_old, s)
        p = jnp.exp(s - m_new)
        alpha = jnp.exp(m_old - m_new)
        l = alpha * l_ref[...] + p
        vn = vn_ref[0]
        acc = alpha * acc_ref[...] + p * jnp.concatenate([vn, vn], axis=0)
        lam = _lambda_value(lam_ref, lam_init)
        att = acc[0:nh, :] / l[0:nh, :] - lam * (acc[nh:nr, :] / l[nh:nr, :])
        o_ref[0] = (_rms(att, subln_ref[...]) * (1.0 - lam_init)).astype(BF16)


def _attn_decode(page_table, q, kn, vn, slope_tile, lam_p, subln, cache_kt, cache_v2, *, pg,
                 lam_init):
    bs, n_pages = page_table.shape
    _, d, page = cache_kt.shape
    hw = d // C_HEADS
    q3, kn3 = (a.reshape(bs, 1, d) for a in (q, kn))
    vn3 = vn.reshape(bs, C_HEADS, hw)
    expand = (jnp.arange(page * C_HEADS)[None, :] // C_HEADS
              == jnp.arange(page)[:, None]).astype(BF16)
    row = lambda b, j, pt: (b, 0, 0)
    const2 = lambda b, j, pt: (0, 0)

    def page_map(p_i):
        return lambda b, j, pt: (pt[b, j * pg + p_i], 0, 0)

    k_specs = [pl.BlockSpec((1, d, page), page_map(p_i)) for p_i in range(pg)]
    v_specs = [pl.BlockSpec((1, page * C_HEADS, hw), page_map(p_i)) for p_i in range(pg)]
    grid_spec = pltpu.PrefetchScalarGridSpec(
        num_scalar_prefetch=1,
        grid=(bs, n_pages // pg),
        in_specs=[pl.BlockSpec((1, 1, d), row), pl.BlockSpec((1, 1, d), row),
                  pl.BlockSpec((1, C_HEADS, hw), row), pl.BlockSpec(slope_tile.shape, const2),
                  pl.BlockSpec(expand.shape, const2), pl.BlockSpec(lam_p.shape, const2),
                  pl.BlockSpec(subln.shape, const2)] + k_specs + v_specs,
        out_specs=pl.BlockSpec((1, C_HEADS, hw), row),
        scratch_shapes=[pltpu.VMEM((2 * C_HEADS, 1), F32), pltpu.VMEM((2 * C_HEADS, 1), F32),
                        pltpu.VMEM((2 * C_HEADS, hw), F32)],
    )
    out = pl.pallas_call(
        functools.partial(_attn_decode_kernel, pg=pg, n_past=n_pages * page, lam_init=lam_init),
        grid_spec=grid_spec,
        out_shape=jax.ShapeDtypeStruct((bs, C_HEADS, hw), BF16),
        compiler_params=_cparams(("parallel", "arbitrary")),
        name="attn_decode",
    )(page_table, q3, kn3, vn3, slope_tile, expand, lam_p, subln, *([cache_kt] * pg),
      *([cache_v2] * pg))
    return out.reshape(bs, d)


def _row_tile(m, pref):
    tm = min(pref, m)
    assert m % tm == 0, (m, tm)
    return tm


def kernel(x_prompt, x_sample, cache_k, cache_v, page_table, state_gdn, state_gdn_conv, state_ffn_conv, norm_mix_pre, norm_mix_post, norm_ffn_pre, norm_ffn_post, w_in_even, a_v_norm, a_w_s, a_b_s, b_conv_w, b_a_log, b_dt_bias, b_out_norm, w_out_even, w_in_odd, c_lambda, c_subln, w_out_odd, w_ffn_gate, w_ffn_up, w_ffn_conv, b_ffn_conv, w_ffn_down):
    b, t, d = x_prompt.shape
    bs = x_sample.shape[0]
    assert x_sample.shape[1] == 1 and t % GDN_STEP == 0
    aw = a_v_norm.shape[-1]
    bw = b_conv_w.shape[-1] // 3
    ff = w_ffn_gate.shape[-1]
    n_split = 2 * aw + 4 * bw
    hw = d // C_HEADS
    _, n_pool, page, _, _, hd = cache_k.shape
    row2 = lambda a: a.reshape(1, -1)

    w_even = w_in_even[0, :, :n_split].astype(BF16)
    w_ba = jnp.pad(w_in_even[0, :, n_split:], ((0, 0), (0, 128 - 2 * B_HEADS))).astype(BF16)
    prm = jnp.zeros((8, 128), F32)
    prm = prm.at[0, B_HEADS:2 * B_HEADS].set(b_dt_bias[0]).at[1, B_HEADS:2 * B_HEADS].set(b_a_log[0])
    onorm = row2(b_out_norm[0])
    w_oe = w_out_even[0].astype(BF16)
    w_odd = w_in_odd[0].astype(BF16)
    w_oo = w_out_odd[0].astype(BF16)
    wg = w_ffn_gate.astype(BF16)
    wu = w_ffn_up.astype(BF16)
    wd = w_ffn_down.astype(BF16)
    slopes = jnp.exp2(-8.0 * jnp.arange(1, C_HEADS + 1, dtype=F32) / C_HEADS)
    slope_tile = jnp.broadcast_to(jnp.tile(slopes, 2)[:, None], (2 * C_HEADS, 128))
    lam_init = 0.8 - 0.6 * math.exp(-0.3 * 1)
    ws_step = row2(jnp.repeat(a_w_s[0, :, 0, 0], aw // A_GROUPS))
    bs_step = row2(jnp.repeat(a_b_s[0, :, 0], aw // A_GROUPS))
    bs_t = a_b_s[0].T

    def ffn_layer(layer, h2, x1, **kw):
        return _ffn(h2, x1, wg[layer], wu[layer], wd[layer], w_ffn_conv[layer],
                    row2(b_ffn_conv[layer]), row2(norm_ffn_post[layer]), **kw)

    tm = _row_tile(b * t, 512)
    tps = t // tm
    xp = x_prompt.reshape(b * t, d)
    a_out, _, qkv, z, ba = _even_in(xp, row2(norm_mix_pre[0]), w_even, w_ba, row2(a_v_norm[0]),
                                    a_w_s[0], bs_t, chunk=A_CHUNK, tm=tm)
    qkv3 = qkv.reshape(b, t, 3 * bw)
    o, gdn_state_p = _gdn_prompt(qkv3, ba.reshape(b, t, 128), z.reshape(b, t, bw), b_conv_w[0],
                                 prm, onorm)
    x1, h2 = _mix_out([a_out, o.reshape(b * t, bw)], xp, w_oe, row2(norm_mix_post[0]),
                      row2(norm_ffn_pre[0]), tm=tm)
    x2, gt0 = ffn_layer(0, h2, x1, tm=tm, tiles_per_seq=tps)
    qh, kh, vh, k_p, v_p = _odd_in(x2, row2(norm_mix_pre[1]), w_odd, tm=tm, seq_len=t)
    tq = _row_tile(t, 512)
    att = _attn_prompt(slopes, qh, kh, vh, c_lambda[0], row2(c_subln[0]), tq=tq, tk=tq,
                       lam_init=lam_init)
    x3, h4 = _mix_out([att.reshape(b * t, d)], x2, w_oo, row2(norm_mix_post[1]),
                      row2(norm_ffn_pre[1]), tm=tm)
    y_p, gt1 = ffn_layer(1, h4, x3, tm=tm, tiles_per_seq=tps)
    ffn_conv_p = jnp.stack([g.reshape(b, tps, 8, ff)[:, -1, 8 - (FFN_CONV - 1):, :]
                            for g in (gt0, gt1)])

    xs = x_sample.reshape(bs, d)
    a_out_s, v_s, qkv_s, z_s, ba_s = _even_in(xs, row2(norm_mix_pre[0]), w_even, w_ba,
                                              row2(a_v_norm[0]), ws_step, bs_step, chunk=1, tm=bs)
    o_s, gdn_state_s = _gdn_step(qkv_s, state_gdn_conv[0].reshape(bs, -1), ba_s, z_s, state_gdn[0],
                                 b_conv_w[0], prm, onorm, nb=min(16, bs))
    x1s, h2s = _mix_out([a_out_s, o_s], xs, w_oe, row2(norm_mix_post[0]), row2(norm_ffn_pre[0]),
                        tm=bs)
    x2s, g0s = ffn_layer(0, h2s, x1s, tm=bs, hist=state_ffn_conv[0].reshape(bs, -1))
    q_s, k_s, v_sn = _odd_in(x2s, row2(norm_mix_pre[1]), w_odd, tm=bs)
    n_pages = page_table.shape[1]
    pg = 4 if n_pages % 4 == 0 else (2 if n_pages % 2 == 0 else 1)
    cache_kt = jnp.transpose(cache_k[0], (0, 2, 3, 4, 1)).reshape(n_pool, d, page)
    cache_v2 = cache_v[0].reshape(n_pool, page * C_HEADS, hw)
    att_s = _attn_decode(page_table, q_s, k_s, v_sn, slope_tile, c_lambda[0], row2(c_subln[0]),
                         cache_kt, cache_v2, pg=pg, lam_init=lam_init)
    x3s, h4s = _mix_out([att_s], x2s, w_oo, row2(norm_mix_post[1]), row2(norm_ffn_pre[1]), tm=bs)
    y_s, g1s = ffn_layer(1, h4s, x3s, tm=bs, hist=state_ffn_conv[1].reshape(bs, -1))
    ffn_conv_s = jnp.stack([jnp.concatenate([state_ffn_conv[l][:, 1:], g[:, None, :]], axis=1)
                            for l, g in ((0, g0s), (1, g1s))])

    return (y_p.reshape(b, t, d), y_s.reshape(bs, 1, d),
            gdn_state_p[None], gdn_state_s[None],
            qkv3[:, t - (B_CONV - 1):, :][None],
            jnp.concatenate([state_gdn_conv[0][:, 1:], qkv_s[:, None, :]], axis=1)[None],
            v_s.reshape(1, bs, 1, aw),
            k_p.reshape(1, b, t, C_HEADS, 2, hd), v_p.reshape(1, b, t, C_HEADS, hw),
            k_s.reshape(1, bs, 1, C_HEADS, 2, hd), v_sn.reshape(1, bs, 1, C_HEADS, hw),
            ffn_conv_p, ffn_conv_s)
```

```python
import functools
import math

import jax
import jax.numpy as jnp
from jax import lax
from jax.experimental import pallas as pl
from jax.experimental.pallas import tpu as pltpu

F32 = jnp.float32
BF16 = jnp.bfloat16
EPS = 1e-6

A_GROUPS = 4
A_CHUNK = 128
B_HEADS = 4
B_CONV = 4
GDN_STEP = 128
GDN_CHUNK = 64
C_HEADS = 8
FFN_CONV = 3
NEG_BIG = -1e30

VMEM_LIMIT_BYTES = 56 * 1024 * 1024
HIGHEST = lax.Precision.HIGHEST


def _cparams(sem):
    return pltpu.CompilerParams(dimension_semantics=sem, vmem_limit_bytes=VMEM_LIMIT_BYTES)


def _gelu(x):
    return 0.5 * x * (1.0 + jnp.tanh(0.7978845608028654 * (x + 0.044715 * (x * x * x))))


def _sigmoid(x):
    return 1.0 / (1.0 + jnp.exp(-x))


def _softplus(x):
    return jnp.maximum(x, 0.0) + jnp.log(1.0 + jnp.exp(-jnp.abs(x)))


def _rms(x, gain):
    return x * lax.rsqrt(jnp.mean(x * x, axis=-1, keepdims=True) + EPS) * gain


def _dot(a, b, precision=None):
    return jnp.dot(a, b, preferred_element_type=F32, precision=precision)


def _dot_nt(a, b, precision=None):
    return lax.dot_general(a, b, (((1,), (1,)), ((), ())), preferred_element_type=F32,
                           precision=precision)


def _split(a):
    hi = a.astype(BF16)
    return hi, (a - hi.astype(F32)).astype(BF16)


def _dot3(a, b):
    ah, al = _split(a)
    bh, bl = _split(b)
    return _dot(ah, bh) + _dot(ah, bl) + _dot(al, bh)


def _const_spec(shape):
    nd = len(shape)
    return pl.BlockSpec(shape, lambda *_: (0,) * nd, pipeline_mode=pl.Buffered(1))


def _even_in_kernel(x_ref, gpre_ref, w_ref, wba_ref, avn_ref, ws_ref, bs_ref,
                    aout_ref, v_ref, qkv_ref, z_ref, ba_ref, *, chunk, aw, bw):
    tm = x_ref.shape[0]
    xb = _rms(x_ref[...], gpre_ref[...]).astype(BF16)
    u = _gelu(_dot(xb, w_ref[:, 0:aw]))
    v = _rms(_gelu(_dot(xb, w_ref[:, aw:2 * aw])), avn_ref[...])
    v_ref[...] = v
    qkv_ref[...] = _dot(xb, w_ref[:, 2 * aw:2 * aw + 3 * bw])
    zz = _dot(xb, w_ref[:, 2 * aw + 3 * bw:2 * aw + 4 * bw])
    z_ref[...] = (zz * _sigmoid(zz)).astype(BF16)
    ba_ref[...] = _dot(xb, wba_ref[...])
    gd = aw // A_GROUPS
    if chunk == 1:
        aout_ref[...] = (u * (v * ws_ref[...] + bs_ref[...])).astype(BF16)
    else:
        row = lax.broadcasted_iota(jnp.int32, (chunk, chunk), 0)
        col = lax.broadcasted_iota(jnp.int32, (chunk, chunk), 1)
        for g in range(A_GROUPS):
            wt = jnp.where(col <= row, ws_ref[g], 0.0).astype(BF16)
            bcol = bs_ref[:, g:g + 1]
            for n in range(tm // chunk):
                rs = slice(n * chunk, (n + 1) * chunk)
                cs = slice(g * gd, (g + 1) * gd)
                mixed = _dot(wt, v[rs, cs].astype(BF16)) + bcol
                aout_ref[rs, cs] = (u[rs, cs] * mixed).astype(BF16)


def _even_in(x, gpre, w_main, w_ba, avn, ws, bs, *, chunk, tm):
    m, d = x.shape
    aw = avn.shape[-1]
    bw = (w_main.shape[1] - 2 * aw) // 4
    row = lambda i: (i, 0)
    return pl.pallas_call(
        functools.partial(_even_in_kernel, chunk=chunk, aw=aw, bw=bw),
        grid=(m // tm,),
        in_specs=[pl.BlockSpec((tm, d), row), _const_spec(gpre.shape), _const_spec(w_main.shape),
                  _const_spec(w_ba.shape), _const_spec(avn.shape), _const_spec(ws.shape),
                  _const_spec(bs.shape)],
        out_specs=[pl.BlockSpec((tm, aw), row), pl.BlockSpec((tm, aw), row),
                   pl.BlockSpec((tm, 3 * bw), row), pl.BlockSpec((tm, bw), row),
                   pl.BlockSpec((tm, 128), row)],
        out_shape=[jax.ShapeDtypeStruct((m, aw), BF16), jax.ShapeDtypeStruct((m, aw), F32),
                   jax.ShapeDtypeStruct((m, 3 * bw), F32), jax.ShapeDtypeStruct((m, bw), BF16),
                   jax.ShapeDtypeStruct((m, 128), F32)],
        compiler_params=_cparams(("parallel",)),
        name="even_in",
    )(x, gpre, w_main, w_ba, avn, ws, bs)


def _gdn_prompt_kernel(qkv_ref, ba_ref, z_ref, cw_ref, prm_ref, onorm_ref,
                       o_ref, st_ref,
                       ext_ref, wm_ref, u0_ref, qe_ref, qk_ref, kwt_ref, egl_ref, *, bw):
    i = pl.program_id(1)
    n = GDN_STEP
    c = GDN_CHUNK
    hd = bw // B_HEADS

    @pl.when(i == 0)
    def _():
        st_ref[...] = jnp.zeros_like(st_ref)
        ext_ref[n:n + 8, :] = jnp.zeros((8, ext_ref.shape[1]), F32)
        wm_ref[...] = jnp.zeros_like(wm_ref)
        u0_ref[...] = jnp.zeros_like(u0_ref)
        qe_ref[...] = jnp.zeros_like(qe_ref)
        qk_ref[...] = jnp.zeros_like(qk_ref)
        kwt_ref[...] = jnp.zeros_like(kwt_ref)
        egl_ref[...] = jnp.zeros_like(egl_ref)

    heads = range(B_HEADS)
    zero_half = jnp.zeros((c, hd), F32)
    egl = [(egl_ref[0:1, B_HEADS + h:B_HEADS + h + 1], egl_ref[c:c + 1, B_HEADS + h:B_HEADS + h + 1])
           for h in heads]
    state = {"s": [st_ref[0, h] for h in heads]}

    def advance_stage_u(r0):
        state["sb"] = [s.astype(BF16) for s in state["s"]]
        state["u"] = [u0_ref[h, r0:r0 + c, :] - _dot(wm_ref[h, r0:r0 + c, :], state["sb"][h])
                      for h in heads]

    def advance_stage_s(r0, part):
        halves = [[u, zero_half] if part == 0 else [zero_half, u] for u in state["u"]]
        uf = [jnp.concatenate(hv, axis=0).astype(BF16) for hv in halves]
        o = [_dot(qe_ref[h, r0:r0 + c, :], state["sb"][h]) + _dot(qk_ref[h, r0:r0 + c, :], uf[h])
             for h in heads]
        state["s"] = [egl[h][part] * state["s"][h] + _dot(kwt_ref[h], uf[h]) for h in heads]
        return o

    ext_ref[5:8, :] = ext_ref[n + 5:n + 8, :]
    cur = qkv_ref[0]
    ext_ref[8:n + 8, :] = cur
    y = (cw_ref[0:1, :] * ext_ref[5:n + 5, :] + cw_ref[1:2, :] * ext_ref[6:n + 6, :]
         + cw_ref[2:3, :] * ext_ref[7:n + 7, :] + cw_ref[3:4, :] * cur)
    y = y * _sigmoid(y)

    ba = ba_ref[0]
    beta = _sigmoid(ba)
    gfull = -jnp.exp(prm_ref[1:2, :]) * _softplus(ba + prm_ref[0:1, :])
    row = lax.broadcasted_iota(jnp.int32, (n, n), 0)
    col = lax.broadcasted_iota(jnp.int32, (n, n), 1)
    same = (row < c) == (col < c)
    incl = same & (col <= row)
    strict = same & (col < row)
    eye = jnp.where(row == col, 1.0, 0.0)

    advance_stage_u(0)
    g_hi, g_lo = _split(gfull)
    ones_incl = jnp.where(incl, 1.0, 0.0).astype(BF16)
    ones_same = jnp.where(same, 1.0, 0.0).astype(BF16)
    gcum = _dot(ones_incl, g_hi) + _dot(ones_incl, g_lo)
    glast = _dot(ones_same, g_hi) + _dot(ones_same, g_lo)
    gcum_t = gcum.T

    qs, ks, vs, kbs, decays = [], [], [], [], []
    for h in heads:
        q = y[:, h * hd:(h + 1) * hd]
        k = y[:, bw + h * hd:bw + (h + 1) * hd]
        qs.append(q * lax.rsqrt(jnp.sum(q * q, axis=-1, keepdims=True) + EPS) * (hd ** -0.5))
        ks.append(k * lax.rsqrt(jnp.sum(k * k, axis=-1, keepdims=True) + EPS))
        vs.append(y[:, 2 * bw + h * hd:2 * bw + (h + 1) * hd])
        kbs.append(ks[h].astype(BF16))
        gcol = gcum[:, B_HEADS + h:B_HEADS + h + 1]
        grow = gcum_t[B_HEADS + h:B_HEADS + h + 1, :]
        decays.append(jnp.where(incl, jnp.exp(jnp.where(incl, gcol - grow, 0.0)), 0.0))
    bcols = [beta[:, h:h + 1] for h in heads]
    gcols = [gcum[:, B_HEADS + h:B_HEADS + h + 1] for h in heads]
    glcols = [glast[:, B_HEADS + h:B_HEADS + h + 1] for h in heads]

    kk = [_dot_nt(kbs[h], kbs[h]) for h in heads]
    o_a = advance_stage_s(0, 0)
    x = [-jnp.where(strict, bcols[h] * decays[h] * kk[h], 0.0) for h in heads]
    p = [eye + x[h] for h in heads]
    o_b = None
    for it in range(int(math.log2(c)) - 1):
        x = [_dot3(x[h], x[h]) for h in heads]
        if it == 0:
            advance_stage_u(c)
        p = [p[h] + _dot3(p[h], x[h]) for h in heads]
        if it == 1:
            o_b = advance_stage_s(c, 1)
    egs = [jnp.exp(gcols[h]) for h in heads]
    wm = [_dot3(p[h], bcols[h] * egs[h] * ks[h]) for h in heads]
    u0 = [_dot3(p[h], bcols[h] * vs[h]) for h in heads]
    qk = [_dot_nt(qs[h].astype(BF16), kbs[h]) for h in heads]

    for h in heads:
        st_ref[0, h] = state["s"][h]
        o = jnp.concatenate([o_a[h], o_b[h]], axis=0)
        cs = slice(h * hd, (h + 1) * hd)
        o_ref[0, :, cs] = (_rms(o, onorm_ref[...]) * z_ref[0, :, cs].astype(F32)).astype(BF16)
        wm_ref[h] = wm[h].astype(BF16)
        u0_ref[h] = u0[h]
        qe_ref[h] = (qs[h] * egs[h]).astype(BF16)
        qk_ref[h] = (qk[h] * decays[h]).astype(BF16)
        kwt_ref[h] = (ks[h] * jnp.exp(glcols[h] - gcols[h])).T.astype(BF16)
    egl_ref[...] = jnp.exp(glast)


def _gdn_prompt(qkv, ba, z, cw, prm, onorm):
    b, t, c3 = qkv.shape
    bw = c3 // 3
    hd = bw // B_HEADS
    n = GDN_STEP
    nt = t // n
    cur = lambda bi, i: (bi, jnp.minimum(i, nt - 1), 0)
    prev = lambda bi, i: (bi, jnp.maximum(i - 1, 0), 0)
    return pl.pallas_call(
        functools.partial(_gdn_prompt_kernel, bw=bw),
        grid=(b, nt + 1),
        in_specs=[pl.BlockSpec((1, n, c3), cur), pl.BlockSpec((1, n, 128), cur),
                  pl.BlockSpec((1, n, bw), prev), _const_spec(cw.shape), _const_spec(prm.shape),
                  _const_spec(onorm.shape)],
        out_specs=[pl.BlockSpec((1, n, bw), prev),
                   pl.BlockSpec((1, B_HEADS, hd, hd), lambda bi, i: (bi, 0, 0, 0))],
        out_shape=[jax.ShapeDtypeStruct((b, t, bw), BF16),
                   jax.ShapeDtypeStruct((b, B_HEADS, hd, hd), F32)],
        scratch_shapes=[pltpu.VMEM((n + 8, c3), F32),
                        pltpu.VMEM((B_HEADS, n, hd), BF16), pltpu.VMEM((B_HEADS, n, hd), F32),
                        pltpu.VMEM((B_HEADS, n, hd), BF16), pltpu.VMEM((B_HEADS, n, n), BF16),
                        pltpu.VMEM((B_HEADS, hd, n), BF16), pltpu.VMEM((n, 128), F32)],
        compiler_params=_cparams(("arbitrary", "arbitrary")),
        name="gdn_prompt",
    )(qkv, ba, z, cw, prm, onorm)


def _gdn_step_kernel(qkv_ref, cst_ref, ba_ref, z_ref, st_ref, cw_ref, prm_ref, onorm_ref,
                     o_ref, sto_ref, *, bw):
    nb = qkv_ref.shape[0]
    hd = bw // B_HEADS
    c3 = 3 * bw
    y = (cw_ref[0:1, :] * cst_ref[:, 0:c3] + cw_ref[1:2, :] * cst_ref[:, c3:2 * c3]
         + cw_ref[2:3, :] * cst_ref[:, 2 * c3:3 * c3] + cw_ref[3:4, :] * qkv_ref[...])
    y = y * _sigmoid(y)
    ba = ba_ref[...]
    beta = _sigmoid(ba)
    eg_all = jnp.exp(-jnp.exp(prm_ref[1:2, :]) * _softplus(ba + prm_ref[0:1, :]))
    row = lax.broadcasted_iota(jnp.int32, (hd, hd), 0)
    col = lax.broadcasted_iota(jnp.int32, (hd, hd), 1)
    eye = jnp.where(row == col, 1.0, 0.0)
    for h in range(B_HEADS):
        q = y[:, h * hd:(h + 1) * hd]
        k = y[:, bw + h * hd:bw + (h + 1) * hd]
        v = y[:, 2 * bw + h * hd:2 * bw + (h + 1) * hd]
        q = q * lax.rsqrt(jnp.sum(q * q, axis=-1, keepdims=True) + EPS) * (hd ** -0.5)
        k = k * lax.rsqrt(jnp.sum(k * k, axis=-1, keepdims=True) + EPS)
        qk = jnp.sum(q * k, axis=-1, keepdims=True)
        k_t = _dot_nt(eye, k, HIGHEST)
        q_t = _dot_nt(eye, q, HIGHEST)
        o_rows = []
        for j in range(nb):
            s = st_ref[j, h]
            eg = eg_all[j:j + 1, B_HEADS + h:B_HEADS + h + 1]
            bt = beta[j:j + 1, h:h + 1]
            kc = k_t[:, j:j + 1]
            ks = jnp.sum(kc * s, axis=0, keepdims=True)
            qs = jnp.sum(q_t[:, j:j + 1] * s, axis=0, keepdims=True)
            u = bt * (v[j:j + 1, :] - eg * ks)
            o_rows.append(eg * qs + qk[j:j + 1, :] * u)
            sto_ref[j, h] = eg * s + kc * u
        o = jnp.concatenate(o_rows, axis=0)
        cs = slice(h * hd, (h + 1) * hd)
        o_ref[:, cs] = (_rms(o, onorm_ref[...]) * z_ref[:, cs].astype(F32)).astype(BF16)


def _gdn_step(qkv, cst, ba, z, state, cw, prm, onorm, *, nb):
    m, c3 = qkv.shape
    bw = c3 // 3
    hd = bw // B_HEADS
    row = lambda i: (i, 0)
    st = lambda i: (i, 0, 0, 0)
    return pl.pallas_call(
        functools.partial(_gdn_step_kernel, bw=bw),
        grid=(m // nb,),
        in_specs=[pl.BlockSpec((nb, c3), row), pl.BlockSpec((nb, 3 * c3), row),
                  pl.BlockSpec((nb, 128), row), pl.BlockSpec((nb, bw), row),
                  pl.BlockSpec((nb, B_HEADS, hd, hd), st), _const_spec(cw.shape),
                  _const_spec(prm.shape), _const_spec(onorm.shape)],
        out_specs=[pl.BlockSpec((nb, bw), row), pl.BlockSpec((nb, B_HEADS, hd, hd), st)],
        out_shape=[jax.ShapeDtypeStruct((m, bw), BF16),
                   jax.ShapeDtypeStruct((m, B_HEADS, hd, hd), F32)],
        compiler_params=_cparams(("parallel",)),
        name="gdn_step",
    )(qkv, cst, ba, z, state, cw, prm, onorm)


def _mix_out_kernel(*refs, n_in):
    ins = refs[:n_in]
    x_ref, w_ref, gpost_ref, gffn_ref, x1_ref, h2_ref = refs[n_in:]
    mix = None
    r0 = 0
    for a_ref in ins:
        kk = a_ref.shape[1]
        part = _dot(a_ref[...], w_ref[r0:r0 + kk, :])
        mix = part if mix is None else mix + part
        r0 += kk
    x1 = x_ref[...] + _rms(mix, gpost_ref[...])
    x1_ref[...] = x1
    h2_ref[...] = _rms(x1, gffn_ref[...]).astype(BF16)


def _mix_out(ins, x, w, gpost, gffn, *, tm):
    m, d = x.shape
    row = lambda i: (i, 0)
    return pl.pallas_call(
        functools.partial(_mix_out_kernel, n_in=len(ins)),
        grid=(m // tm,),
        in_specs=[pl.BlockSpec((tm, a.shape[1]), row) for a in ins]
        + [pl.BlockSpec((tm, d), row), _const_spec(w.shape), _const_spec(gpost.shape),
           _const_spec(gffn.shape)],
        out_specs=[pl.BlockSpec((tm, d), row), pl.BlockSpec((tm, d), row)],
        out_shape=[jax.ShapeDtypeStruct((m, d), F32), jax.ShapeDtypeStruct((m, d), BF16)],
        compiler_params=_cparams(("parallel",)),
        name="mix_out",
    )(*ins, x, w, gpost, gffn)


def _ffn_kernel(h_ref, x_ref, wg_ref, wu_ref, wd_ref, wc_ref, bc_ref, gpost_ref, *rest,
                tiles_per_seq, fc):
    tm = h_ref.shape[0]
    ff = wg_ref.shape[1]
    if tiles_per_seq is None:
        hist_ref, x2_ref, gout_ref = rest
    else:
        x2_ref, gout_ref, carry_ref, gbuf_ref = rest

        @pl.when(pl.program_id(0) % tiles_per_seq == 0)
        def _():
            carry_ref[...] = jnp.zeros_like(carry_ref)

    hb = h_ref[...]
    acc = None
    for c0 in range(0, ff, fc):
        cs = slice(c0, c0 + fc)
        g = _dot(hb, wg_ref[:, cs])
        up = _dot(hb, wu_ref[:, cs])
        if tiles_per_seq is None:
            gout_ref[:, cs] = g
            sh2 = hist_ref[:, cs]
            sh1 = hist_ref[:, ff + c0:ff + c0 + fc]
        else:
            gbuf_ref[0:8, cs] = carry_ref[:, cs]
            gbuf_ref[8:tm + 8, cs] = g
            sh2 = gbuf_ref[6:tm + 6, cs]
            sh1 = gbuf_ref[7:tm + 7, cs]
            tail = gbuf_ref[tm:tm + 8, cs]
            carry_ref[:, cs] = tail
            gout_ref[0, :, cs] = tail
        conv = (wc_ref[0:1, cs] * sh2 + wc_ref[1:2, cs] * sh1 + wc_ref[2:3, cs] * g
                + bc_ref[:, cs])
        a = (_gelu(conv) * up).astype(BF16)
        part = _dot(a, wd_ref[cs, :])
        acc = part if acc is None else acc + part
    x2_ref[...] = x_ref[...] + _rms(acc, gpost_ref[...])


def _ffn(h2, x1, wg, wu, wd, wc, bc, gpost, *, tm, tiles_per_seq=None, hist=None, fc=256):
    m, d = x1.shape
    ff = wg.shape[1]
    row = lambda i: (i, 0)
    in_specs = [pl.BlockSpec((tm, d), row), pl.BlockSpec((tm, d), row), _const_spec(wg.shape),
                _const_spec(wu.shape), _const_spec(wd.shape), _const_spec(wc.shape),
                _const_spec(bc.shape), _const_spec(gpost.shape)]
    args = [h2, x1, wg, wu, wd, wc, bc, gpost]
    if tiles_per_seq is None:
        in_specs.append(pl.BlockSpec((tm, 2 * ff), row))
        args.append(hist)
        out_specs = [pl.BlockSpec((tm, d), row), pl.BlockSpec((tm, ff), row)]
        out_shape = [jax.ShapeDtypeStruct((m, d), F32), jax.ShapeDtypeStruct((m, ff), F32)]
        scratch = []
    else:
        out_specs = [pl.BlockSpec((tm, d), row), pl.BlockSpec((1, 8, ff), lambda i: (i, 0, 0))]
        out_shape = [jax.ShapeDtypeStruct((m, d), F32),
                     jax.ShapeDtypeStruct((m // tm, 8, ff), F32)]
        scratch = [pltpu.VMEM((8, ff), F32), pltpu.VMEM((tm + 8, ff), F32)]
    return pl.pallas_call(
        functools.partial(_ffn_kernel, tiles_per_seq=tiles_per_seq, fc=fc),
        grid=(m // tm,),
        in_specs=in_specs, out_specs=out_specs, out_shape=out_shape, scratch_shapes=scratch,
        compiler_params=_cparams(("arbitrary",)),
        name="conv_ffn",
    )(*args)


def _odd_in_kernel(x_ref, gpre_ref, w_ref, *outs, head_major, qscale):
    d = x_ref.shape[1]
    xb = _rms(x_ref[...], gpre_ref[...]).astype(BF16)
    q = _dot(xb, w_ref[:, 0:d]) * qscale
    k = _dot(xb, w_ref[:, d:2 * d])
    v = _dot(xb, w_ref[:, 2 * d:3 * d])
    if head_major:
        qh_ref, kh_ref, vt_ref, kt_ref, v_ref = outs
        kt_ref[0] = k.T
        v_ref[...] = v
        vt = v.T
        hw = d // C_HEADS
        for h in range(C_HEADS):
            cs = slice(h * hw, (h + 1) * hw)
            qh_ref[0, h] = q[:, cs].astype(BF16)
            kh_ref[0, h] = k[:, cs].astype(BF16)
            vt_ref[0, h, 0] = vt[cs, :].astype(BF16)
    else:
        q_ref, k_ref, v_ref = outs
        q_ref[...] = q
        k_ref[...] = k
        v_ref[...] = v


def _odd_in(x, gpre, w, *, tm, seq_len=None):
    m, d = x.shape
    qscale = (d // (2 * C_HEADS)) ** -0.5
    row = lambda i: (i, 0)
    head_major = seq_len is not None
    if head_major:
        tps = seq_len // tm
        hw = d // C_HEADS
        nb = m // seq_len
        hm = lambda i: (i // tps, 0, i % tps, 0)
        out_specs = [pl.BlockSpec((1, C_HEADS, tm, hw), hm)] * 2 + [
            pl.BlockSpec((1, C_HEADS, 1, hw, tm), lambda i: (i // tps, 0, i % tps, 0, 0)),
            pl.BlockSpec((1, d, tm), lambda i: (i // tps, 0, i % tps)),
            pl.BlockSpec((tm, d), row)]
        out_shape = [jax.ShapeDtypeStruct((nb, C_HEADS, seq_len, hw), BF16)] * 2 + [
            jax.ShapeDtypeStruct((nb, C_HEADS, tps, hw, tm), BF16),
            jax.ShapeDtypeStruct((nb, d, seq_len), F32),
            jax.ShapeDtypeStruct((m, d), F32)]
    else:
        out_specs = [pl.BlockSpec((tm, d), row)] * 3
        out_shape = [jax.ShapeDtypeStruct((m, d), F32)] * 3
    return pl.pallas_call(
        functools.partial(_odd_in_kernel, head_major=head_major, qscale=qscale),
        grid=(m // tm,),
        in_specs=[pl.BlockSpec((tm, d), row), _const_spec(gpre.shape), _const_spec(w.shape)],
        out_specs=out_specs, out_shape=out_shape,
        compiler_params=_cparams(("parallel",)),
        name="odd_in",
    )(x, gpre, w)


def _lambda_value(lam_ref, lam_init):
    l = lam_ref[...]
    a = jnp.sum(l[0:1, :] * l[1:2, :], axis=-1, keepdims=True)
    b = jnp.sum(l[2:3, :] * l[3:4, :], axis=-1, keepdims=True)
    return jnp.exp(a) - jnp.exp(b) + lam_init


def _attn_prompt_kernel(slope_ref, q_ref, k_ref, vt_ref, lam_ref, subln_ref, o_ref,
                        qs_ref, s_ref, m_ref, l_ref, acc_ref, *, tq, tk, qb, lam_init):
    h = pl.program_id(1)
    iq = pl.program_id(2)
    hw = q_ref.shape[3]
    hd = hw // 2
    r = 2 * tq
    slope = slope_ref[h]

    q = q_ref[0, 0]
    lane = lax.broadcasted_iota(jnp.int32, (tq, hw), 1)
    qs_ref[0:tq, :] = jnp.where(lane < hd, q, jnp.zeros_like(q))
    qs_ref[tq:r, :] = jnp.where(lane >= hd, q, jnp.zeros_like(q))
    m_ref[...] = jnp.full_like(m_ref, NEG_BIG)
    l_ref[...] = jnp.zeros_like(l_ref)
    acc_ref[...] = jnp.zeros_like(acc_ref)

    key_i = lax.broadcasted_iota(jnp.int32, (tk, qb), 0)
    qry_i = lax.broadcasted_iota(jnp.int32, (tk, qb), 1)
    bias = slope * key_i.astype(F32)

    def keys(kc):
        return k_ref[0, 0, pl.ds(pl.multiple_of(kc * tk, tk), tk), :]

    def step(kc, kb_next, masked):
        vtb = vt_ref[0, 0, kc]
        cshift = slope * (kc * tk).astype(F32)
        for c0 in range(0, r, qb):
            cols = slice(c0, c0 + qb)
            s = s_ref[:, cols] + bias
            if kb_next is not None:
                s_ref[:, cols] = _dot_nt(kb_next, qs_ref[cols, :])
            if masked:
                s = jnp.where(kc * tk + key_i <= iq * tq + (c0 % tq) + qry_i, s, NEG_BIG)
            m_old = m_ref[:, cols]
            m_new = jnp.maximum(m_old, jnp.max(s, axis=0, keepdims=True) + cshift)
            p = jnp.exp(s - (m_new - cshift))
            alpha = jnp.exp(m_old - m_new)
            l_ref[:, cols] = alpha * l_ref[:, cols] + jnp.sum(p, axis=0, keepdims=True)
            acc_ref[:, cols] = alpha * acc_ref[:, cols] + _dot(vtb, p.astype(BF16))
            m_ref[:, cols] = m_new

    s_ref[...] = _dot_nt(keys(0), qs_ref[...])

    def body(kc, carry):
        step(kc, keys(kc + 1), False)
        return carry

    lax.fori_loop(0, iq, body, 0)
    step(iq, None, True)

    lam = _lambda_value(lam_ref, lam_init)
    o1 = acc_ref[:, 0:tq] / l_ref[:, 0:tq]
    o2 = acc_ref[:, tq:r] / l_ref[:, tq:r]
    att = (o1 - lam * o2).T
    o_ref[0] = (_rms(att, subln_ref[...]) * (1.0 - lam_init)).astype(BF16)


def _attn_prompt(slopes, qh, kh, vt, lam_p, subln, *, tq, lam_init):
    b, nh, t, hw = qh.shape
    tk = vt.shape[4]
    assert tq == tk
    return pl.pallas_call(
        functools.partial(_attn_prompt_kernel, tq=tq, tk=tk, qb=min(256, tq), lam_init=lam_init),
        grid=(b, nh, t // tq),
        in_specs=[pl.BlockSpec(memory_space=pltpu.SMEM),
                  pl.BlockSpec((1, 1, tq, hw), lambda bi, h, iq: (bi, h, iq, 0)),
                  pl.BlockSpec((1, 1, t, hw), lambda bi, h, iq: (bi, h, 0, 0)),
                  pl.BlockSpec((1, 1, t // tk, hw, tk), lambda bi, h, iq: (bi, h, 0, 0, 0)),
                  _const_spec(lam_p.shape), _const_spec(subln.shape)],
        out_specs=pl.BlockSpec((1, tq, hw), lambda bi, h, iq: (bi, iq, h)),
        out_shape=jax.ShapeDtypeStruct((b, t, nh * hw), BF16),
        scratch_shapes=[pltpu.VMEM((2 * tq, hw), BF16), pltpu.VMEM((tk, 2 * tq), F32),
                        pltpu.VMEM((1, 2 * tq), F32), pltpu.VMEM((1, 2 * tq), F32),
                        pltpu.VMEM((hw, 2 * tq), F32)],
        compiler_params=_cparams(("parallel", "parallel", "arbitrary")),
        name="attn_prompt",
    )(slopes, qh, kh, vt, lam_p, subln)


def _attn_decode_kernel(pt_ref, q_ref, kn_ref, vn_ref, slope_ref, expand_ref, lam_ref, subln_ref,
                        *rest, pg, n_past, lam_init):
    k_refs = rest[0:pg]
    v_refs = rest[pg:2 * pg]
    o_ref, m_ref, l_ref, acc_ref = rest[2 * pg:]
    j = pl.program_id(1)
    nj = pl.num_programs(1)
    _, d, page = k_refs[0].shape
    nh = C_HEADS
    nr = 2 * nh
    hd = d // nr
    slope = slope_ref[:, 0:1]

    rowi = lax.broadcasted_iota(jnp.int32, (nr, d), 0)
    coli = lax.broadcasted_iota(jnp.int32, (nr, d), 1)
    lo = (jnp.where(rowi >= nh, rowi - nh, rowi) * 2 + jnp.where(rowi >= nh, 1, 0)) * hd
    qbd = jnp.where((coli >= lo) & (coli < lo + hd), q_ref[0], 0.0)
    rowe = lax.broadcasted_iota(jnp.int32, (nr, nh * page), 0)
    cole = lax.broadcasted_iota(jnp.int32, (nr, nh * page), 1)
    own_head = (cole & (nh - 1)) == jnp.where(rowe >= nh, rowe - nh, rowe)

    @pl.when(j == 0)
    def _():
        m_ref[...] = jnp.full_like(m_ref, NEG_BIG)
        l_ref[...] = jnp.zeros_like(l_ref)
        acc_ref[...] = jnp.zeros_like(acc_ref)

    qb = qbd.astype(BF16)
    tpos = lax.broadcasted_iota(jnp.int32, (1, page), 1)
    s_parts = []
    for p_i in range(pg):
        dist = (n_past - ((j * pg + p_i) * page + tpos)).astype(F32)
        s_parts.append(_dot(qb, k_refs[p_i][0].astype(BF16)) - slope * dist)
    s = jnp.concatenate(s_parts, axis=1)
    m_old = m_ref[...]
    m_new = jnp.maximum(m_old, jnp.max(s, axis=-1, keepdims=True))
    p = jnp.exp(s - m_new).astype(BF16)
    alpha = jnp.exp(m_old - m_new)
    l_ref[...] = alpha * l_ref[...] + jnp.sum(p.astype(F32), axis=-1, keepdims=True)
    m_ref[...] = m_new
    pbig = [jnp.where(own_head, _dot(p[:, p_i * page:(p_i + 1) * page], expand_ref[...]), 0.0)
            for p_i in range(pg)]
    pv = [_dot(pbig[p_i].astype(BF16), v_refs[p_i][0].astype(BF16)) for p_i in range(pg)]
    acc_ref[...] = alpha * acc_ref[...] + sum(pv[1:], pv[0])

    @pl.when(j == nj - 1)
    def _():
        s = jnp.sum(qbd * kn_ref[0], axis=-1, keepdims=True)
        m_old = m_ref[...]
        m_new = jnp.maximum(m_old, s)
        p = jnp.exp(s - m_new)
        alpha = jnp.exp(m_old - m_new)
        l = alpha * l_ref[...] + p
        vn = vn_ref[0]
        acc = alpha * acc_ref[...] + p * jnp.concatenate([vn, vn], axis=0)
        lam = _lambda_value(lam_ref, lam_init)
        att = acc[0:nh, :] / l[0:nh, :] - lam * (acc[nh:nr, :] / l[nh:nr, :])
        o_ref[0] = (_rms(att, subln_ref[...]) * (1.0 - lam_init)).astype(BF16)


def _attn_decode(page_table, q, kn, vn, slope_tile, lam_p, subln, cache_kt, cache_v2, *, pg,
                 lam_init):
    bs, n_pages = page_table.shape
    _, d, page = cache_kt.shape
    hw = d // C_HEADS
    q3, kn3 = (a.reshape(bs, 1, d) for a in (q, kn))
    vn3 = vn.reshape(bs, C_HEADS, hw)
    expand = (jnp.arange(page * C_HEADS)[None, :] // C_HEADS
              == jnp.arange(page)[:, None]).astype(BF16)
    row = lambda b, j, pt: (b, 0, 0)
    const2 = lambda b, j, pt: (0, 0)

    def page_map(p_i):
        return lambda b, j, pt: (pt[b, j * pg + p_i], 0, 0)

    k_specs = [pl.BlockSpec((1, d, page), page_map(p_i)) for p_i in range(pg)]
    v_specs = [pl.BlockSpec((1, page * C_HEADS, hw), page_map(p_i)) for p_i in range(pg)]
    grid_spec = pltpu.PrefetchScalarGridSpec(
        num_scalar_prefetch=1,
        grid=(bs, n_pages // pg),
        in_specs=[pl.BlockSpec((1, 1, d), row), pl.BlockSpec((1, 1, d), row),
                  pl.BlockSpec((1, C_HEADS, hw), row), pl.BlockSpec(slope_tile.shape, const2),
                  pl.BlockSpec(expand.shape, const2), pl.BlockSpec(lam_p.shape, const2),
                  pl.BlockSpec(subln.shape, const2)] + k_specs + v_specs,
        out_specs=pl.BlockSpec((1, C_HEADS, hw), row),
        scratch_shapes=[pltpu.VMEM((2 * C_HEADS, 1), F32), pltpu.VMEM((2 * C_HEADS, 1), F32),
                        pltpu.VMEM((2 * C_HEADS, hw), F32)],
    )
    out = pl.pallas_call(
        functools.partial(_attn_decode_kernel, pg=pg, n_past=n_pages * page, lam_init=lam_init),
        grid_spec=grid_spec,
        out_shape=jax.ShapeDtypeStruct((bs, C_HEADS, hw), BF16),
        compiler_params=_cparams(("parallel", "arbitrary")),
        name="attn_decode",
    )(page_table, q3, kn3, vn3, slope_tile, expand, lam_p, subln, *([cache_kt] * pg),
      *([cache_v2] * pg))
    return out.reshape(bs, d)


def _row_tile(m, pref):
    tm = min(pref, m)
    assert m % tm == 0, (m, tm)
    return tm


def kernel(x_prompt, x_sample, cache_k, cache_v, page_table, state_gdn, state_gdn_conv, state_ffn_conv, norm_mix_pre, norm_mix_post, norm_ffn_pre, norm_ffn_post, w_in_even, a_v_norm, a_w_s, a_b_s, b_conv_w, b_a_log, b_dt_bias, b_out_norm, w_out_even, w_in_odd, c_lambda, c_subln, w_out_odd, w_ffn_gate, w_ffn_up, w_ffn_conv, b_ffn_conv, w_ffn_down):
    b, t, d = x_prompt.shape
    bs = x_sample.shape[0]
    assert x_sample.shape[1] == 1 and t % GDN_STEP == 0
    aw = a_v_norm.shape[-1]
    bw = b_conv_w.shape[-1] // 3
    ff = w_ffn_gate.shape[-1]
    n_split = 2 * aw + 4 * bw
    hw = d // C_HEADS
    _, n_pool, page, _, _, hd = cache_k.shape
    row2 = lambda a: a.reshape(1, -1)

    w_even = w_in_even[0, :, :n_split].astype(BF16)
    w_ba = jnp.pad(w_in_even[0, :, n_split:], ((0, 0), (0, 128 - 2 * B_HEADS))).astype(BF16)
    prm = jnp.zeros((8, 128), F32)
    prm = prm.at[0, B_HEADS:2 * B_HEADS].set(b_dt_bias[0]).at[1, B_HEADS:2 * B_HEADS].set(b_a_log[0])
    onorm = row2(b_out_norm[0])
    w_oe = w_out_even[0].astype(BF16)
    w_odd = w_in_odd[0].astype(BF16)
    w_oo = w_out_odd[0].astype(BF16)
    wg = w_ffn_gate.astype(BF16)
    wu = w_ffn_up.astype(BF16)
    wd = w_ffn_down.astype(BF16)
    slopes = jnp.exp2(-8.0 * jnp.arange(1, C_HEADS + 1, dtype=F32) / C_HEADS)
    slope_tile = jnp.broadcast_to(jnp.tile(slopes, 2)[:, None], (2 * C_HEADS, 128))
    lam_init = 0.8 - 0.6 * math.exp(-0.3 * 1)
    ws_step = row2(jnp.repeat(a_w_s[0, :, 0, 0], aw // A_GROUPS))
    bs_step = row2(jnp.repeat(a_b_s[0, :, 0], aw // A_GROUPS))
    bs_t = a_b_s[0].T

    def ffn_layer(layer, h2, x1, **kw):
        return _ffn(h2, x1, wg[layer], wu[layer], wd[layer], w_ffn_conv[layer],
                    row2(b_ffn_conv[layer]), row2(norm_ffn_post[layer]), **kw)

    tm = _row_tile(b * t, 512)
    tps = t // tm
    xp = x_prompt.reshape(b * t, d)
    a_out, _, qkv, z, ba = _even_in(xp, row2(norm_mix_pre[0]), w_even, w_ba, row2(a_v_norm[0]),
                                    a_w_s[0], bs_t, chunk=A_CHUNK, tm=tm)
    qkv3 = qkv.reshape(b, t, 3 * bw)
    o, gdn_state_p = _gdn_prompt(qkv3, ba.reshape(b, t, 128), z.reshape(b, t, bw), b_conv_w[0],
                                 prm, onorm)
    x1, h2 = _mix_out([a_out, o.reshape(b * t, bw)], xp, w_oe, row2(norm_mix_post[0]),
                      row2(norm_ffn_pre[0]), tm=tm)
    x2, gt0 = ffn_layer(0, h2, x1, tm=tm, tiles_per_seq=tps)
    qh, kh, vt, kt_p, v_p = _odd_in(x2, row2(norm_mix_pre[1]), w_odd, tm=tm, seq_len=t)
    att = _attn_prompt(slopes, qh, kh, vt, c_lambda[0], row2(c_subln[0]), tq=tm, lam_init=lam_init)
    x3, h4 = _mix_out([att.reshape(b * t, d)], x2, w_oo, row2(norm_mix_post[1]),
                      row2(norm_ffn_pre[1]), tm=tm)
    y_p, gt1 = ffn_layer(1, h4, x3, tm=tm, tiles_per_seq=tps)
    ffn_conv_p = jnp.stack([g.reshape(b, tps, 8, ff)[:, -1, 8 - (FFN_CONV - 1):, :]
                            for g in (gt0, gt1)])

    xs = x_sample.reshape(bs, d)
    a_out_s, v_s, qkv_s, z_s, ba_s = _even_in(xs, row2(norm_mix_pre[0]), w_even, w_ba,
                                              row2(a_v_norm[0]), ws_step, bs_step, chunk=1, tm=bs)
    o_s, gdn_state_s = _gdn_step(qkv_s, state_gdn_conv[0].reshape(bs, -1), ba_s, z_s, state_gdn[0],
                                 b_conv_w[0], prm, onorm, nb=min(16, bs))
    x1s, h2s = _mix_out([a_out_s, o_s], xs, w_oe, row2(norm_mix_post[0]), row2(norm_ffn_pre[0]),
                        tm=bs)
    x2s, g0s = ffn_layer(0, h2s, x1s, tm=bs, hist=state_ffn_conv[0].reshape(bs, -1))
    q_s, k_s, v_sn = _odd_in(x2s, row2(norm_mix_pre[1]), w_odd, tm=bs)
    n_pages = page_table.shape[1]
    pg = next(c for c in (8, 4, 2, 1) if n_pages % c == 0)
    cache_kt = jnp.transpose(cache_k[0], (0, 2, 3, 4, 1)).reshape(n_pool, d, page)
    cache_v2 = cache_v[0].reshape(n_pool, page * C_HEADS, hw)
    att_s = _attn_decode(page_table, q_s, k_s, v_sn, slope_tile, c_lambda[0], row2(c_subln[0]),
                         cache_kt, cache_v2, pg=pg, lam_init=lam_init)
    x3s, h4s = _mix_out([att_s], x2s, w_oo, row2(norm_mix_post[1]), row2(norm_ffn_pre[1]), tm=bs)
    y_s, g1s = ffn_layer(1, h4s, x3s, tm=bs, hist=state_ffn_conv[1].reshape(bs, -1))
    ffn_conv_s = jnp.stack([jnp.concatenate([state_ffn_conv[l][:, 1:], g[:, None, :]], axis=1)
                            for l, g in ((0, g0s), (1, g1s))])

    return (y_p.reshape(b, t, d), y_s.reshape(bs, 1, d),
            gdn_state_p[None], gdn_state_s[None],
            qkv3[:, t - (B_CONV - 1):, :][None],
            jnp.concatenate([state_gdn_conv[0][:, 1:], qkv_s[:, None, :]], axis=1)[None],
            v_s.reshape(1, bs, 1, aw),
            jnp.transpose(kt_p.reshape(1, b, C_HEADS, 2, hd, t), (0, 1, 5, 2, 3, 4)),
            v_p.reshape(1, b, t, C_HEADS, hw),
            k_s.reshape(1, bs, 1, C_HEADS, 2, hd), v_sn.reshape(1, bs, 1, C_HEADS, hw),
            ffn_conv_p, ffn_conv_s)
```

```python
import functools
import math

import jax
import jax.numpy as jnp
from jax import lax
from jax.experimental import pallas as pl
from jax.experimental.pallas import tpu as pltpu

F32 = jnp.float32
BF16 = jnp.bfloat16
EPS = 1e-6

A_GROUPS = 4
A_CHUNK = 128
B_HEADS = 4
B_CONV = 4
GDN_STEP = 128
GDN_CHUNK = 64
C_HEADS = 8
FFN_CONV = 3
NEG_BIG = -1e30
LOG2E = 1.4426950408889634
POS_SPLITS = 3

VMEM_LIMIT_BYTES = 56 * 1024 * 1024
HIGHEST = lax.Precision.HIGHEST


def _cparams(sem):
    return pltpu.CompilerParams(dimension_semantics=sem, vmem_limit_bytes=VMEM_LIMIT_BYTES)


def _gelu(x):
    return 0.5 * x * (1.0 + jnp.tanh(0.7978845608028654 * (x + 0.044715 * (x * x * x))))


def _sigmoid(x):
    return 1.0 / (1.0 + jnp.exp(-x))


def _softplus(x):
    return jnp.maximum(x, 0.0) + jnp.log(1.0 + jnp.exp(-jnp.abs(x)))


def _rms(x, gain):
    return x * lax.rsqrt(jnp.mean(x * x, axis=-1, keepdims=True) + EPS) * gain


def _dot(a, b, precision=None):
    return jnp.dot(a, b, preferred_element_type=F32, precision=precision)


def _dot_nt(a, b, precision=None):
    return lax.dot_general(a, b, (((1,), (1,)), ((), ())), preferred_element_type=F32,
                           precision=precision)


def _split(a):
    hi = a.astype(BF16)
    return hi, (a - hi.astype(F32)).astype(BF16)


def _dot3(a, b):
    ah, al = _split(a)
    bh, bl = _split(b)
    return _dot(ah, bh) + _dot(ah, bl) + _dot(al, bh)


def _const_spec(shape):
    nd = len(shape)
    return pl.BlockSpec(shape, lambda *_: (0,) * nd, pipeline_mode=pl.Buffered(1))


def _even_in_kernel(x_ref, gpre_ref, w_ref, wba_ref, avn_ref, ws_ref, bs_ref,
                    aout_ref, v_ref, qkv_ref, z_ref, ba_ref, *, chunk, aw, bw):
    tm = x_ref.shape[0]
    xb = _rms(x_ref[...], gpre_ref[...]).astype(BF16)
    u = _gelu(_dot(xb, w_ref[:, 0:aw]))
    v = _rms(_gelu(_dot(xb, w_ref[:, aw:2 * aw])), avn_ref[...])
    v_ref[...] = v
    qkv_ref[...] = _dot(xb, w_ref[:, 2 * aw:2 * aw + 3 * bw])
    zz = _dot(xb, w_ref[:, 2 * aw + 3 * bw:2 * aw + 4 * bw])
    z_ref[...] = (zz * _sigmoid(zz)).astype(BF16)
    ba_ref[...] = _dot(xb, wba_ref[...])
    gd = aw // A_GROUPS
    if chunk == 1:
        aout_ref[...] = (u * (v * ws_ref[...] + bs_ref[...])).astype(BF16)
    else:
        row = lax.broadcasted_iota(jnp.int32, (chunk, chunk), 0)
        col = lax.broadcasted_iota(jnp.int32, (chunk, chunk), 1)
        for g in range(A_GROUPS):
            wt = jnp.where(col <= row, ws_ref[g], 0.0).astype(BF16)
            bcol = bs_ref[:, g:g + 1]
            for n in range(tm // chunk):
                rs = slice(n * chunk, (n + 1) * chunk)
                cs = slice(g * gd, (g + 1) * gd)
                mixed = _dot(wt, v[rs, cs].astype(BF16)) + bcol
                aout_ref[rs, cs] = (u[rs, cs] * mixed).astype(BF16)


def _even_in(x, gpre, w_main, w_ba, avn, ws, bs, *, chunk, tm):
    m, d = x.shape
    aw = avn.shape[-1]
    bw = (w_main.shape[1] - 2 * aw) // 4
    row = lambda i: (i, 0)
    return pl.pallas_call(
        functools.partial(_even_in_kernel, chunk=chunk, aw=aw, bw=bw),
        grid=(m // tm,),
        in_specs=[pl.BlockSpec((tm, d), row), _const_spec(gpre.shape), _const_spec(w_main.shape),
                  _const_spec(w_ba.shape), _const_spec(avn.shape), _const_spec(ws.shape),
                  _const_spec(bs.shape)],
        out_specs=[pl.BlockSpec((tm, aw), row), pl.BlockSpec((tm, aw), row),
                   pl.BlockSpec((tm, 3 * bw), row), pl.BlockSpec((tm, bw), row),
                   pl.BlockSpec((tm, 128), row)],
        out_shape=[jax.ShapeDtypeStruct((m, aw), BF16), jax.ShapeDtypeStruct((m, aw), F32),
                   jax.ShapeDtypeStruct((m, 3 * bw), F32), jax.ShapeDtypeStruct((m, bw), BF16),
                   jax.ShapeDtypeStruct((m, 128), F32)],
        compiler_params=_cparams(("parallel",)),
        name="even_in",
    )(x, gpre, w_main, w_ba, avn, ws, bs)


def _gdn_prompt_kernel(qkv_ref, ba_ref, z_ref, cw_ref, prm_ref, onorm_ref,
                       o_ref, st_ref,
                       ext_ref, wm_ref, u0_ref, qe_ref, qk_ref, kwt_ref, egl_ref, *, bw):
    i = pl.program_id(1)
    n = GDN_STEP
    c = GDN_CHUNK
    hd = bw // B_HEADS

    @pl.when(i == 0)
    def _():
        st_ref[...] = jnp.zeros_like(st_ref)
        ext_ref[n:n + 8, :] = jnp.zeros((8, ext_ref.shape[1]), F32)
        wm_ref[...] = jnp.zeros_like(wm_ref)
        u0_ref[...] = jnp.zeros_like(u0_ref)
        qe_ref[...] = jnp.zeros_like(qe_ref)
        qk_ref[...] = jnp.zeros_like(qk_ref)
        kwt_ref[...] = jnp.zeros_like(kwt_ref)
        egl_ref[...] = jnp.zeros_like(egl_ref)

    heads = range(B_HEADS)
    zero_half = jnp.zeros((c, hd), F32)
    egl = [(egl_ref[0:1, B_HEADS + h:B_HEADS + h + 1], egl_ref[c:c + 1, B_HEADS + h:B_HEADS + h + 1])
           for h in heads]
    state = {"s": [st_ref[0, h] for h in heads]}

    def advance_stage_u(r0):
        state["sb"] = [s.astype(BF16) for s in state["s"]]
        state["u"] = [u0_ref[h, r0:r0 + c, :] - _dot(wm_ref[h, r0:r0 + c, :], state["sb"][h])
                      for h in heads]

    def advance_stage_s(r0, part):
        halves = [[u, zero_half] if part == 0 else [zero_half, u] for u in state["u"]]
        uf = [jnp.concatenate(hv, axis=0).astype(BF16) for hv in halves]
        o = [_dot(qe_ref[h, r0:r0 + c, :], state["sb"][h]) + _dot(qk_ref[h, r0:r0 + c, :], uf[h])
             for h in heads]
        state["s"] = [egl[h][part] * state["s"][h] + _dot(kwt_ref[h], uf[h]) for h in heads]
        return o

    ext_ref[5:8, :] = ext_ref[n + 5:n + 8, :]
    cur = qkv_ref[0]
    ext_ref[8:n + 8, :] = cur
    y = (cw_ref[0:1, :] * ext_ref[5:n + 5, :] + cw_ref[1:2, :] * ext_ref[6:n + 6, :]
         + cw_ref[2:3, :] * ext_ref[7:n + 7, :] + cw_ref[3:4, :] * cur)
    y = y * _sigmoid(y)

    ba = ba_ref[0]
    beta = _sigmoid(ba)
    gfull = -jnp.exp(prm_ref[1:2, :]) * _softplus(ba + prm_ref[0:1, :])
    row = lax.broadcasted_iota(jnp.int32, (n, n), 0)
    col = lax.broadcasted_iota(jnp.int32, (n, n), 1)
    same = (row < c) == (col < c)
    incl = same & (col <= row)
    strict = same & (col < row)
    eye = jnp.where(row == col, 1.0, 0.0)

    advance_stage_u(0)
    g_hi, g_lo = _split(gfull)
    ones_incl = jnp.where(incl, 1.0, 0.0).astype(BF16)
    ones_same = jnp.where(same, 1.0, 0.0).astype(BF16)
    gcum = _dot(ones_incl, g_hi) + _dot(ones_incl, g_lo)
    glast = _dot(ones_same, g_hi) + _dot(ones_same, g_lo)
    gcum_t = gcum.T

    qs, ks, vs, kbs, decays = [], [], [], [], []
    for h in heads:
        q = y[:, h * hd:(h + 1) * hd]
        k = y[:, bw + h * hd:bw + (h + 1) * hd]
        qs.append(q * lax.rsqrt(jnp.sum(q * q, axis=-1, keepdims=True) + EPS) * (hd ** -0.5))
        ks.append(k * lax.rsqrt(jnp.sum(k * k, axis=-1, keepdims=True) + EPS))
        vs.append(y[:, 2 * bw + h * hd:2 * bw + (h + 1) * hd])
        kbs.append(ks[h].astype(BF16))
        gcol = gcum[:, B_HEADS + h:B_HEADS + h + 1]
        grow = gcum_t[B_HEADS + h:B_HEADS + h + 1, :]
        decays.append(jnp.where(incl, jnp.exp(jnp.where(incl, gcol - grow, 0.0)), 0.0))
    bcols = [beta[:, h:h + 1] for h in heads]
    gcols = [gcum[:, B_HEADS + h:B_HEADS + h + 1] for h in heads]
    glcols = [glast[:, B_HEADS + h:B_HEADS + h + 1] for h in heads]

    kk = [_dot_nt(kbs[h], kbs[h]) for h in heads]
    o_a = advance_stage_s(0, 0)
    x = [-jnp.where(strict, bcols[h] * decays[h] * kk[h], 0.0) for h in heads]
    p = [eye + x[h] for h in heads]
    o_b = None
    for it in range(int(math.log2(c)) - 1):
        x = [_dot3(x[h], x[h]) for h in heads]
        if it == 0:
            advance_stage_u(c)
        p = [p[h] + _dot3(p[h], x[h]) for h in heads]
        if it == 1:
            o_b = advance_stage_s(c, 1)
    egs = [jnp.exp(gcols[h]) for h in heads]
    wm = [_dot3(p[h], bcols[h] * egs[h] * ks[h]) for h in heads]
    u0 = [_dot3(p[h], bcols[h] * vs[h]) for h in heads]
    qk = [_dot_nt(qs[h].astype(BF16), kbs[h]) for h in heads]

    for h in heads:
        st_ref[0, h] = state["s"][h]
        o = jnp.concatenate([o_a[h], o_b[h]], axis=0)
        cs = slice(h * hd, (h + 1) * hd)
        o_ref[0, :, cs] = (_rms(o, onorm_ref[...]) * z_ref[0, :, cs].astype(F32)).astype(BF16)
        wm_ref[h] = wm[h].astype(BF16)
        u0_ref[h] = u0[h]
        qe_ref[h] = (qs[h] * egs[h]).astype(BF16)
        qk_ref[h] = (qk[h] * decays[h]).astype(BF16)
        kwt_ref[h] = (ks[h] * jnp.exp(glcols[h] - gcols[h])).T.astype(BF16)
    egl_ref[...] = jnp.exp(glast)


def _gdn_prompt(qkv, ba, z, cw, prm, onorm):
    b, t, c3 = qkv.shape
    bw = c3 // 3
    hd = bw // B_HEADS
    n = GDN_STEP
    nt = t // n
    cur = lambda bi, i: (bi, jnp.minimum(i, nt - 1), 0)
    prev = lambda bi, i: (bi, jnp.maximum(i - 1, 0), 0)
    return pl.pallas_call(
        functools.partial(_gdn_prompt_kernel, bw=bw),
        grid=(b, nt + 1),
        in_specs=[pl.BlockSpec((1, n, c3), cur), pl.BlockSpec((1, n, 128), cur),
                  pl.BlockSpec((1, n, bw), prev), _const_spec(cw.shape), _const_spec(prm.shape),
                  _const_spec(onorm.shape)],
        out_specs=[pl.BlockSpec((1, n, bw), prev),
                   pl.BlockSpec((1, B_HEADS, hd, hd), lambda bi, i: (bi, 0, 0, 0))],
        out_shape=[jax.ShapeDtypeStruct((b, t, bw), BF16),
                   jax.ShapeDtypeStruct((b, B_HEADS, hd, hd), F32)],
        scratch_shapes=[pltpu.VMEM((n + 8, c3), F32),
                        pltpu.VMEM((B_HEADS, n, hd), BF16), pltpu.VMEM((B_HEADS, n, hd), F32),
                        pltpu.VMEM((B_HEADS, n, hd), BF16), pltpu.VMEM((B_HEADS, n, n), BF16),
                        pltpu.VMEM((B_HEADS, hd, n), BF16), pltpu.VMEM((n, 128), F32)],
        compiler_params=_cparams(("arbitrary", "arbitrary")),
        name="gdn_prompt",
    )(qkv, ba, z, cw, prm, onorm)


def _gdn_step_kernel(qkv_ref, cst_ref, ba_ref, z_ref, st_ref, cw_ref, prm_ref, onorm_ref,
                     o_ref, sto_ref, *, bw):
    nb = qkv_ref.shape[0]
    hd = bw // B_HEADS
    c3 = 3 * bw
    y = (cw_ref[0:1, :] * cst_ref[:, 0:c3] + cw_ref[1:2, :] * cst_ref[:, c3:2 * c3]
         + cw_ref[2:3, :] * cst_ref[:, 2 * c3:3 * c3] + cw_ref[3:4, :] * qkv_ref[...])
    y = y * _sigmoid(y)
    ba = ba_ref[...]
    beta = _sigmoid(ba)
    eg_all = jnp.exp(-jnp.exp(prm_ref[1:2, :]) * _softplus(ba + prm_ref[0:1, :]))
    row = lax.broadcasted_iota(jnp.int32, (hd, hd), 0)
    col = lax.broadcasted_iota(jnp.int32, (hd, hd), 1)
    eye = jnp.where(row == col, 1.0, 0.0)
    for h in range(B_HEADS):
        q = y[:, h * hd:(h + 1) * hd]
        k = y[:, bw + h * hd:bw + (h + 1) * hd]
        v = y[:, 2 * bw + h * hd:2 * bw + (h + 1) * hd]
        q = q * lax.rsqrt(jnp.sum(q * q, axis=-1, keepdims=True) + EPS) * (hd ** -0.5)
        k = k * lax.rsqrt(jnp.sum(k * k, axis=-1, keepdims=True) + EPS)
        qk = jnp.sum(q * k, axis=-1, keepdims=True)
        k_t = _dot_nt(eye, k, HIGHEST)
        q_t = _dot_nt(eye, q, HIGHEST)
        o_rows = []
        for j in range(nb):
            s = st_ref[j, h]
            eg = eg_all[j:j + 1, B_HEADS + h:B_HEADS + h + 1]
            bt = beta[j:j + 1, h:h + 1]
            kc = k_t[:, j:j + 1]
            ks = jnp.sum(kc * s, axis=0, keepdims=True)
            qs = jnp.sum(q_t[:, j:j + 1] * s, axis=0, keepdims=True)
            u = bt * (v[j:j + 1, :] - eg * ks)
            o_rows.append(eg * qs + qk[j:j + 1, :] * u)
            sto_ref[j, h] = eg * s + kc * u
        o = jnp.concatenate(o_rows, axis=0)
        cs = slice(h * hd, (h + 1) * hd)
        o_ref[:, cs] = (_rms(o, onorm_ref[...]) * z_ref[:, cs].astype(F32)).astype(BF16)


def _gdn_step(qkv, cst, ba, z, state, cw, prm, onorm, *, nb):
    m, c3 = qkv.shape
    bw = c3 // 3
    hd = bw // B_HEADS
    row = lambda i: (i, 0)
    st = lambda i: (i, 0, 0, 0)
    return pl.pallas_call(
        functools.partial(_gdn_step_kernel, bw=bw),
        grid=(m // nb,),
        in_specs=[pl.BlockSpec((nb, c3), row), pl.BlockSpec((nb, 3 * c3), row),
                  pl.BlockSpec((nb, 128), row), pl.BlockSpec((nb, bw), row),
                  pl.BlockSpec((nb, B_HEADS, hd, hd), st), _const_spec(cw.shape),
                  _const_spec(prm.shape), _const_spec(onorm.shape)],
        out_specs=[pl.BlockSpec((nb, bw), row), pl.BlockSpec((nb, B_HEADS, hd, hd), st)],
        out_shape=[jax.ShapeDtypeStruct((m, bw), BF16),
                   jax.ShapeDtypeStruct((m, B_HEADS, hd, hd), F32)],
        compiler_params=_cparams(("parallel",)),
        name="gdn_step",
    )(qkv, cst, ba, z, state, cw, prm, onorm)


def _mix_out_kernel(*refs, n_in):
    ins = refs[:n_in]
    x_ref, w_ref, gpost_ref, gffn_ref, x1_ref, h2_ref = refs[n_in:]
    mix = None
    r0 = 0
    for a_ref in ins:
        kk = a_ref.shape[1]
        part = _dot(a_ref[...], w_ref[r0:r0 + kk, :])
        mix = part if mix is None else mix + part
        r0 += kk
    x1 = x_ref[...] + _rms(mix, gpost_ref[...])
    x1_ref[...] = x1
    h2_ref[...] = _rms(x1, gffn_ref[...]).astype(BF16)


def _mix_out(ins, x, w, gpost, gffn, *, tm):
    m, d = x.shape
    row = lambda i: (i, 0)
    return pl.pallas_call(
        functools.partial(_mix_out_kernel, n_in=len(ins)),
        grid=(m // tm,),
        in_specs=[pl.BlockSpec((tm, a.shape[1]), row) for a in ins]
        + [pl.BlockSpec((tm, d), row), _const_spec(w.shape), _const_spec(gpost.shape),
           _const_spec(gffn.shape)],
        out_specs=[pl.BlockSpec((tm, d), row), pl.BlockSpec((tm, d), row)],
        out_shape=[jax.ShapeDtypeStruct((m, d), F32), jax.ShapeDtypeStruct((m, d), BF16)],
        compiler_params=_cparams(("parallel",)),
        name="mix_out",
    )(*ins, x, w, gpost, gffn)


def _ffn_kernel(h_ref, x_ref, wg_ref, wu_ref, wd_ref, wc_ref, bc_ref, gpost_ref, *rest,
                tiles_per_seq, fc):
    tm = h_ref.shape[0]
    ff = wg_ref.shape[1]
    if tiles_per_seq is None:
        hist_ref, x2_ref, gout_ref = rest
    else:
        x2_ref, gout_ref, carry_ref, gbuf_ref = rest

        @pl.when(pl.program_id(0) % tiles_per_seq == 0)
        def _():
            carry_ref[...] = jnp.zeros_like(carry_ref)

    hb = h_ref[...]

    chunks = [slice(c0, min(c0 + fc, ff)) for c0 in range(0, ff, fc)]

    def gate_up(cs):
        return _dot(hb, wg_ref[:, cs]), _dot(hb, wu_ref[:, cs])

    acc = None
    ahead = gate_up(chunks[0])
    for ci, cs in enumerate(chunks):
        g, up = ahead
        if ci + 1 < len(chunks):
            ahead = gate_up(chunks[ci + 1])
        if tiles_per_seq is None:
            gout_ref[:, cs] = g
            sh2 = hist_ref[:, cs]
            sh1 = hist_ref[:, ff + cs.start:ff + cs.stop]
        else:
            gbuf_ref[0:8, cs] = carry_ref[:, cs]
            gbuf_ref[8:tm + 8, cs] = g
            sh2 = gbuf_ref[6:tm + 6, cs]
            sh1 = gbuf_ref[7:tm + 7, cs]
            tail = gbuf_ref[tm:tm + 8, cs]
            carry_ref[:, cs] = tail
            gout_ref[0, :, cs] = tail
        conv = (wc_ref[0:1, cs] * sh2 + wc_ref[1:2, cs] * sh1 + wc_ref[2:3, cs] * g
                + bc_ref[:, cs])
        a = (_gelu(conv) * up).astype(BF16)
        part = _dot(a, wd_ref[cs, :])
        acc = part if acc is None else acc + part
    x2_ref[...] = x_ref[...] + _rms(acc, gpost_ref[...])


def _ffn(h2, x1, wg, wu, wd, wc, bc, gpost, *, tm, tiles_per_seq=None, hist=None, fc=512):
    m, d = x1.shape
    ff = wg.shape[1]
    row = lambda i: (i, 0)
    in_specs = [pl.BlockSpec((tm, d), row), pl.BlockSpec((tm, d), row), _const_spec(wg.shape),
                _const_spec(wu.shape), _const_spec(wd.shape), _const_spec(wc.shape),
                _const_spec(bc.shape), _const_spec(gpost.shape)]
    args = [h2, x1, wg, wu, wd, wc, bc, gpost]
    if tiles_per_seq is None:
        in_specs.append(pl.BlockSpec((tm, 2 * ff), row))
        args.append(hist)
        out_specs = [pl.BlockSpec((tm, d), row), pl.BlockSpec((tm, ff), row)]
        out_shape = [jax.ShapeDtypeStruct((m, d), F32), jax.ShapeDtypeStruct((m, ff), F32)]
        scratch = []
    else:
        out_specs = [pl.BlockSpec((tm, d), row), pl.BlockSpec((1, 8, ff), lambda i: (i, 0, 0))]
        out_shape = [jax.ShapeDtypeStruct((m, d), F32),
                     jax.ShapeDtypeStruct((m // tm, 8, ff), F32)]
        scratch = [pltpu.VMEM((8, ff), F32), pltpu.VMEM((tm + 8, ff), F32)]
    return pl.pallas_call(
        functools.partial(_ffn_kernel, tiles_per_seq=tiles_per_seq, fc=fc),
        grid=(m // tm,),
        in_specs=in_specs, out_specs=out_specs, out_shape=out_shape, scratch_shapes=scratch,
        compiler_params=_cparams(("arbitrary",)),
        name="conv_ffn",
    )(*args)


def _odd_in_kernel(x_ref, gpre_ref, w_ref, *outs, head_major, qscale):
    d = x_ref.shape[1]
    xb = _rms(x_ref[...], gpre_ref[...]).astype(BF16)
    q = _dot(xb, w_ref[:, 0:d]) * qscale
    k = _dot(xb, w_ref[:, d:2 * d])
    v = _dot(xb, w_ref[:, 2 * d:3 * d])
    if head_major:
        qh_ref, kh_ref, vt_ref, kt_ref, v_ref = outs
        tm = x_ref.shape[0]
        kt_ref[0] = k.T
        v_ref[...] = v
        vt = v.T
        hw = d // C_HEADS
        rowi = lax.broadcasted_iota(jnp.int32, (tm, hw), 0)
        lane = lax.broadcasted_iota(jnp.int32, (tm, hw), 1)
        pos = jnp.where((lane & 1) == 0, lax.shift_right_logical(rowi, 4), rowi & 15)
        feat = jnp.where(lane < 2 * POS_SPLITS, pos, 0).astype(F32).astype(BF16)
        ones_pad = jnp.where(lax.broadcasted_iota(jnp.int32, (16, tm), 0) == 0, 1.0, 0.0)
        for h in range(C_HEADS):
            cs = slice(h * hw, (h + 1) * hw)
            qh_ref[0, h] = q[:, cs].astype(BF16)
            kh_ref[0, h, :, 0:hw] = k[:, cs].astype(BF16)
            kh_ref[0, h, :, hw:2 * hw] = feat
            vt_ref[0, h, 0] = jnp.concatenate([vt[cs, :], ones_pad], axis=0).astype(BF16)
    else:
        q_ref, k_ref, v_ref = outs
        q_ref[...] = q
        k_ref[...] = k
        v_ref[...] = v


def _odd_in(x, gpre, w, *, tm, seq_len=None):
    m, d = x.shape
    qscale = (d // (2 * C_HEADS)) ** -0.5
    row = lambda i: (i, 0)
    head_major = seq_len is not None
    if head_major:
        assert tm <= 512
        qscale *= LOG2E
        tps = seq_len // tm
        hw = d // C_HEADS
        nb = m // seq_len
        hm = lambda i: (i // tps, 0, i % tps, 0)
        out_specs = [pl.BlockSpec((1, C_HEADS, tm, hw), hm),
                     pl.BlockSpec((1, C_HEADS, tm, 2 * hw), hm),
                     pl.BlockSpec((1, C_HEADS, 1, hw + 16, tm),
                                  lambda i: (i // tps, 0, i % tps, 0, 0)),
                     pl.BlockSpec((1, d, tm), lambda i: (i // tps, 0, i % tps)),
                     pl.BlockSpec((tm, d), row)]
        out_shape = [jax.ShapeDtypeStruct((nb, C_HEADS, seq_len, hw), BF16),
                     jax.ShapeDtypeStruct((nb, C_HEADS, seq_len, 2 * hw), BF16),
                     jax.ShapeDtypeStruct((nb, C_HEADS, tps, hw + 16, tm), BF16),
                     jax.ShapeDtypeStruct((nb, d, seq_len), F32),
                     jax.ShapeDtypeStruct((m, d), F32)]
    else:
        out_specs = [pl.BlockSpec((tm, d), row)] * 3
        out_shape = [jax.ShapeDtypeStruct((m, d), F32)] * 3
    return pl.pallas_call(
        functools.partial(_odd_in_kernel, head_major=head_major, qscale=qscale),
        grid=(m // tm,),
        in_specs=[pl.BlockSpec((tm, d), row), _const_spec(gpre.shape), _const_spec(w.shape)],
        out_specs=out_specs, out_shape=out_shape,
        compiler_params=_cparams(("parallel",)),
        name="odd_in",
    )(x, gpre, w)


def _lambda_value(lam_ref, lam_init):
    l = lam_ref[...]
    a = jnp.sum(l[0:1, :] * l[1:2, :], axis=-1, keepdims=True)
    b = jnp.sum(l[2:3, :] * l[3:4, :], axis=-1, keepdims=True)
    return jnp.exp(a) - jnp.exp(b) + lam_init


def _attn_prompt_kernel(slope2_ref, q_ref, k_ref, vt_ref, lam_ref, subln_ref, o_ref,
                        qs_ref, qn_ref, s_ref, m_ref, acc_ref, *, tq, tk, qb, lam_init):
    h = pl.program_id(1)
    iq = pl.program_id(2)
    nq = pl.num_programs(2)
    hw = q_ref.shape[3]
    hd = hw // 2
    r = 2 * tq
    slope2 = slope2_ref[h]

    lane1 = lax.broadcasted_iota(jnp.int32, (1, hw), 1)
    rest = jnp.full((1, hw), slope2, F32)
    feat = jnp.zeros((1, hw), F32)
    for i in range(POS_SPLITS):
        piece = rest.astype(BF16).astype(F32)
        rest = rest - piece
        feat = jnp.where(lane1 == 2 * i, 16.0 * piece, jnp.where(lane1 == 2 * i + 1, piece, feat))
    feat = jnp.broadcast_to(feat, (r, hw)).astype(BF16)
    lane = lax.broadcasted_iota(jnp.int32, (tq, hw), 1)

    def stack_queries(dst_ref, tile):
        q = q_ref[0, 0, pl.ds(pl.multiple_of(tile * tq, tq), tq), :]
        dst_ref[0:tq, 0:hw] = jnp.where(lane < hd, q, jnp.zeros_like(q))
        dst_ref[tq:r, 0:hw] = jnp.where(lane >= hd, q, jnp.zeros_like(q))
        dst_ref[:, hw:2 * hw] = feat

    stack_queries(qs_ref, iq)
    stack_queries(qn_ref, jnp.minimum(iq + 1, nq - 1))
    m_ref[...] = jnp.full_like(m_ref, NEG_BIG)
    acc_ref[...] = jnp.zeros_like(acc_ref)

    def keys(kc):
        return k_ref[0, 0, pl.ds(pl.multiple_of(kc * tk, tk), tk), :]

    def step(kc, kb_next, qsrc_ref, masked):
        vtb = vt_ref[0, 0, kc]
        cshift = slope2 * (kc * tk).astype(F32)
        for c0 in range(0, r, qb):
            cols = slice(c0, c0 + qb)
            kn = min(tk, (c0 % tq) + qb) if masked else tk
            s = s_ref[0:kn, cols]
            s_ref[:, cols] = _dot_nt(kb_next, qsrc_ref[cols, :])
            if masked:
                key_i = lax.broadcasted_iota(jnp.int32, (kn, qb), 0)
                qry_i = lax.broadcasted_iota(jnp.int32, (kn, qb), 1)
                s = jnp.where(key_i <= (c0 % tq) + qry_i, s, NEG_BIG)
            m_old = m_ref[:, cols]
            m_new = jnp.maximum(m_old, jnp.max(s, axis=0, keepdims=True) + cshift)
            p = jnp.exp2(s - (m_new - cshift)).astype(BF16)
            alpha = jnp.exp2(m_old - m_new)
            acc_ref[:, cols] = alpha * acc_ref[:, cols] + _dot(vtb[:, 0:kn], p)
            m_ref[:, cols] = m_new

    @pl.when(iq == 0)
    def _():
        s_ref[...] = _dot_nt(keys(0), qs_ref[...])

    def run(first, trips, width):
        def body(j, carry):
            for u in range(width):
                kc = first + width * j + u
                step(kc, keys(kc + 1), qs_ref, False)
            return carry
        lax.fori_loop(0, trips, body, 0)

    n_quads = lax.shift_right_logical(iq, 2)
    n_pairs = lax.shift_right_logical(iq & 3, 1)
    run(0, n_quads, 4)
    run(4 * n_quads, n_pairs, 2)
    run(4 * n_quads + 2 * n_pairs, iq & 1, 1)
    step(iq, keys(0), qn_ref, True)

    lam = _lambda_value(lam_ref, lam_init)
    o1 = acc_ref[0:hw, 0:tq] / acc_ref[hw:hw + 1, 0:tq]
    o2 = acc_ref[0:hw, tq:r] / acc_ref[hw:hw + 1, tq:r]
    att = (o1 - lam * o2).T
    o_ref[0] = (_rms(att, subln_ref[...]) * (1.0 - lam_init)).astype(BF16)


def _attn_prompt(slopes, qh, kh, vt, lam_p, subln, *, tq, lam_init):
    b, nh, t, hw = qh.shape
    tk = vt.shape[4]
    assert tq == tk
    return pl.pallas_call(
        functools.partial(_attn_prompt_kernel, tq=tq, tk=tk, qb=min(256, tq), lam_init=lam_init),
        grid=(b, nh, t // tq),
        in_specs=[pl.BlockSpec(memory_space=pltpu.SMEM),
                  pl.BlockSpec((1, 1, t, hw), lambda bi, h, iq: (bi, h, 0, 0)),
                  pl.BlockSpec((1, 1, t, 2 * hw), lambda bi, h, iq: (bi, h, 0, 0)),
                  pl.BlockSpec((1, 1, t // tk, hw + 16, tk), lambda bi, h, iq: (bi, h, 0, 0, 0)),
                  _const_spec(lam_p.shape), _const_spec(subln.shape)],
        out_specs=pl.BlockSpec((1, tq, hw), lambda bi, h, iq: (bi, iq, h)),
        out_shape=jax.ShapeDtypeStruct((b, t, nh * hw), BF16),
        scratch_shapes=[pltpu.VMEM((2 * tq, 2 * hw), BF16), pltpu.VMEM((2 * tq, 2 * hw), BF16),
                        pltpu.VMEM((tk, 2 * tq), F32), pltpu.VMEM((1, 2 * tq), F32),
                        pltpu.VMEM((hw + 16, 2 * tq), F32)],
        compiler_params=_cparams(("arbitrary", "arbitrary", "arbitrary")),
        name="attn_prompt",
    )(slopes * LOG2E, qh, kh, vt, lam_p, subln)


def _attn_decode_kernel(pt_ref, q_ref, kn_ref, vn_ref, slope_ref, expand_ref, lam_ref, subln_ref,
                        *rest, pg, n_past, lam_init):
    k_refs = rest[0:pg]
    v_refs = rest[pg:2 * pg]
    o_ref, m_ref, l_ref, acc_ref = rest[2 * pg:]
    j = pl.program_id(1)
    nj = pl.num_programs(1)
    _, d, page = k_refs[0].shape
    nh = C_HEADS
    nr = 2 * nh
    hd = d // nr
    slope = slope_ref[:, 0:1]

    rowi = lax.broadcasted_iota(jnp.int32, (nr, d), 0)
    coli = lax.broadcasted_iota(jnp.int32, (nr, d), 1)
    lo = (jnp.where(rowi >= nh, rowi - nh, rowi) * 2 + jnp.where(rowi >= nh, 1, 0)) * hd
    qbd = jnp.where((coli >= lo) & (coli < lo + hd), q_ref[0], 0.0)
    rowe = lax.broadcasted_iota(jnp.int32, (nr, nh * page), 0)
    cole = lax.broadcasted_iota(jnp.int32, (nr, nh * page), 1)
    own_head = (cole & (nh - 1)) == jnp.where(rowe >= nh, rowe - nh, rowe)

    @pl.when(j == 0)
    def _():
        m_ref[...] = jnp.full_like(m_ref, NEG_BIG)
        l_ref[...] = jnp.zeros_like(l_ref)
        acc_ref[...] = jnp.zeros_like(acc_ref)

    qb = qbd.astype(BF16)
    tpos = lax.broadcasted_iota(jnp.int32, (1, page), 1)
    s_parts = []
    for p_i in range(pg):
        dist = (n_past - ((j * pg + p_i) * page + tpos)).astype(F32)
        s_parts.append(_dot(qb, k_refs[p_i][0].astype(BF16)) - slope * dist)
    s = jnp.concatenate(s_parts, axis=1)
    m_old = m_ref[...]
    m_new = jnp.maximum(m_old, jnp.max(s, axis=-1, keepdims=True))
    p = jnp.exp(s - m_new).astype(BF16)
    alpha = jnp.exp(m_old - m_new)
    l_ref[...] = alpha * l_ref[...] + jnp.sum(p.astype(F32), axis=-1, keepdims=True)
    m_ref[...] = m_new
    pbig = [jnp.where(own_head, _dot(p[:, p_i * page:(p_i + 1) * page], expand_ref[...]), 0.0)
            for p_i in range(pg)]
    pv = [_dot(pbig[p_i].astype(BF16), v_refs[p_i][0].astype(BF16)) for p_i in range(pg)]
    acc_ref[...] = alpha * acc_ref[...] + sum(pv[1:], pv[0])

    @pl.when(j == nj - 1)
    def _():
        s = jnp.sum(qbd * kn_ref[0], axis=-1, keepdims=True)
        m_old = m_ref[...]
        m_new = jnp.maximum(m_old, s)
        p = jnp.exp(s - m_new)
        alpha = jnp.exp(m_old - m_new)
        l = alpha * l_ref[...] + p
        vn = vn_ref[0]
        acc = alpha * acc_ref[...] + p * jnp.concatenate([vn, vn], axis=0)
        lam = _lambda_value(lam_ref, lam_init)
        att = acc[0:nh, :] / l[0:nh, :] - lam * (acc[nh:nr, :] / l[nh:nr, :])
        o_ref[0] = (_rms(att, subln_ref[...]) * (1.0 - lam_init)).astype(BF16)


def _attn_decode(page_table, q, kn, vn, slope_tile, lam_p, subln, cache_kt, cache_v2, *, pg,
                 lam_init):
    bs, n_pages = page_table.shape
    _, d, page = cache_kt.shape
    hw = d // C_HEADS
    q3, kn3 = (a.reshape(bs, 1, d) for a in (q, kn))
    vn3 = vn.reshape(bs, C_HEADS, hw)
    expand = (jnp.arange(page * C_HEADS)[None, :] // C_HEADS
              == jnp.arange(page)[:, None]).astype(BF16)
    row = lambda b, j, pt: (b, 0, 0)
    const2 = lambda b, j, pt: (0, 0)

    def page_map(p_i):
        return lambda b, j, pt: (pt[b, j * pg + p_i], 0, 0)

    k_specs = [pl.BlockSpec((1, d, page), page_map(p_i)) for p_i in range(pg)]
    v_specs = [pl.BlockSpec((1, page * C_HEADS, hw), page_map(p_i)) for p_i in range(pg)]
    grid_spec = pltpu.PrefetchScalarGridSpec(
        num_scalar_prefetch=1,
        grid=(bs, n_pages // pg),
        in_specs=[pl.BlockSpec((1, 1, d), row), pl.BlockSpec((1, 1, d), row),
                  pl.BlockSpec((1, C_HEADS, hw), row), pl.BlockSpec(slope_tile.shape, const2),
                  pl.BlockSpec(expand.shape, const2), pl.BlockSpec(lam_p.shape, const2),
                  pl.BlockSpec(subln.shape, const2)] + k_specs + v_specs,
        out_specs=pl.BlockSpec((1, C_HEADS, hw), row),
        scratch_shapes=[pltpu.VMEM((2 * C_HEADS, 1), F32), pltpu.VMEM((2 * C_HEADS, 1), F32),
                        pltpu.VMEM((2 * C_HEADS, hw), F32)],
    )
    out = pl.pallas_call(
        functools.partial(_attn_decode_kernel, pg=pg, n_past=n_pages * page, lam_init=lam_init),
        grid_spec=grid_spec,
        out_shape=jax.ShapeDtypeStruct((bs, C_HEADS, hw), BF16),
        compiler_params=_cparams(("parallel", "arbitrary")),
        name="attn_decode",
    )(page_table, q3, kn3, vn3, slope_tile, expand, lam_p, subln, *([cache_kt] * pg),
      *([cache_v2] * pg))
    return out.reshape(bs, d)


def _row_tile(m, pref):
    tm = min(pref, m)
    assert m % tm == 0, (m, tm)
    return tm


def kernel(x_prompt, x_sample, cache_k, cache_v, page_table, state_gdn, state_gdn_conv, state_ffn_conv, norm_mix_pre, norm_mix_post, norm_ffn_pre, norm_ffn_post, w_in_even, a_v_norm, a_w_s, a_b_s, b_conv_w, b_a_log, b_dt_bias, b_out_norm, w_out_even, w_in_odd, c_lambda, c_subln, w_out_odd, w_ffn_gate, w_ffn_up, w_ffn_conv, b_ffn_conv, w_ffn_down):
    b, t, d = x_prompt.shape
    bs = x_sample.shape[0]
    assert x_sample.shape[1] == 1 and t % GDN_STEP == 0
    aw = a_v_norm.shape[-1]
    bw = b_conv_w.shape[-1] // 3
    ff = w_ffn_gate.shape[-1]
    n_split = 2 * aw + 4 * bw
    hw = d // C_HEADS
    _, n_pool, page, _, _, hd = cache_k.shape
    row2 = lambda a: a.reshape(1, -1)

    w_even = w_in_even[0, :, :n_split].astype(BF16)
    w_ba = jnp.pad(w_in_even[0, :, n_split:], ((0, 0), (0, 128 - 2 * B_HEADS))).astype(BF16)
    prm = jnp.zeros((8, 128), F32)
    prm = prm.at[0, B_HEADS:2 * B_HEADS].set(b_dt_bias[0]).at[1, B_HEADS:2 * B_HEADS].set(b_a_log[0])
    onorm = row2(b_out_norm[0])
    w_oe = w_out_even[0].astype(BF16)
    w_odd = w_in_odd[0].astype(BF16)
    w_oo = w_out_odd[0].astype(BF16)
    wg = w_ffn_gate.astype(BF16)
    wu = w_ffn_up.astype(BF16)
    wd = w_ffn_down.astype(BF16)
    slopes = jnp.exp2(-8.0 * jnp.arange(1, C_HEADS + 1, dtype=F32) / C_HEADS)
    slope_tile = jnp.broadcast_to(jnp.tile(slopes, 2)[:, None], (2 * C_HEADS, 128))
    lam_init = 0.8 - 0.6 * math.exp(-0.3 * 1)
    ws_step = row2(jnp.repeat(a_w_s[0, :, 0, 0], aw // A_GROUPS))
    bs_step = row2(jnp.repeat(a_b_s[0, :, 0], aw // A_GROUPS))
    bs_t = a_b_s[0].T

    def ffn_layer(layer, h2, x1, **kw):
        return _ffn(h2, x1, wg[layer], wu[layer], wd[layer], w_ffn_conv[layer],
                    row2(b_ffn_conv[layer]), row2(norm_ffn_post[layer]), **kw)

    tm = _row_tile(b * t, 512)
    tps = t // tm
    xp = x_prompt.reshape(b * t, d)
    a_out, _, qkv, z, ba = _even_in(xp, row2(norm_mix_pre[0]), w_even, w_ba, row2(a_v_norm[0]),
                                    a_w_s[0], bs_t, chunk=A_CHUNK, tm=tm)
    qkv3 = qkv.reshape(b, t, 3 * bw)
    o, gdn_state_p = _gdn_prompt(qkv3, ba.reshape(b, t, 128), z.reshape(b, t, bw), b_conv_w[0],
                                 prm, onorm)
    x1, h2 = _mix_out([a_out, o.reshape(b * t, bw)], xp, w_oe, row2(norm_mix_post[0]),
                      row2(norm_ffn_pre[0]), tm=tm)
    x2, gt0 = ffn_layer(0, h2, x1, tm=tm, tiles_per_seq=tps)
    qh, kh, vt, kt_p, v_p = _odd_in(x2, row2(norm_mix_pre[1]), w_odd, tm=tm, seq_len=t)
    att = _attn_prompt(slopes, qh, kh, vt, c_lambda[0], row2(c_subln[0]), tq=tm, lam_init=lam_init)
    x3, h4 = _mix_out([att.reshape(b * t, d)], x2, w_oo, row2(norm_mix_post[1]),
                      row2(norm_ffn_pre[1]), tm=tm)
    y_p, gt1 = ffn_layer(1, h4, x3, tm=tm, tiles_per_seq=tps)
    ffn_conv_p = jnp.stack([g.reshape(b, tps, 8, ff)[:, -1, 8 - (FFN_CONV - 1):, :]
                            for g in (gt0, gt1)])

    xs = x_sample.reshape(bs, d)
    a_out_s, v_s, qkv_s, z_s, ba_s = _even_in(xs, row2(norm_mix_pre[0]), w_even, w_ba,
                                              row2(a_v_norm[0]), ws_step, bs_step, chunk=1, tm=bs)
    o_s, gdn_state_s = _gdn_step(qkv_s, state_gdn_conv[0].reshape(bs, -1), ba_s, z_s, state_gdn[0],
                                 b_conv_w[0], prm, onorm, nb=min(16, bs))
    x1s, h2s = _mix_out([a_out_s, o_s], xs, w_oe, row2(norm_mix_post[0]), row2(norm_ffn_pre[0]),
                        tm=bs)
    x2s, g0s = ffn_layer(0, h2s, x1s, tm=bs, hist=state_ffn_conv[0].reshape(bs, -1))
    q_s, k_s, v_sn = _odd_in(x2s, row2(norm_mix_pre[1]), w_odd, tm=bs)
    n_pages = page_table.shape[1]
    pg = next(c for c in (8, 4, 2, 1) if n_pages % c == 0)
    cache_kt = jnp.transpose(cache_k[0], (0, 2, 3, 4, 1)).reshape(n_pool, d, page)
    cache_v2 = cache_v[0].reshape(n_pool, page * C_HEADS, hw)
    att_s = _attn_decode(page_table, q_s, k_s, v_sn, slope_tile, c_lambda[0], row2(c_subln[0]),
                         cache_kt, cache_v2, pg=pg, lam_init=lam_init)
    x3s, h4s = _mix_out([att_s], x2s, w_oo, row2(norm_mix_post[1]), row2(norm_ffn_pre[1]), tm=bs)
    y_s, g1s = ffn_layer(1, h4s, x3s, tm=bs, hist=state_ffn_conv[1].reshape(bs, -1))
    ffn_conv_s = jnp.stack([jnp.concatenate([state_ffn_conv[l][:, 1:], g[:, None, :]], axis=1)
                            for l, g in ((0, g0s), (1, g1s))])

    return (y_p.reshape(b, t, d), y_s.reshape(bs, 1, d),
            gdn_state_p[None], gdn_state_s[None],
            qkv3[:, t - (B_CONV - 1):, :][None],
            jnp.concatenate([state_gdn_conv[0][:, 1:], qkv_s[:, None, :]], axis=1)[None],
            v_s.reshape(1, bs, 1, aw),
            jnp.transpose(kt_p.reshape(1, b, C_HEADS, 2, hd, t), (0, 1, 5, 2, 3, 4)),
            v_p.reshape(1, b, t, C_HEADS, hw),
            k_s.reshape(1, bs, 1, C_HEADS, 2, hd), v_sn.reshape(1, bs, 1, C_HEADS, hw),
            ffn_conv_p, ffn_conv_s)
```

```python
import functools
import math

import jax
import jax.numpy as jnp
from jax import lax
from jax.experimental import pallas as pl
from jax.experimental.pallas import tpu as pltpu

F32 = jnp.float32
BF16 = jnp.bfloat16
EPS = 1e-6

A_GROUPS = 4
A_CHUNK = 128
B_HEADS = 4
B_CONV = 4
GDN_STEP = 128
GDN_CHUNK = 64
C_HEADS = 8
FFN_CONV = 3
NEG_BIG = -1e30
LOG2E = 1.4426950408889634
POS_SPLITS = 3

VMEM_LIMIT_BYTES = 56 * 1024 * 1024
HIGHEST = lax.Precision.HIGHEST


def _cparams(sem):
    return pltpu.CompilerParams(dimension_semantics=sem, vmem_limit_bytes=VMEM_LIMIT_BYTES)


def _gelu(x):
    return 0.5 * x * (1.0 + jnp.tanh(0.7978845608028654 * (x + 0.044715 * (x * x * x))))


def _sigmoid(x):
    return 1.0 / (1.0 + jnp.exp(-x))


def _softplus(x):
    return jnp.maximum(x, 0.0) + jnp.log(1.0 + jnp.exp(-jnp.abs(x)))


def _rms(x, gain):
    return x * lax.rsqrt(jnp.mean(x * x, axis=-1, keepdims=True) + EPS) * gain


def _dot(a, b, precision=None):
    return jnp.dot(a, b, preferred_element_type=F32, precision=precision)


def _dot_nt(a, b, precision=None):
    return lax.dot_general(a, b, (((1,), (1,)), ((), ())), preferred_element_type=F32,
                           precision=precision)


def _split(a):
    hi = a.astype(BF16)
    return hi, (a - hi.astype(F32)).astype(BF16)


def _dot3(a, b):
    (ah, al), (bh, bl) = a, b
    return _dot(ah, bh) + _dot(ah, bl) + _dot(al, bh)


def _const_spec(shape):
    nd = len(shape)
    return pl.BlockSpec(shape, lambda *_: (0,) * nd, pipeline_mode=pl.Buffered(1))


def _even_in_kernel(x_ref, gpre_ref, w_ref, wba_ref, avn_ref, ws_ref, bs_ref, *rest,
                    chunk, aw, bw):
    tm = x_ref.shape[0]
    if chunk == 1:
        aout_ref, v_ref, qkv_ref, z_ref, ba_ref = rest
    else:
        aout_ref, qkv_ref, z_ref, ba_ref = rest
    xb = _rms(x_ref[...], gpre_ref[...]).astype(BF16)
    u = _gelu(_dot(xb, w_ref[:, 0:aw]))
    v = _rms(_gelu(_dot(xb, w_ref[:, aw:2 * aw])), avn_ref[...])
    qkv_ref[...] = _dot(xb, w_ref[:, 2 * aw:2 * aw + 3 * bw])
    zz = _dot(xb, w_ref[:, 2 * aw + 3 * bw:2 * aw + 4 * bw])
    z_ref[...] = (zz * _sigmoid(zz)).astype(BF16)
    ba_ref[...] = _dot(xb, wba_ref[...])
    gd = aw // A_GROUPS
    if chunk == 1:
        v_ref[...] = v
        aout_ref[...] = (u * (v * ws_ref[...] + bs_ref[...])).astype(BF16)
    else:
        row = lax.broadcasted_iota(jnp.int32, (chunk, chunk), 0)
        col = lax.broadcasted_iota(jnp.int32, (chunk, chunk), 1)
        for g in range(A_GROUPS):
            wt = jnp.where(col <= row, ws_ref[g], 0.0).astype(BF16)
            bcol = bs_ref[:, g:g + 1]
            for n in range(tm // chunk):
                rs = slice(n * chunk, (n + 1) * chunk)
                cs = slice(g * gd, (g + 1) * gd)
                mixed = _dot(wt, v[rs, cs].astype(BF16)) + bcol
                aout_ref[rs, cs] = (u[rs, cs] * mixed).astype(BF16)


def _even_in(x, gpre, w_main, w_ba, avn, ws, bs, *, chunk, tm):
    m, d = x.shape
    aw = avn.shape[-1]
    bw = (w_main.shape[1] - 2 * aw - 2 * B_HEADS) // 4
    row = lambda i: (i, 0)
    in_specs = [pl.BlockSpec((tm, d), row), _const_spec(gpre.shape), _const_spec(w_main.shape),
                _const_spec(w_ba.shape), _const_spec(avn.shape), _const_spec(ws.shape),
                _const_spec(bs.shape)]
    args = [x, gpre, w_main, w_ba, avn, ws, bs]
    tail_specs = [pl.BlockSpec((tm, 3 * bw), row), pl.BlockSpec((tm, bw), row),
                  pl.BlockSpec((tm, 128), row)]
    tail_shape = [jax.ShapeDtypeStruct((m, 3 * bw), F32), jax.ShapeDtypeStruct((m, bw), BF16),
                  jax.ShapeDtypeStruct((m, 128), F32)]
    if chunk == 1:
        out_specs = [pl.BlockSpec((tm, aw), row), pl.BlockSpec((tm, aw), row)] + tail_specs
        out_shape = ([jax.ShapeDtypeStruct((m, aw), BF16), jax.ShapeDtypeStruct((m, aw), F32)]
                     + tail_shape)
    else:
        out_specs = [pl.BlockSpec((tm, aw), row)] + tail_specs
        out_shape = [jax.ShapeDtypeStruct((m, aw), BF16)] + tail_shape
    return pl.pallas_call(
        functools.partial(_even_in_kernel, chunk=chunk, aw=aw, bw=bw),
        grid=(m // tm,),
        in_specs=in_specs, out_specs=out_specs, out_shape=out_shape,
        compiler_params=_cparams(("parallel",)),
        name="even_in",
    )(*args)


def _gdn_prompt_kernel(qkv_ref, ba_ref, z_ref, cw_ref, prm_ref, onorm_ref,
                       o_ref, st_ref,
                       ext_ref, wm_ref, u0_ref, qe_ref, qk_ref, kwt_ref, egl_ref, *, bw):
    i = pl.program_id(1)
    n = GDN_STEP
    c = GDN_CHUNK
    hd = bw // B_HEADS

    @pl.when(i == 0)
    def _():
        st_ref[...] = jnp.zeros_like(st_ref)
        ext_ref[n:n + 8, :] = jnp.zeros((8, ext_ref.shape[1]), F32)
        wm_ref[...] = jnp.zeros_like(wm_ref)
        u0_ref[...] = jnp.zeros_like(u0_ref)
        qe_ref[...] = jnp.zeros_like(qe_ref)
        qk_ref[...] = jnp.zeros_like(qk_ref)
        kwt_ref[...] = jnp.zeros_like(kwt_ref)
        egl_ref[...] = jnp.zeros_like(egl_ref)

    heads = range(B_HEADS)
    zero_half = jnp.zeros((c, hd), F32)
    egl = [(egl_ref[0:1, B_HEADS + h:B_HEADS + h + 1], egl_ref[c:c + 1, B_HEADS + h:B_HEADS + h + 1])
           for h in heads]
    state = {"s": [st_ref[0, h] for h in heads]}

    def advance_stage_u(r0):
        state["sb"] = [s.astype(BF16) for s in state["s"]]
        state["u"] = [u0_ref[h, r0:r0 + c, :] - _dot(wm_ref[h, r0:r0 + c, :], state["sb"][h])
                      for h in heads]

    def advance_stage_s(r0, part):
        halves = [[u, zero_half] if part == 0 else [zero_half, u] for u in state["u"]]
        uf = [jnp.concatenate(hv, axis=0).astype(BF16) for hv in halves]
        o = [_dot(qe_ref[h, r0:r0 + c, :], state["sb"][h]) + _dot(qk_ref[h, r0:r0 + c, :], uf[h])
             for h in heads]
        state["s"] = [egl[h][part] * state["s"][h] + _dot(kwt_ref[h], uf[h]) for h in heads]
        return o

    ext_ref[5:8, :] = ext_ref[n + 5:n + 8, :]
    cur = qkv_ref[0]
    ext_ref[8:n + 8, :] = cur
    y = (cw_ref[0:1, :] * ext_ref[5:n + 5, :] + cw_ref[1:2, :] * ext_ref[6:n + 6, :]
         + cw_ref[2:3, :] * ext_ref[7:n + 7, :] + cw_ref[3:4, :] * cur)
    y = y * _sigmoid(y)

    ba = ba_ref[0]
    beta = _sigmoid(ba)
    gfull = -jnp.exp(prm_ref[1:2, :]) * _softplus(ba + prm_ref[0:1, :])
    row = lax.broadcasted_iota(jnp.int32, (n, n), 0)
    col = lax.broadcasted_iota(jnp.int32, (n, n), 1)
    same = (row < c) == (col < c)
    incl = same & (col <= row)
    strict = same & (col < row)
    eye = jnp.where(row == col, 1.0, 0.0)

    advance_stage_u(0)
    g_hi, g_lo = _split(gfull)
    ones_incl = jnp.where(incl, 1.0, 0.0).astype(BF16)
    ones_same = jnp.where(same, 1.0, 0.0).astype(BF16)
    gcum = _dot(ones_incl, g_hi) + _dot(ones_incl, g_lo)
    glast = _dot(ones_same, g_hi) + _dot(ones_same, g_lo)
    gcum_t = gcum.T

    qs, ks, vs, kbs, decays = [], [], [], [], []
    for h in heads:
        q = y[:, h * hd:(h + 1) * hd]
        k = y[:, bw + h * hd:bw + (h + 1) * hd]
        qs.append(q * lax.rsqrt(jnp.sum(q * q, axis=-1, keepdims=True) + EPS) * (hd ** -0.5))
        ks.append(k * lax.rsqrt(jnp.sum(k * k, axis=-1, keepdims=True) + EPS))
        vs.append(y[:, 2 * bw + h * hd:2 * bw + (h + 1) * hd])
        kbs.append(ks[h].astype(BF16))
        gcol = gcum[:, B_HEADS + h:B_HEADS + h + 1]
        grow = gcum_t[B_HEADS + h:B_HEADS + h + 1, :]
        decays.append(jnp.where(incl, jnp.exp(jnp.where(incl, gcol - grow, 0.0)), 0.0))
    bcols = [beta[:, h:h + 1] for h in heads]
    gcols = [gcum[:, B_HEADS + h:B_HEADS + h + 1] for h in heads]
    glcols = [glast[:, B_HEADS + h:B_HEADS + h + 1] for h in heads]

    kk = [_dot_nt(kbs[h], kbs[h]) for h in heads]
    o_a = advance_stage_s(0, 0)
    x = [-jnp.where(strict, bcols[h] * decays[h] * kk[h], 0.0) for h in heads]
    p = [eye + x[h] for h in heads]
    xs = [_split(x[h]) for h in heads]
    o_b = None
    for it in range(int(math.log2(c)) - 1):
        xs = [_split(_dot3(xs[h], xs[h])) for h in heads]
        if it == 0:
            advance_stage_u(c)
        p = [p[h] + _dot3(_split(p[h]), xs[h]) for h in heads]
        if it == 1:
            o_b = advance_stage_s(c, 1)
    egs = [jnp.exp(gcols[h]) for h in heads]
    ps = [_split(p[h]) for h in heads]
    wm = [_dot3(ps[h], _split(bcols[h] * egs[h] * ks[h])) for h in heads]
    u0 = [_dot3(ps[h], _split(bcols[h] * vs[h])) for h in heads]
    qk = [_dot_nt(qs[h].astype(BF16), kbs[h]) for h in heads]

    for h in heads:
        st_ref[0, h] = state["s"][h]
        o = jnp.concatenate([o_a[h], o_b[h]], axis=0)
        cs = slice(h * hd, (h + 1) * hd)
        o_ref[0, :, cs] = (_rms(o, onorm_ref[...]) * z_ref[0, :, cs].astype(F32)).astype(BF16)
        wm_ref[h] = wm[h].astype(BF16)
        u0_ref[h] = u0[h]
        qe_ref[h] = (qs[h] * egs[h]).astype(BF16)
        qk_ref[h] = (qk[h] * decays[h]).astype(BF16)
        kwt_ref[h] = (ks[h] * jnp.exp(glcols[h] - gcols[h])).T.astype(BF16)
    egl_ref[...] = jnp.exp(glast)


def _gdn_prompt(qkv, ba, z, cw, prm, onorm):
    b, t, c3 = qkv.shape
    bw = c3 // 3
    hd = bw // B_HEADS
    n = GDN_STEP
    nt = t // n
    cur = lambda bi, i: (bi, jnp.minimum(i, nt - 1), 0)
    prev = lambda bi, i: (bi, jnp.maximum(i - 1, 0), 0)
    return pl.pallas_call(
        functools.partial(_gdn_prompt_kernel, bw=bw),
        grid=(b, nt + 1),
        in_specs=[pl.BlockSpec((1, n, c3), cur), pl.BlockSpec((1, n, 128), cur),
                  pl.BlockSpec((1, n, bw), prev), _const_spec(cw.shape), _const_spec(prm.shape),
                  _const_spec(onorm.shape)],
        out_specs=[pl.BlockSpec((1, n, bw), prev),
                   pl.BlockSpec((1, B_HEADS, hd, hd), lambda bi, i: (bi, 0, 0, 0))],
        out_shape=[jax.ShapeDtypeStruct((b, t, bw), BF16),
                   jax.ShapeDtypeStruct((b, B_HEADS, hd, hd), F32)],
        scratch_shapes=[pltpu.VMEM((n + 8, c3), F32),
                        pltpu.VMEM((B_HEADS, n, hd), BF16), pltpu.VMEM((B_HEADS, n, hd), F32),
                        pltpu.VMEM((B_HEADS, n, hd), BF16), pltpu.VMEM((B_HEADS, n, n), BF16),
                        pltpu.VMEM((B_HEADS, hd, n), BF16), pltpu.VMEM((n, 128), F32)],
        compiler_params=_cparams(("arbitrary", "arbitrary")),
        name="gdn_prompt",
    )(qkv, ba, z, cw, prm, onorm)


def _gdn_step_kernel(qkv_ref, cst_ref, ba_ref, z_ref, st_ref, cw_ref, prm_ref, onorm_ref,
                     o_ref, sto_ref, *, bw):
    nb = qkv_ref.shape[0]
    hd = bw // B_HEADS
    c3 = 3 * bw
    y = (cw_ref[0:1, :] * cst_ref[:, 0:c3] + cw_ref[1:2, :] * cst_ref[:, c3:2 * c3]
         + cw_ref[2:3, :] * cst_ref[:, 2 * c3:3 * c3] + cw_ref[3:4, :] * qkv_ref[...])
    y = y * _sigmoid(y)
    ba = ba_ref[...]
    beta = _sigmoid(ba)
    eg_all = jnp.exp(-jnp.exp(prm_ref[1:2, :]) * _softplus(ba + prm_ref[0:1, :]))
    row = lax.broadcasted_iota(jnp.int32, (hd, hd), 0)
    col = lax.broadcasted_iota(jnp.int32, (hd, hd), 1)
    eye = jnp.where(row == col, 1.0, 0.0)
    for h in range(B_HEADS):
        q = y[:, h * hd:(h + 1) * hd]
        k = y[:, bw + h * hd:bw + (h + 1) * hd]
        v = y[:, 2 * bw + h * hd:2 * bw + (h + 1) * hd]
        q = q * lax.rsqrt(jnp.sum(q * q, axis=-1, keepdims=True) + EPS) * (hd ** -0.5)
        k = k * lax.rsqrt(jnp.sum(k * k, axis=-1, keepdims=True) + EPS)
        qk = jnp.sum(q * k, axis=-1, keepdims=True)
        k_t = _dot_nt(eye, k, HIGHEST)
        q_t = _dot_nt(eye, q, HIGHEST)
        o_rows = []
        for j in range(nb):
            s = st_ref[j, h]
            eg = eg_all[j:j + 1, B_HEADS + h:B_HEADS + h + 1]
            bt = beta[j:j + 1, h:h + 1]
            kc = k_t[:, j:j + 1]
            ks = jnp.sum(kc * s, axis=0, keepdims=True)
            qs = jnp.sum(q_t[:, j:j + 1] * s, axis=0, keepdims=True)
            u = bt * (v[j:j + 1, :] - eg * ks)
            o_rows.append(eg * qs + qk[j:j + 1, :] * u)
            sto_ref[j, h] = eg * s + kc * u
        o = jnp.concatenate(o_rows, axis=0)
        cs = slice(h * hd, (h + 1) * hd)
        o_ref[:, cs] = (_rms(o, onorm_ref[...]) * z_ref[:, cs].astype(F32)).astype(BF16)


def _gdn_step(qkv, cst, ba, z, state, cw, prm, onorm, *, nb):
    m, c3 = qkv.shape
    bw = c3 // 3
    hd = bw // B_HEADS
    row = lambda i: (i, 0)
    st = lambda i: (i, 0, 0, 0)
    return pl.pallas_call(
        functools.partial(_gdn_step_kernel, bw=bw),
        grid=(m // nb,),
        in_specs=[pl.BlockSpec((nb, c3), row), pl.BlockSpec((nb, 3 * c3), row),
                  pl.BlockSpec((nb, 128), row), pl.BlockSpec((nb, bw), row),
                  pl.BlockSpec((nb, B_HEADS, hd, hd), st), _const_spec(cw.shape),
                  _const_spec(prm.shape), _const_spec(onorm.shape)],
        out_specs=[pl.BlockSpec((nb, bw), row), pl.BlockSpec((nb, B_HEADS, hd, hd), st)],
        out_shape=[jax.ShapeDtypeStruct((m, bw), BF16),
                   jax.ShapeDtypeStruct((m, B_HEADS, hd, hd), F32)],
        compiler_params=_cparams(("parallel",)),
        name="gdn_step",
    )(qkv, cst, ba, z, state, cw, prm, onorm)


def _mix_out_kernel(*refs, n_in):
    ins = refs[:n_in]
    x_ref, w_ref, gpost_ref, x1_ref = refs[n_in:]
    mix = None
    r0 = 0
    for a_ref in ins:
        kk = a_ref.shape[1]
        part = _dot(a_ref[...], w_ref[r0:r0 + kk, :])
        mix = part if mix is None else mix + part
        r0 += kk
    x1_ref[...] = x_ref[...] + _rms(mix, gpost_ref[...])


def _mix_out(ins, x, w, gpost, *, tm):
    m, d = x.shape
    row = lambda i: (i, 0)
    return pl.pallas_call(
        functools.partial(_mix_out_kernel, n_in=len(ins)),
        grid=(m // tm,),
        in_specs=[pl.BlockSpec((tm, a.shape[1]), row) for a in ins]
        + [pl.BlockSpec((tm, d), row), _const_spec(w.shape), _const_spec(gpost.shape)],
        out_specs=pl.BlockSpec((tm, d), row),
        out_shape=jax.ShapeDtypeStruct((m, d), F32),
        compiler_params=_cparams(("parallel",)),
        name="mix_out",
    )(*ins, x, w, gpost)


def _ffn_kernel(x_ref, gpre_ref, wg_ref, wu_ref, wd_ref, wc_ref, bc_ref, gpost_ref, *rest,
                tiles_per_seq, fc):
    tm = x_ref.shape[0]
    ff = wg_ref.shape[1]
    if tiles_per_seq is None:
        hist_ref, x2_ref, gout_ref = rest
    else:
        x2_ref, gout_ref, carry_ref, gbuf_ref = rest

        @pl.when(pl.program_id(0) % tiles_per_seq == 0)
        def _():
            carry_ref[...] = jnp.zeros_like(carry_ref)

    hb = _rms(x_ref[...], gpre_ref[...]).astype(BF16)

    chunks = [slice(c0, min(c0 + fc, ff)) for c0 in range(0, ff, fc)]

    def gate_up(cs):
        return _dot(hb, wg_ref[:, cs]), _dot(hb, wu_ref[:, cs])

    acc = None
    ahead = gate_up(chunks[0])
    for ci, cs in enumerate(chunks):
        g, up = ahead
        if ci + 1 < len(chunks):
            ahead = gate_up(chunks[ci + 1])
        if tiles_per_seq is None:
            gout_ref[:, cs] = g
            sh2 = hist_ref[:, cs]
            sh1 = hist_ref[:, ff + cs.start:ff + cs.stop]
        else:
            gbuf_ref[0:8, cs] = carry_ref[:, cs]
            gbuf_ref[8:tm + 8, cs] = g
            sh2 = gbuf_ref[6:tm + 6, cs]
            sh1 = gbuf_ref[7:tm + 7, cs]
            tail = gbuf_ref[tm:tm + 8, cs]
            carry_ref[:, cs] = tail
            gout_ref[0, :, cs] = tail
        conv = (wc_ref[0:1, cs] * sh2 + wc_ref[1:2, cs] * sh1 + wc_ref[2:3, cs] * g
                + bc_ref[:, cs])
        a = (_gelu(conv) * up).astype(BF16)
        part = _dot(a, wd_ref[cs, :])
        acc = part if acc is None else acc + part
    x2_ref[...] = x_ref[...] + _rms(acc, gpost_ref[...])


def _ffn(x1, gpre, wg, wu, wd, wc, bc, gpost, *, layer, tm, tiles_per_seq=None, hist=None, fc=512):
    m, d = x1.shape
    ff = wg.shape[2]
    row = lambda i: (i, 0)

    def layer_spec(w):
        return pl.BlockSpec((None,) + w.shape[1:], lambda i: (layer, 0, 0),
                            pipeline_mode=pl.Buffered(1))

    in_specs = [pl.BlockSpec((tm, d), row), _const_spec(gpre.shape), layer_spec(wg),
                layer_spec(wu), layer_spec(wd), _const_spec(wc.shape),
                _const_spec(bc.shape), _const_spec(gpost.shape)]
    args = [x1, gpre, wg, wu, wd, wc, bc, gpost]
    if tiles_per_seq is None:
        in_specs.append(pl.BlockSpec((tm, 2 * ff), row))
        args.append(hist)
        out_specs = [pl.BlockSpec((tm, d), row), pl.BlockSpec((tm, ff), row)]
        out_shape = [jax.ShapeDtypeStruct((m, d), F32), jax.ShapeDtypeStruct((m, ff), F32)]
        scratch = []
    else:
        out_specs = [pl.BlockSpec((tm, d), row), pl.BlockSpec((1, 8, ff), lambda i: (i, 0, 0))]
        out_shape = [jax.ShapeDtypeStruct((m, d), F32),
                     jax.ShapeDtypeStruct((m // tm, 8, ff), F32)]
        scratch = [pltpu.VMEM((8, ff), F32), pltpu.VMEM((tm + 8, ff), F32)]
    return pl.pallas_call(
        functools.partial(_ffn_kernel, tiles_per_seq=tiles_per_seq, fc=fc),
        grid=(m // tm,),
        in_specs=in_specs, out_specs=out_specs, out_shape=out_shape, scratch_shapes=scratch,
        compiler_params=_cparams(("arbitrary",)),
        name="conv_ffn",
    )(*args)


def _odd_in_kernel(x_ref, gpre_ref, w_ref, *outs, head_major, qscale):
    d = x_ref.shape[1]
    xb = _rms(x_ref[...], gpre_ref[...]).astype(BF16)
    q = _dot(xb, w_ref[:, 0:d]) * qscale
    k = _dot(xb, w_ref[:, d:2 * d])
    v = _dot(xb, w_ref[:, 2 * d:3 * d])
    if head_major:
        qh_ref, kh_ref, vt_ref, kt_ref, v_ref = outs
        tm = x_ref.shape[0]
        kt_ref[0] = k.T
        v_ref[...] = v
        vt = v.T
        hw = d // C_HEADS
        rowi = lax.broadcasted_iota(jnp.int32, (tm, hw), 0)
        lane = lax.broadcasted_iota(jnp.int32, (tm, hw), 1)
        pos = jnp.where((lane & 1) == 0, lax.shift_right_logical(rowi, 4), rowi & 15)
        feat = jnp.where(lane < 2 * POS_SPLITS, pos, 0).astype(F32).astype(BF16)
        ones_pad = jnp.where(lax.broadcasted_iota(jnp.int32, (16, tm), 0) == 0, 1.0, 0.0)
        for h in range(C_HEADS):
            cs = slice(h * hw, (h + 1) * hw)
            qh_ref[0, h] = q[:, cs].astype(BF16)
            kh_ref[0, h, :, 0:hw] = k[:, cs].astype(BF16)
            kh_ref[0, h, :, hw:2 * hw] = feat
            vt_ref[0, h, 0] = jnp.concatenate([vt[cs, :], ones_pad], axis=0).astype(BF16)
    else:
        q_ref, k_ref, v_ref = outs
        q_ref[...] = q
        k_ref[...] = k
        v_ref[...] = v


def _odd_in(x, gpre, w, *, tm, seq_len=None):
    m, d = x.shape
    qscale = (d // (2 * C_HEADS)) ** -0.5
    row = lambda i: (i, 0)
    head_major = seq_len is not None
    if head_major:
        assert tm <= 512
        qscale *= LOG2E
        tps = seq_len // tm
        hw = d // C_HEADS
        nb = m // seq_len
        hm = lambda i: (i // tps, 0, i % tps, 0)
        out_specs = [pl.BlockSpec((1, C_HEADS, tm, hw), hm),
                     pl.BlockSpec((1, C_HEADS, tm, 2 * hw), hm),
                     pl.BlockSpec((1, C_HEADS, 1, hw + 16, tm),
                                  lambda i: (i // tps, 0, i % tps, 0, 0)),
                     pl.BlockSpec((1, d, tm), lambda i: (i // tps, 0, i % tps)),
                     pl.BlockSpec((tm, d), row)]
        out_shape = [jax.ShapeDtypeStruct((nb, C_HEADS, seq_len, hw), BF16),
                     jax.ShapeDtypeStruct((nb, C_HEADS, seq_len, 2 * hw), BF16),
                     jax.ShapeDtypeStruct((nb, C_HEADS, tps, hw + 16, tm), BF16),
                     jax.ShapeDtypeStruct((nb, d, seq_len), F32),
                     jax.ShapeDtypeStruct((m, d), F32)]
    else:
        out_specs = [pl.BlockSpec((tm, d), row)] * 3
        out_shape = [jax.ShapeDtypeStruct((m, d), F32)] * 3
    return pl.pallas_call(
        functools.partial(_odd_in_kernel, head_major=head_major, qscale=qscale),
        grid=(m // tm,),
        in_specs=[pl.BlockSpec((tm, d), row), _const_spec(gpre.shape), _const_spec(w.shape)],
        out_specs=out_specs, out_shape=out_shape,
        compiler_params=_cparams(("parallel",)),
        name="odd_in",
    )(x, gpre, w)


def _lambda_value(lam_ref, lam_init):
    l = lam_ref[...]
    a = jnp.sum(l[0:1, :] * l[1:2, :], axis=-1, keepdims=True)
    b = jnp.sum(l[2:3, :] * l[3:4, :], axis=-1, keepdims=True)
    return jnp.exp(a) - jnp.exp(b) + lam_init


def _attn_prompt_kernel(slope2_ref, q_ref, k_ref, vt_ref, lam_ref, subln_ref, o_ref,
                        qs_ref, qn_ref, s_ref, m_ref, acc_ref, *, tq, tk, qb, lam_init):
    h = pl.program_id(1)
    iq = pl.program_id(2)
    nq = pl.num_programs(2)
    hw = q_ref.shape[3]
    hd = hw // 2
    r = 2 * tq
    slope2 = slope2_ref[h]

    lane1 = lax.broadcasted_iota(jnp.int32, (1, hw), 1)
    rest = jnp.full((1, hw), slope2, F32)
    feat = jnp.zeros((1, hw), F32)
    for i in range(POS_SPLITS):
        piece = rest.astype(BF16).astype(F32)
        rest = rest - piece
        feat = jnp.where(lane1 == 2 * i, 16.0 * piece, jnp.where(lane1 == 2 * i + 1, piece, feat))
    feat = jnp.broadcast_to(feat, (r, hw)).astype(BF16)
    lane = lax.broadcasted_iota(jnp.int32, (tq, hw), 1)

    def stack_queries(dst_ref, tile):
        q = q_ref[0, 0, pl.ds(pl.multiple_of(tile * tq, tq), tq), :]
        dst_ref[0:tq, 0:hw] = jnp.where(lane < hd, q, jnp.zeros_like(q))
        dst_ref[tq:r, 0:hw] = jnp.where(lane >= hd, q, jnp.zeros_like(q))
        dst_ref[:, hw:2 * hw] = feat

    stack_queries(qs_ref, iq)
    stack_queries(qn_ref, jnp.minimum(iq + 1, nq - 1))
    m_ref[...] = jnp.full_like(m_ref, NEG_BIG)
    acc_ref[...] = jnp.zeros_like(acc_ref)

    def keys(kc):
        return k_ref[0, 0, pl.ds(pl.multiple_of(kc * tk, tk), tk), :]

    def step(kc, kb_next, qsrc_ref, masked):
        vtb = vt_ref[0, 0, kc]
        cshift = slope2 * (kc * tk).astype(F32)
        for c0 in range(0, r, qb):
            cols = slice(c0, c0 + qb)
            kn = min(tk, (c0 % tq) + qb) if masked else tk
            s = s_ref[0:kn, cols]
            s_ref[:, cols] = _dot_nt(kb_next, qsrc_ref[cols, :])
            if masked:
                key_i = lax.broadcasted_iota(jnp.int32, (kn, qb), 0)
                qry_i = lax.broadcasted_iota(jnp.int32, (kn, qb), 1)
                s = jnp.where(key_i <= (c0 % tq) + qry_i, s, NEG_BIG)
            m_old = m_ref[:, cols]
            m_new = jnp.maximum(m_old, jnp.max(s, axis=0, keepdims=True) + cshift)
            p = jnp.exp2(s - (m_new - cshift)).astype(BF16)
            alpha = jnp.exp2(m_old - m_new)
            acc_ref[:, cols] = alpha * acc_ref[:, cols] + _dot(vtb[:, 0:kn], p)
            m_ref[:, cols] = m_new

    @pl.when(iq == 0)
    def _():
        s_ref[...] = _dot_nt(keys(0), qs_ref[...])

    done = 0
    for width in (8, 4, 2, 1):
        trips = lax.shift_right_logical(iq - done, width.bit_length() - 1)

        def body(j, carry, first=done, width=width):
            for u in range(width):
                kc = first + width * j + u
                step(kc, keys(kc + 1), qs_ref, False)
            return carry

        lax.fori_loop(0, trips, body, 0)
        done = done + width * trips
    step(iq, keys(0), qn_ref, True)

    lam = _lambda_value(lam_ref, lam_init)
    o1 = acc_ref[0:hw, 0:tq] / acc_ref[hw:hw + 1, 0:tq]
    o2 = acc_ref[0:hw, tq:r] / acc_ref[hw:hw + 1, tq:r]
    att = (o1 - lam * o2).T
    o_ref[0] = (_rms(att, subln_ref[...]) * (1.0 - lam_init)).astype(BF16)


def _attn_prompt(slopes, qh, kh, vt, lam_p, subln, *, tq, lam_init):
    b, nh, t, hw = qh.shape
    tk = vt.shape[4]
    assert tq == tk
    return pl.pallas_call(
        functools.partial(_attn_prompt_kernel, tq=tq, tk=tk, qb=min(256, tq), lam_init=lam_init),
        grid=(b, nh, t // tq),
        in_specs=[pl.BlockSpec(memory_space=pltpu.SMEM),
                  pl.BlockSpec((1, 1, t, hw), lambda bi, h, iq: (bi, h, 0, 0)),
                  pl.BlockSpec((1, 1, t, 2 * hw), lambda bi, h, iq: (bi, h, 0, 0)),
                  pl.BlockSpec((1, 1, t // tk, hw + 16, tk), lambda bi, h, iq: (bi, h, 0, 0, 0)),
                  _const_spec(lam_p.shape), _const_spec(subln.shape)],
        out_specs=pl.BlockSpec((1, tq, hw), lambda bi, h, iq: (bi, iq, h)),
        out_shape=jax.ShapeDtypeStruct((b, t, nh * hw), BF16),
        scratch_shapes=[pltpu.VMEM((2 * tq, 2 * hw), BF16), pltpu.VMEM((2 * tq, 2 * hw), BF16),
                        pltpu.VMEM((tk, 2 * tq), F32), pltpu.VMEM((1, 2 * tq), F32),
                        pltpu.VMEM((hw + 16, 2 * tq), F32)],
        compiler_params=_cparams(("arbitrary", "arbitrary", "arbitrary")),
        name="attn_prompt",
    )(slopes * LOG2E, qh, kh, vt, lam_p, subln)


def _attn_decode_kernel(pt_ref, q_ref, kn_ref, vn_ref, slope_ref, expand_ref, lam_ref, subln_ref,
                        *rest, pg, n_past, lam_init):
    k_refs = rest[0:pg]
    v_refs = rest[pg:2 * pg]
    o_ref, m_ref, l_ref, acc_ref = rest[2 * pg:]
    j = pl.program_id(1)
    nj = pl.num_programs(1)
    _, d, page = k_refs[0].shape
    nh = C_HEADS
    nr = 2 * nh
    hd = d // nr
    slope = slope_ref[:, 0:1]

    rowi = lax.broadcasted_iota(jnp.int32, (nr, d), 0)
    coli = lax.broadcasted_iota(jnp.int32, (nr, d), 1)
    lo = (jnp.where(rowi >= nh, rowi - nh, rowi) * 2 + jnp.where(rowi >= nh, 1, 0)) * hd
    qbd = jnp.where((coli >= lo) & (coli < lo + hd), q_ref[0], 0.0)
    rowe = lax.broadcasted_iota(jnp.int32, (nr, nh * page), 0)
    cole = lax.broadcasted_iota(jnp.int32, (nr, nh * page), 1)
    own_head = (cole & (nh - 1)) == jnp.where(rowe >= nh, rowe - nh, rowe)

    @pl.when(j == 0)
    def _():
        m_ref[...] = jnp.full_like(m_ref, NEG_BIG)
        l_ref[...] = jnp.zeros_like(l_ref)
        acc_ref[...] = jnp.zeros_like(acc_ref)

    qb = qbd.astype(BF16)
    tpos = lax.broadcasted_iota(jnp.int32, (1, page), 1)
    s_parts = []
    for p_i in range(pg):
        dist = (n_past - ((j * pg + p_i) * page + tpos)).astype(F32)
        s_parts.append(_dot(qb, k_refs[p_i][0].astype(BF16)) - slope * dist)
    s = jnp.concatenate(s_parts, axis=1)
    m_old = m_ref[...]
    m_new = jnp.maximum(m_old, jnp.max(s, axis=-1, keepdims=True))
    p = jnp.exp(s - m_new).astype(BF16)
    alpha = jnp.exp(m_old - m_new)
    l_ref[...] = alpha * l_ref[...] + jnp.sum(p.astype(F32), axis=-1, keepdims=True)
    m_ref[...] = m_new
    pbig = [jnp.where(own_head, _dot(p[:, p_i * page:(p_i + 1) * page], expand_ref[...]), 0.0)
            for p_i in range(pg)]
    pv = [_dot(pbig[p_i].astype(BF16), v_refs[p_i][0].astype(BF16)) for p_i in range(pg)]
    acc_ref[...] = alpha * acc_ref[...] + sum(pv[1:], pv[0])

    @pl.when(j == nj - 1)
    def _():
        s = jnp.sum(qbd * kn_ref[0], axis=-1, keepdims=True)
        m_old = m_ref[...]
        m_new = jnp.maximum(m_old, s)
        p = jnp.exp(s - m_new)
        alpha = jnp.exp(m_old - m_new)
        l = alpha * l_ref[...] + p
        vn = vn_ref[0]
        acc = alpha * acc_ref[...] + p * jnp.concatenate([vn, vn], axis=0)
        lam = _lambda_value(lam_ref, lam_init)
        att = acc[0:nh, :] / l[0:nh, :] - lam * (acc[nh:nr, :] / l[nh:nr, :])
        o_ref[0] = (_rms(att, subln_ref[...]) * (1.0 - lam_init)).astype(BF16)


def _attn_decode(page_table, q, kn, vn, slope_tile, lam_p, subln, cache_kt, cache_v2, *, pg,
                 lam_init):
    bs, n_pages = page_table.shape
    _, d, page = cache_kt.shape
    hw = d // C_HEADS
    q3, kn3 = (a.reshape(bs, 1, d) for a in (q, kn))
    vn3 = vn.reshape(bs, C_HEADS, hw)
    expand = (jnp.arange(page * C_HEADS)[None, :] // C_HEADS
              == jnp.arange(page)[:, None]).astype(BF16)
    row = lambda b, j, pt: (b, 0, 0)
    const2 = lambda b, j, pt: (0, 0)

    def page_map(p_i):
        return lambda b, j, pt: (pt[b, j * pg + p_i], 0, 0)

    k_specs = [pl.BlockSpec((1, d, page), page_map(p_i)) for p_i in range(pg)]
    v_specs = [pl.BlockSpec((1, page * C_HEADS, hw), page_map(p_i)) for p_i in range(pg)]
    grid_spec = pltpu.PrefetchScalarGridSpec(
        num_scalar_prefetch=1,
        grid=(bs, n_pages // pg),
        in_specs=[pl.BlockSpec((1, 1, d), row), pl.BlockSpec((1, 1, d), row),
                  pl.BlockSpec((1, C_HEADS, hw), row), pl.BlockSpec(slope_tile.shape, const2),
                  pl.BlockSpec(expand.shape, const2), pl.BlockSpec(lam_p.shape, const2),
                  pl.BlockSpec(subln.shape, const2)] + k_specs + v_specs,
        out_specs=pl.BlockSpec((1, C_HEADS, hw), row),
        scratch_shapes=[pltpu.VMEM((2 * C_HEADS, 1), F32), pltpu.VMEM((2 * C_HEADS, 1), F32),
                        pltpu.VMEM((2 * C_HEADS, hw), F32)],
    )
    out = pl.pallas_call(
        functools.partial(_attn_decode_kernel, pg=pg, n_past=n_pages * page, lam_init=lam_init),
        grid_spec=grid_spec,
        out_shape=jax.ShapeDtypeStruct((bs, C_HEADS, hw), BF16),
        compiler_params=_cparams(("parallel", "arbitrary")),
        name="attn_decode",
    )(page_table, q3, kn3, vn3, slope_tile, expand, lam_p, subln, *([cache_kt] * pg),
      *([cache_v2] * pg))
    return out.reshape(bs, d)


def _row_tile(m, pref):
    tm = min(pref, m)
    assert m % tm == 0, (m, tm)
    return tm


def kernel(x_prompt, x_sample, cache_k, cache_v, page_table, state_gdn, state_gdn_conv, state_ffn_conv, norm_mix_pre, norm_mix_post, norm_ffn_pre, norm_ffn_post, w_in_even, a_v_norm, a_w_s, a_b_s, b_conv_w, b_a_log, b_dt_bias, b_out_norm, w_out_even, w_in_odd, c_lambda, c_subln, w_out_odd, w_ffn_gate, w_ffn_up, w_ffn_conv, b_ffn_conv, w_ffn_down):
    b, t, d = x_prompt.shape
    bs = x_sample.shape[0]
    assert x_sample.shape[1] == 1 and t % GDN_STEP == 0
    aw = a_v_norm.shape[-1]
    bw = b_conv_w.shape[-1] // 3
    ff = w_ffn_gate.shape[-1]
    n_split = 2 * aw + 4 * bw
    hw = d // C_HEADS
    _, n_pool, page, _, _, hd = cache_k.shape
    row2 = lambda a: a.reshape(1, -1)

    w_even = w_in_even[0].astype(BF16)
    w_ba = jnp.pad(w_in_even[0, :, n_split:], ((0, 0), (0, 128 - 2 * B_HEADS))).astype(BF16)
    prm = jnp.zeros((8, 128), F32)
    prm = prm.at[0, B_HEADS:2 * B_HEADS].set(b_dt_bias[0]).at[1, B_HEADS:2 * B_HEADS].set(b_a_log[0])
    onorm = row2(b_out_norm[0])
    w_oe = w_out_even[0].astype(BF16)
    w_odd = w_in_odd[0].astype(BF16)
    w_oo = w_out_odd[0].astype(BF16)
    wg = w_ffn_gate.astype(BF16)
    wu = w_ffn_up.astype(BF16)
    wd = w_ffn_down.astype(BF16)
    slopes = jnp.exp2(-8.0 * jnp.arange(1, C_HEADS + 1, dtype=F32) / C_HEADS)
    slope_tile = jnp.broadcast_to(jnp.tile(slopes, 2)[:, None], (2 * C_HEADS, 128))
    lam_init = 0.8 - 0.6 * math.exp(-0.3 * 1)
    ws_step = row2(jnp.repeat(a_w_s[0, :, 0, 0], aw // A_GROUPS))
    bs_step = row2(jnp.repeat(a_b_s[0, :, 0], aw // A_GROUPS))
    bs_t = a_b_s[0].T

    def ffn_layer(layer, x1, **kw):
        return _ffn(x1, row2(norm_ffn_pre[layer]), wg, wu, wd, w_ffn_conv[layer],
                    row2(b_ffn_conv[layer]), row2(norm_ffn_post[layer]), layer=layer, **kw)

    tm = _row_tile(b * t, 512)
    tps = t // tm
    xp = x_prompt.reshape(b * t, d)
    a_out, qkv, z, ba = _even_in(xp, row2(norm_mix_pre[0]), w_even, w_ba, row2(a_v_norm[0]),
                                 a_w_s[0], bs_t, chunk=A_CHUNK, tm=tm)
    qkv3 = qkv.reshape(b, t, 3 * bw)
    o, gdn_state_p = _gdn_prompt(qkv3, ba.reshape(b, t, 128), z.reshape(b, t, bw), b_conv_w[0],
                                 prm, onorm)
    gdn_conv_p = qkv3[:, t - (B_CONV - 1):, :]
    x1 = _mix_out([a_out, o.reshape(b * t, bw)], xp, w_oe, row2(norm_mix_post[0]), tm=tm)
    x2, gt0 = ffn_layer(0, x1, tm=tm, tiles_per_seq=tps)
    qh, kh, vt, kt_p, v_p = _odd_in(x2, row2(norm_mix_pre[1]), w_odd, tm=tm, seq_len=t)
    att = _attn_prompt(slopes, qh, kh, vt, c_lambda[0], row2(c_subln[0]), tq=tm, lam_init=lam_init)
    x3 = _mix_out([att.reshape(b * t, d)], x2, w_oo, row2(norm_mix_post[1]), tm=tm)
    y_p, gt1 = ffn_layer(1, x3, tm=tm, tiles_per_seq=tps)
    ffn_conv_p = jnp.stack([g.reshape(b, tps, 8, ff)[:, -1, 8 - (FFN_CONV - 1):, :]
                            for g in (gt0, gt1)])

    xs = x_sample.reshape(bs, d)
    a_out_s, v_s, qkv_s, z_s, ba_s = _even_in(xs, row2(norm_mix_pre[0]), w_even, w_ba,
                                              row2(a_v_norm[0]), ws_step, bs_step, chunk=1, tm=bs)
    o_s, gdn_state_s = _gdn_step(qkv_s, state_gdn_conv[0].reshape(bs, -1), ba_s, z_s, state_gdn[0],
                                 b_conv_w[0], prm, onorm, nb=min(16, bs))
    x1s = _mix_out([a_out_s, o_s], xs, w_oe, row2(norm_mix_post[0]), tm=bs)
    x2s, g0s = ffn_layer(0, x1s, tm=bs, hist=state_ffn_conv[0].reshape(bs, -1))
    q_s, k_s, v_sn = _odd_in(x2s, row2(norm_mix_pre[1]), w_odd, tm=bs)
    n_pages = page_table.shape[1]
    pg = next(c for c in (16, 8, 4, 2, 1) if n_pages % c == 0)
    cache_kt = jnp.transpose(cache_k[0], (0, 2, 3, 4, 1)).reshape(n_pool, d, page)
    cache_v2 = cache_v[0].reshape(n_pool, page * C_HEADS, hw)
    att_s = _attn_decode(page_table, q_s, k_s, v_sn, slope_tile, c_lambda[0], row2(c_subln[0]),
                         cache_kt, cache_v2, pg=pg, lam_init=lam_init)
    x3s = _mix_out([att_s], x2s, w_oo, row2(norm_mix_post[1]), tm=bs)
    y_s, g1s = ffn_layer(1, x3s, tm=bs, hist=state_ffn_conv[1].reshape(bs, -1))
    ffn_conv_s = jnp.stack([jnp.concatenate([state_ffn_conv[l][:, 1:], g[:, None, :]], axis=1)
                            for l, g in ((0, g0s), (1, g1s))])

    return (y_p.reshape(b, t, d), y_s.reshape(bs, 1, d),
            gdn_state_p[None], gdn_state_s[None],
            gdn_conv_p[None],
            jnp.concatenate([state_gdn_conv[0][:, 1:], qkv_s[:, None, :]], axis=1)[None],
            v_s.reshape(1, bs, 1, aw),
            jnp.transpose(kt_p.reshape(1, b, C_HEADS, 2, hd, t), (0, 1, 5, 2, 3, 4)),
            v_p.reshape(1, b, t, C_HEADS, hw),
            k_s.reshape(1, bs, 1, C_HEADS, 2, hd), v_sn.reshape(1, bs, 1, C_HEADS, hw),
            ffn_conv_p, ffn_conv_s)
```

```python
import functools
import math

import jax
import jax.numpy as jnp
from jax import lax
from jax.experimental import pallas as pl
from jax.experimental.pallas import tpu as pltpu

F32 = jnp.float32
BF16 = jnp.bfloat16
EPS = 1e-6

A_GROUPS = 4
A_CHUNK = 128
B_HEADS = 4
B_CONV = 4
GDN_STEP = 128
GDN_CHUNK = 64
C_HEADS = 8
FFN_CONV = 3
NEG_BIG = -1e30
LOG2E = 1.4426950408889634
POS_SPLITS = 3

VMEM_LIMIT_BYTES = 56 * 1024 * 1024
HIGHEST = lax.Precision.HIGHEST


def _cparams(sem):
    return pltpu.CompilerParams(dimension_semantics=sem, vmem_limit_bytes=VMEM_LIMIT_BYTES)


def _gelu(x):
    return 0.5 * x * (1.0 + jnp.tanh(0.7978845608028654 * (x + 0.044715 * (x * x * x))))


def _sigmoid(x):
    return 1.0 / (1.0 + jnp.exp(-x))


def _softplus(x):
    return jnp.maximum(x, 0.0) + jnp.log(1.0 + jnp.exp(-jnp.abs(x)))


def _rms(x, gain):
    return x * lax.rsqrt(jnp.mean(x * x, axis=-1, keepdims=True) + EPS) * gain


def _dot(a, b, precision=None):
    return jnp.dot(a, b, preferred_element_type=F32, precision=precision)


def _dot_nt(a, b, precision=None):
    return lax.dot_general(a, b, (((1,), (1,)), ((), ())), preferred_element_type=F32,
                           precision=precision)


def _split(a):
    hi = a.astype(BF16)
    return hi, (a - hi.astype(F32)).astype(BF16)


def _dot3(a, b):
    (ah, al), (bh, bl) = a, b
    return _dot(ah, bh) + _dot(ah, bl) + _dot(al, bh)


def _const_spec(shape):
    nd = len(shape)
    return pl.BlockSpec(shape, lambda *_: (0,) * nd, pipeline_mode=pl.Buffered(1))


def _even_in_kernel(x_ref, gpre_ref, w_ref, wba_ref, avn_ref, ws_ref, bs_ref, *rest,
                    chunk, aw, bw):
    tm = x_ref.shape[0]
    if chunk == 1:
        aout_ref, v_ref, qkv_ref, z_ref, ba_ref = rest
    else:
        aout_ref, qkv_ref, z_ref, ba_ref = rest
    xb = _rms(x_ref[...], gpre_ref[...]).astype(BF16)
    u = _gelu(_dot(xb, w_ref[:, 0:aw]))
    v = _rms(_gelu(_dot(xb, w_ref[:, aw:2 * aw])), avn_ref[...])
    qkv_ref[...] = _dot(xb, w_ref[:, 2 * aw:2 * aw + 3 * bw])
    zz = _dot(xb, w_ref[:, 2 * aw + 3 * bw:2 * aw + 4 * bw])
    z_ref[...] = (zz * _sigmoid(zz)).astype(BF16)
    ba_ref[...] = _dot(xb, wba_ref[...])
    gd = aw // A_GROUPS
    if chunk == 1:
        v_ref[...] = v
        aout_ref[...] = (u * (v * ws_ref[...] + bs_ref[...])).astype(BF16)
    else:
        row = lax.broadcasted_iota(jnp.int32, (chunk, chunk), 0)
        col = lax.broadcasted_iota(jnp.int32, (chunk, chunk), 1)
        for g in range(A_GROUPS):
            wt = jnp.where(col <= row, ws_ref[g], 0.0).astype(BF16)
            bcol = bs_ref[:, g:g + 1]
            for n in range(tm // chunk):
                rs = slice(n * chunk, (n + 1) * chunk)
                cs = slice(g * gd, (g + 1) * gd)
                mixed = _dot(wt, v[rs, cs].astype(BF16)) + bcol
                aout_ref[rs, cs] = (u[rs, cs] * mixed).astype(BF16)


def _even_in(x, gpre, w_main, w_ba, avn, ws, bs, *, chunk, tm):
    m, d = x.shape
    aw = avn.shape[-1]
    bw = (w_main.shape[1] - 2 * aw - 2 * B_HEADS) // 4
    row = lambda i: (i, 0)
    in_specs = [pl.BlockSpec((tm, d), row), _const_spec(gpre.shape), _const_spec(w_main.shape),
                _const_spec(w_ba.shape), _const_spec(avn.shape), _const_spec(ws.shape),
                _const_spec(bs.shape)]
    args = [x, gpre, w_main, w_ba, avn, ws, bs]
    tail_specs = [pl.BlockSpec((tm, 3 * bw), row), pl.BlockSpec((tm, bw), row),
                  pl.BlockSpec((tm, 128), row)]
    tail_shape = [jax.ShapeDtypeStruct((m, 3 * bw), F32), jax.ShapeDtypeStruct((m, bw), BF16),
                  jax.ShapeDtypeStruct((m, 128), F32)]
    if chunk == 1:
        out_specs = [pl.BlockSpec((tm, aw), row), pl.BlockSpec((tm, aw), row)] + tail_specs
        out_shape = ([jax.ShapeDtypeStruct((m, aw), BF16), jax.ShapeDtypeStruct((m, aw), F32)]
                     + tail_shape)
    else:
        out_specs = [pl.BlockSpec((tm, aw), row)] + tail_specs
        out_shape = [jax.ShapeDtypeStruct((m, aw), BF16)] + tail_shape
    return pl.pallas_call(
        functools.partial(_even_in_kernel, chunk=chunk, aw=aw, bw=bw),
        grid=(m // tm,),
        in_specs=in_specs, out_specs=out_specs, out_shape=out_shape,
        compiler_params=_cparams(("parallel",)),
        name="even_in",
    )(*args)


def _gdn_prompt_kernel(qkv_ref, ba_ref, z_ref, cw_ref, prm_ref, onorm_ref,
                       o_ref, st_ref,
                       ext_ref, wm_ref, u0_ref, qe_ref, qk_ref, kwt_ref, egl_ref, *, bw):
    i = pl.program_id(1)
    n = GDN_STEP
    c = GDN_CHUNK
    hd = bw // B_HEADS

    @pl.when(i == 0)
    def _():
        st_ref[...] = jnp.zeros_like(st_ref)
        ext_ref[n:n + 8, :] = jnp.zeros((8, ext_ref.shape[1]), F32)
        wm_ref[...] = jnp.zeros_like(wm_ref)
        u0_ref[...] = jnp.zeros_like(u0_ref)
        qe_ref[...] = jnp.zeros_like(qe_ref)
        qk_ref[...] = jnp.zeros_like(qk_ref)
        kwt_ref[...] = jnp.zeros_like(kwt_ref)
        egl_ref[...] = jnp.zeros_like(egl_ref)

    heads = range(B_HEADS)
    zero_half = jnp.zeros((c, hd), F32)
    egl = [(egl_ref[0:1, B_HEADS + h:B_HEADS + h + 1], egl_ref[c:c + 1, B_HEADS + h:B_HEADS + h + 1])
           for h in heads]
    state = {"s": [st_ref[0, h] for h in heads]}

    def advance_stage_u(r0):
        state["sb"] = [s.astype(BF16) for s in state["s"]]
        state["u"] = [u0_ref[h, r0:r0 + c, :] - _dot(wm_ref[h, r0:r0 + c, :], state["sb"][h])
                      for h in heads]

    def advance_stage_s(r0, part):
        halves = [[u, zero_half] if part == 0 else [zero_half, u] for u in state["u"]]
        uf = [jnp.concatenate(hv, axis=0).astype(BF16) for hv in halves]
        o = [_dot(qe_ref[h, r0:r0 + c, :], state["sb"][h]) + _dot(qk_ref[h, r0:r0 + c, :], uf[h])
             for h in heads]
        state["s"] = [egl[h][part] * state["s"][h] + _dot(kwt_ref[h], uf[h]) for h in heads]
        return o

    ext_ref[5:8, :] = ext_ref[n + 5:n + 8, :]
    cur = qkv_ref[0]
    ext_ref[8:n + 8, :] = cur
    y = (cw_ref[0:1, :] * ext_ref[5:n + 5, :] + cw_ref[1:2, :] * ext_ref[6:n + 6, :]
         + cw_ref[2:3, :] * ext_ref[7:n + 7, :] + cw_ref[3:4, :] * cur)
    y = y * _sigmoid(y)

    ba = ba_ref[0]
    beta = _sigmoid(ba)
    gfull = -jnp.exp(prm_ref[1:2, :]) * _softplus(ba + prm_ref[0:1, :])
    row = lax.broadcasted_iota(jnp.int32, (n, n), 0)
    col = lax.broadcasted_iota(jnp.int32, (n, n), 1)
    same = (row < c) == (col < c)
    incl = same & (col <= row)
    strict = same & (col < row)
    eye = jnp.where(row == col, 1.0, 0.0)

    advance_stage_u(0)
    g_hi, g_lo = _split(gfull)
    ones_incl = jnp.where(incl, 1.0, 0.0).astype(BF16)
    ones_same = jnp.where(same, 1.0, 0.0).astype(BF16)
    gcum = _dot(ones_incl, g_hi) + _dot(ones_incl, g_lo)
    glast = _dot(ones_same, g_hi) + _dot(ones_same, g_lo)
    gcum_t = gcum.T

    qs, ks, vs, kbs, decays = [], [], [], [], []
    for h in heads:
        q = y[:, h * hd:(h + 1) * hd]
        k = y[:, bw + h * hd:bw + (h + 1) * hd]
        qs.append(q * lax.rsqrt(jnp.sum(q * q, axis=-1, keepdims=True) + EPS) * (hd ** -0.5))
        ks.append(k * lax.rsqrt(jnp.sum(k * k, axis=-1, keepdims=True) + EPS))
        vs.append(y[:, 2 * bw + h * hd:2 * bw + (h + 1) * hd])
        kbs.append(ks[h].astype(BF16))
        gcol = gcum[:, B_HEADS + h:B_HEADS + h + 1]
        grow = gcum_t[B_HEADS + h:B_HEADS + h + 1, :]
        decays.append(jnp.where(incl, jnp.exp(jnp.where(incl, gcol - grow, 0.0)), 0.0))
    bcols = [beta[:, h:h + 1] for h in heads]
    gcols = [gcum[:, B_HEADS + h:B_HEADS + h + 1] for h in heads]
    glcols = [glast[:, B_HEADS + h:B_HEADS + h + 1] for h in heads]

    kk = [_dot_nt(kbs[h], kbs[h]) for h in heads]
    o_a = advance_stage_s(0, 0)
    x = [-jnp.where(strict, bcols[h] * decays[h] * kk[h], 0.0) for h in heads]
    p = [eye + x[h] for h in heads]
    xs = [_split(x[h]) for h in heads]
    o_b = None
    for it in range(int(math.log2(c)) - 1):
        xs = [_split(_dot3(xs[h], xs[h])) for h in heads]
        if it == 0:
            advance_stage_u(c)
        p = [p[h] + _dot3(_split(p[h]), xs[h]) for h in heads]
        if it == 1:
            o_b = advance_stage_s(c, 1)
    egs = [jnp.exp(gcols[h]) for h in heads]
    ps = [_split(p[h]) for h in heads]
    wm = [_dot3(ps[h], _split(bcols[h] * egs[h] * ks[h])) for h in heads]
    u0 = [_dot3(ps[h], _split(bcols[h] * vs[h])) for h in heads]
    qk = [_dot_nt(qs[h].astype(BF16), kbs[h]) for h in heads]

    for h in heads:
        st_ref[0, h] = state["s"][h]
        o = jnp.concatenate([o_a[h], o_b[h]], axis=0)
        cs = slice(h * hd, (h + 1) * hd)
        o_ref[0, :, cs] = (_rms(o, onorm_ref[...]) * z_ref[0, :, cs].astype(F32)).astype(BF16)
        wm_ref[h] = wm[h].astype(BF16)
        u0_ref[h] = u0[h]
        qe_ref[h] = (qs[h] * egs[h]).astype(BF16)
        qk_ref[h] = (qk[h] * decays[h]).astype(BF16)
        kwt_ref[h] = (ks[h] * jnp.exp(glcols[h] - gcols[h])).T.astype(BF16)
    egl_ref[...] = jnp.exp(glast)


def _gdn_prompt(qkv, ba, z, cw, prm, onorm):
    b, t, c3 = qkv.shape
    bw = c3 // 3
    hd = bw // B_HEADS
    n = GDN_STEP
    nt = t // n
    cur = lambda bi, i: (bi, jnp.minimum(i, nt - 1), 0)
    prev = lambda bi, i: (bi, jnp.maximum(i - 1, 0), 0)
    return pl.pallas_call(
        functools.partial(_gdn_prompt_kernel, bw=bw),
        grid=(b, nt + 1),
        in_specs=[pl.BlockSpec((1, n, c3), cur), pl.BlockSpec((1, n, 128), cur),
                  pl.BlockSpec((1, n, bw), prev), _const_spec(cw.shape), _const_spec(prm.shape),
                  _const_spec(onorm.shape)],
        out_specs=[pl.BlockSpec((1, n, bw), prev),
                   pl.BlockSpec((1, B_HEADS, hd, hd), lambda bi, i: (bi, 0, 0, 0))],
        out_shape=[jax.ShapeDtypeStruct((b, t, bw), BF16),
                   jax.ShapeDtypeStruct((b, B_HEADS, hd, hd), F32)],
        scratch_shapes=[pltpu.VMEM((n + 8, c3), F32),
                        pltpu.VMEM((B_HEADS, n, hd), BF16), pltpu.VMEM((B_HEADS, n, hd), F32),
                        pltpu.VMEM((B_HEADS, n, hd), BF16), pltpu.VMEM((B_HEADS, n, n), BF16),
                        pltpu.VMEM((B_HEADS, hd, n), BF16), pltpu.VMEM((n, 128), F32)],
        compiler_params=_cparams(("arbitrary", "arbitrary")),
        name="gdn_prompt",
    )(qkv, ba, z, cw, prm, onorm)


def _gdn_step_kernel(qkv_ref, cst_ref, ba_ref, z_ref, st_ref, cw_ref, prm_ref, onorm_ref,
                     o_ref, sto_ref, *, bw):
    nb = qkv_ref.shape[0]
    hd = bw // B_HEADS
    c3 = 3 * bw
    y = (cw_ref[0:1, :] * cst_ref[:, 0:c3] + cw_ref[1:2, :] * cst_ref[:, c3:2 * c3]
         + cw_ref[2:3, :] * cst_ref[:, 2 * c3:3 * c3] + cw_ref[3:4, :] * qkv_ref[...])
    y = y * _sigmoid(y)
    ba = ba_ref[...]
    beta = _sigmoid(ba)
    eg_all = jnp.exp(-jnp.exp(prm_ref[1:2, :]) * _softplus(ba + prm_ref[0:1, :]))
    row = lax.broadcasted_iota(jnp.int32, (hd, hd), 0)
    col = lax.broadcasted_iota(jnp.int32, (hd, hd), 1)
    eye = jnp.where(row == col, 1.0, 0.0)
    for h in range(B_HEADS):
        q = y[:, h * hd:(h + 1) * hd]
        k = y[:, bw + h * hd:bw + (h + 1) * hd]
        v = y[:, 2 * bw + h * hd:2 * bw + (h + 1) * hd]
        q = q * lax.rsqrt(jnp.sum(q * q, axis=-1, keepdims=True) + EPS) * (hd ** -0.5)
        k = k * lax.rsqrt(jnp.sum(k * k, axis=-1, keepdims=True) + EPS)
        qk = jnp.sum(q * k, axis=-1, keepdims=True)
        k_t = _dot_nt(eye, k, HIGHEST)
        q_t = _dot_nt(eye, q, HIGHEST)
        o_rows = []
        for j in range(nb):
            s = st_ref[j, h]
            eg = eg_all[j:j + 1, B_HEADS + h:B_HEADS + h + 1]
            bt = beta[j:j + 1, h:h + 1]
            kc = k_t[:, j:j + 1]
            ks = jnp.sum(kc * s, axis=0, keepdims=True)
            qs = jnp.sum(q_t[:, j:j + 1] * s, axis=0, keepdims=True)
            u = bt * (v[j:j + 1, :] - eg * ks)
            o_rows.append(eg * qs + qk[j:j + 1, :] * u)
            sto_ref[j, h] = eg * s + kc * u
        o = jnp.concatenate(o_rows, axis=0)
        cs = slice(h * hd, (h + 1) * hd)
        o_ref[:, cs] = (_rms(o, onorm_ref[...]) * z_ref[:, cs].astype(F32)).astype(BF16)


def _gdn_step(qkv, cst, ba, z, state, cw, prm, onorm, *, nb):
    m, c3 = qkv.shape
    bw = c3 // 3
    hd = bw // B_HEADS
    row = lambda i: (i, 0)
    st = lambda i: (i, 0, 0, 0)
    return pl.pallas_call(
        functools.partial(_gdn_step_kernel, bw=bw),
        grid=(m // nb,),
        in_specs=[pl.BlockSpec((nb, c3), row), pl.BlockSpec((nb, 3 * c3), row),
                  pl.BlockSpec((nb, 128), row), pl.BlockSpec((nb, bw), row),
                  pl.BlockSpec((nb, B_HEADS, hd, hd), st), _const_spec(cw.shape),
                  _const_spec(prm.shape), _const_spec(onorm.shape)],
        out_specs=[pl.BlockSpec((nb, bw), row), pl.BlockSpec((nb, B_HEADS, hd, hd), st)],
        out_shape=[jax.ShapeDtypeStruct((m, bw), BF16),
                   jax.ShapeDtypeStruct((m, B_HEADS, hd, hd), F32)],
        compiler_params=_cparams(("parallel",)),
        name="gdn_step",
    )(qkv, cst, ba, z, state, cw, prm, onorm)


def _mix_out_kernel(*refs, n_in):
    ins = refs[:n_in]
    x_ref, w_ref, gpost_ref, x1_ref = refs[n_in:]
    mix = None
    r0 = 0
    for a_ref in ins:
        kk = a_ref.shape[1]
        part = _dot(a_ref[...], w_ref[r0:r0 + kk, :])
        mix = part if mix is None else mix + part
        r0 += kk
    x1_ref[...] = x_ref[...] + _rms(mix, gpost_ref[...])


def _mix_out(ins, x, w, gpost, *, tm):
    m, d = x.shape
    row = lambda i: (i, 0)
    return pl.pallas_call(
        functools.partial(_mix_out_kernel, n_in=len(ins)),
        grid=(m // tm,),
        in_specs=[pl.BlockSpec((tm, a.shape[1]), row) for a in ins]
        + [pl.BlockSpec((tm, d), row), _const_spec(w.shape), _const_spec(gpost.shape)],
        out_specs=pl.BlockSpec((tm, d), row),
        out_shape=jax.ShapeDtypeStruct((m, d), F32),
        compiler_params=_cparams(("parallel",)),
        name="mix_out",
    )(*ins, x, w, gpost)


def _ffn_kernel(x_ref, gpre_ref, wg_ref, wu_ref, wd_ref, wc_ref, bc_ref, gpost_ref, *rest,
                tiles_per_seq, fc):
    tm = x_ref.shape[0]
    ff = wg_ref.shape[1]
    if tiles_per_seq is None:
        hist_ref, x2_ref, gout_ref = rest
    else:
        x2_ref, gout_ref, carry_ref, gbuf_ref = rest

        @pl.when(pl.program_id(0) % tiles_per_seq == 0)
        def _():
            carry_ref[...] = jnp.zeros_like(carry_ref)

    hb = _rms(x_ref[...], gpre_ref[...]).astype(BF16)

    chunks = [slice(c0, min(c0 + fc, ff)) for c0 in range(0, ff, fc)]

    def gate_up(cs):
        return _dot(hb, wg_ref[:, cs]), _dot(hb, wu_ref[:, cs])

    acc = None
    ahead = gate_up(chunks[0])
    for ci, cs in enumerate(chunks):
        g, up = ahead
        if ci + 1 < len(chunks):
            ahead = gate_up(chunks[ci + 1])
        if tiles_per_seq is None:
            gout_ref[:, cs] = g
            sh2 = hist_ref[:, cs]
            sh1 = hist_ref[:, ff + cs.start:ff + cs.stop]
        else:
            gbuf_ref[0:8, cs] = carry_ref[:, cs]
            gbuf_ref[8:tm + 8, cs] = g
            sh2 = gbuf_ref[6:tm + 6, cs]
            sh1 = gbuf_ref[7:tm + 7, cs]
            tail = gbuf_ref[tm:tm + 8, cs]
            carry_ref[:, cs] = tail
            gout_ref[0, :, cs] = tail
        conv = (wc_ref[0:1, cs] * sh2 + wc_ref[1:2, cs] * sh1 + wc_ref[2:3, cs] * g
                + bc_ref[:, cs])
        a = (_gelu(conv) * up).astype(BF16)
        part = _dot(a, wd_ref[cs, :])
        acc = part if acc is None else acc + part
    x2_ref[...] = x_ref[...] + _rms(acc, gpost_ref[...])


def _ffn(x1, gpre, wg, wu, wd, wc, bc, gpost, *, layer, tm, tiles_per_seq=None, hist=None, fc=512):
    m, d = x1.shape
    ff = wg.shape[2]
    row = lambda i: (i, 0)

    def layer_spec(w):
        return pl.BlockSpec((None,) + w.shape[1:], lambda i: (layer, 0, 0),
                            pipeline_mode=pl.Buffered(1))

    in_specs = [pl.BlockSpec((tm, d), row), _const_spec(gpre.shape), layer_spec(wg),
                layer_spec(wu), layer_spec(wd), _const_spec(wc.shape),
                _const_spec(bc.shape), _const_spec(gpost.shape)]
    args = [x1, gpre, wg, wu, wd, wc, bc, gpost]
    if tiles_per_seq is None:
        in_specs.append(pl.BlockSpec((tm, 2 * ff), row))
        args.append(hist)
        out_specs = [pl.BlockSpec((tm, d), row), pl.BlockSpec((tm, ff), row)]
        out_shape = [jax.ShapeDtypeStruct((m, d), F32), jax.ShapeDtypeStruct((m, ff), F32)]
        scratch = []
    else:
        out_specs = [pl.BlockSpec((tm, d), row), pl.BlockSpec((1, 8, ff), lambda i: (i, 0, 0))]
        out_shape = [jax.ShapeDtypeStruct((m, d), F32),
                     jax.ShapeDtypeStruct((m // tm, 8, ff), F32)]
        scratch = [pltpu.VMEM((8, ff), F32), pltpu.VMEM((tm + 8, ff), F32)]
    return pl.pallas_call(
        functools.partial(_ffn_kernel, tiles_per_seq=tiles_per_seq, fc=fc),
        grid=(m // tm,),
        in_specs=in_specs, out_specs=out_specs, out_shape=out_shape, scratch_shapes=scratch,
        compiler_params=_cparams(("arbitrary",)),
        name="conv_ffn",
    )(*args)


def _odd_in_kernel(x_ref, gpre_ref, w_ref, *outs, head_major, qscale):
    d = x_ref.shape[1]
    xb = _rms(x_ref[...], gpre_ref[...]).astype(BF16)
    q = _dot(xb, w_ref[:, 0:d]) * qscale
    k = _dot(xb, w_ref[:, d:2 * d])
    v = _dot(xb, w_ref[:, 2 * d:3 * d])
    if head_major:
        qh_ref, kh_ref, vt_ref, kt_ref, v_ref = outs
        tm = x_ref.shape[0]
        kt_ref[0] = k.T
        v_ref[...] = v
        vt = v.T
        hw = d // C_HEADS
        rowi = lax.broadcasted_iota(jnp.int32, (tm, hw), 0)
        lane = lax.broadcasted_iota(jnp.int32, (tm, hw), 1)
        pos = jnp.where((lane & 1) == 0, lax.shift_right_logical(rowi, 4), rowi & 15)
        feat = jnp.where(lane < 2 * POS_SPLITS, pos, 0).astype(F32).astype(BF16)
        ones_pad = jnp.where(lax.broadcasted_iota(jnp.int32, (16, tm), 0) == 0, 1.0, 0.0)
        for h in range(C_HEADS):
            cs = slice(h * hw, (h + 1) * hw)
            qh_ref[0, h] = q[:, cs].astype(BF16)
            kh_ref[0, h, :, 0:hw] = k[:, cs].astype(BF16)
            kh_ref[0, h, :, hw:2 * hw] = feat
            vt_ref[0, h, 0] = jnp.concatenate([vt[cs, :], ones_pad], axis=0).astype(BF16)
    else:
        q_ref, k_ref, v_ref = outs
        q_ref[...] = q
        k_ref[...] = k
        v_ref[...] = v


def _odd_in(x, gpre, w, *, tm, seq_len=None):
    m, d = x.shape
    qscale = (d // (2 * C_HEADS)) ** -0.5
    row = lambda i: (i, 0)
    head_major = seq_len is not None
    if head_major:
        assert tm <= 512
        qscale *= LOG2E
        tps = seq_len // tm
        hw = d // C_HEADS
        nb = m // seq_len
        hm = lambda i: (i // tps, 0, i % tps, 0)
        out_specs = [pl.BlockSpec((1, C_HEADS, tm, hw), hm),
                     pl.BlockSpec((1, C_HEADS, tm, 2 * hw), hm),
                     pl.BlockSpec((1, C_HEADS, 1, hw + 16, tm),
                                  lambda i: (i // tps, 0, i % tps, 0, 0)),
                     pl.BlockSpec((1, d, tm), lambda i: (i // tps, 0, i % tps)),
                     pl.BlockSpec((tm, d), row)]
        out_shape = [jax.ShapeDtypeStruct((nb, C_HEADS, seq_len, hw), BF16),
                     jax.ShapeDtypeStruct((nb, C_HEADS, seq_len, 2 * hw), BF16),
                     jax.ShapeDtypeStruct((nb, C_HEADS, tps, hw + 16, tm), BF16),
                     jax.ShapeDtypeStruct((nb, d, seq_len), F32),
                     jax.ShapeDtypeStruct((m, d), F32)]
    else:
        out_specs = [pl.BlockSpec((tm, d), row)] * 3
        out_shape = [jax.ShapeDtypeStruct((m, d), F32)] * 3
    return pl.pallas_call(
        functools.partial(_odd_in_kernel, head_major=head_major, qscale=qscale),
        grid=(m // tm,),
        in_specs=[pl.BlockSpec((tm, d), row), _const_spec(gpre.shape), _const_spec(w.shape)],
        out_specs=out_specs, out_shape=out_shape,
        compiler_params=_cparams(("parallel",)),
        name="odd_in",
    )(x, gpre, w)


def _lambda_value(lam_ref, lam_init):
    l = lam_ref[...]
    a = jnp.sum(l[0:1, :] * l[1:2, :], axis=-1, keepdims=True)
    b = jnp.sum(l[2:3, :] * l[3:4, :], axis=-1, keepdims=True)
    return jnp.exp(a) - jnp.exp(b) + lam_init


def _attn_prompt_kernel(slope2_ref, q_ref, k_ref, vt_ref, lam_ref, subln_ref, o_ref,
                        qs_ref, qn_ref, s_ref, m_ref, acc_ref, *, tq, tk, qb, lam_init,
                        side_work=None):
    h = pl.program_id(1)
    iq = pl.program_id(2)
    nq = pl.num_programs(2)
    hw = q_ref.shape[3]
    hd = hw // 2
    r = 2 * tq
    slope2 = slope2_ref[h]

    lane1 = lax.broadcasted_iota(jnp.int32, (1, hw), 1)
    rest = jnp.full((1, hw), slope2, F32)
    feat = jnp.zeros((1, hw), F32)
    for i in range(POS_SPLITS):
        piece = rest.astype(BF16).astype(F32)
        rest = rest - piece
        feat = jnp.where(lane1 == 2 * i, 16.0 * piece, jnp.where(lane1 == 2 * i + 1, piece, feat))
    feat = jnp.broadcast_to(feat, (r, hw)).astype(BF16)
    lane = lax.broadcasted_iota(jnp.int32, (tq, hw), 1)

    def stack_queries(dst_ref, tile):
        q = q_ref[0, 0, pl.ds(pl.multiple_of(tile * tq, tq), tq), :]
        dst_ref[0:tq, 0:hw] = jnp.where(lane < hd, q, jnp.zeros_like(q))
        dst_ref[tq:r, 0:hw] = jnp.where(lane >= hd, q, jnp.zeros_like(q))
        dst_ref[:, hw:2 * hw] = feat

    stack_queries(qs_ref, iq)
    stack_queries(qn_ref, jnp.minimum(iq + 1, nq - 1))
    m_ref[...] = jnp.full_like(m_ref, NEG_BIG)
    acc_ref[...] = jnp.zeros_like(acc_ref)

    def keys(kc):
        return k_ref[0, 0, pl.ds(pl.multiple_of(kc * tk, tk), tk), :]

    def step(kc, kb_next, qsrc_ref, masked, between=None):
        vtb = vt_ref[0, 0, kc]
        cshift = slope2 * (kc * tk).astype(F32)
        for c0 in range(0, r, qb):
            if between is not None and c0 == r // 2:
                between()
            cols = slice(c0, c0 + qb)
            kn = min(tk, (c0 % tq) + qb) if masked else tk
            s = s_ref[0:kn, cols]
            s_ref[:, cols] = _dot_nt(kb_next, qsrc_ref[cols, :])
            if masked:
                key_i = lax.broadcasted_iota(jnp.int32, (kn, qb), 0)
                qry_i = lax.broadcasted_iota(jnp.int32, (kn, qb), 1)
                s = jnp.where(key_i <= (c0 % tq) + qry_i, s, NEG_BIG)
            m_old = m_ref[:, cols]
            m_new = jnp.maximum(m_old, jnp.max(s, axis=0, keepdims=True) + cshift)
            p = jnp.exp2(s - (m_new - cshift)).astype(BF16)
            alpha = jnp.exp2(m_old - m_new)
            acc_ref[:, cols] = alpha * acc_ref[:, cols] + _dot(vtb[:, 0:kn], p)
            m_ref[:, cols] = m_new

    @pl.when(iq == 0)
    def _():
        s_ref[...] = _dot_nt(keys(0), qs_ref[...])

    done = 0
    for width in (8, 4, 2, 1):
        trips = lax.shift_right_logical(iq - done, width.bit_length() - 1)

        def body(j, carry, first=done, width=width):
            for u in range(width):
                kc = first + width * j + u
                step(kc, keys(kc + 1), qs_ref, False)
            return carry

        lax.fori_loop(0, trips, body, 0)
        done = done + width * trips
    before, between, after = side_work or (None, None, None)
    if before is not None:
        before()
    step(iq, keys(0), qn_ref, True, between)
    if after is not None:
        after()

    lam = _lambda_value(lam_ref, lam_init)
    o1 = acc_ref[0:hw, 0:tq] / acc_ref[hw:hw + 1, 0:tq]
    o2 = acc_ref[0:hw, tq:r] / acc_ref[hw:hw + 1, tq:r]
    att = (o1 - lam * o2).T
    o_ref[0] = (_rms(att, subln_ref[...]) * (1.0 - lam_init)).astype(BF16)


def _attn_prompt(slopes, qh, kh, vt, lam_p, subln, *, tq, lam_init):
    b, nh, t, hw = qh.shape
    tk = vt.shape[4]
    assert tq == tk
    return pl.pallas_call(
        functools.partial(_attn_prompt_kernel, tq=tq, tk=tk, qb=min(256, tq), lam_init=lam_init),
        grid=(b, nh, t // tq),
        in_specs=[pl.BlockSpec(memory_space=pltpu.SMEM),
                  pl.BlockSpec((1, 1, t, hw), lambda bi, h, iq: (bi, h, 0, 0)),
                  pl.BlockSpec((1, 1, t, 2 * hw), lambda bi, h, iq: (bi, h, 0, 0)),
                  pl.BlockSpec((1, 1, t // tk, hw + 16, tk), lambda bi, h, iq: (bi, h, 0, 0, 0)),
                  _const_spec(lam_p.shape), _const_spec(subln.shape)],
        out_specs=pl.BlockSpec((1, tq, hw), lambda bi, h, iq: (bi, iq, h)),
        out_shape=jax.ShapeDtypeStruct((b, t, nh * hw), BF16),
        scratch_shapes=[pltpu.VMEM((2 * tq, 2 * hw), BF16), pltpu.VMEM((2 * tq, 2 * hw), BF16),
                        pltpu.VMEM((tk, 2 * tq), F32), pltpu.VMEM((1, 2 * tq), F32),
                        pltpu.VMEM((hw + 16, 2 * tq), F32)],
        compiler_params=_cparams(("arbitrary", "arbitrary", "arbitrary")),
        name="attn_prompt",
    )(slopes * LOG2E, qh, kh, vt, lam_p, subln)


def _decode_stages(q_ref, kn_ref, vn_ref, slope_ref, expand_ref, lam_ref, subln_ref, k_refs,
                   v_refs, o_ref, m_ref, l_ref, acc_ref, *, j, live, n_past, lam_init):
    pg = len(k_refs)
    _, d, page = k_refs[0].shape
    nh = C_HEADS
    nr = 2 * nh
    hd = d // nr
    st = {}

    def scores():
        slope = slope_ref[:, 0:1]
        rowi = lax.broadcasted_iota(jnp.int32, (nr, d), 0)
        coli = lax.broadcasted_iota(jnp.int32, (nr, d), 1)
        lo = (jnp.where(rowi >= nh, rowi - nh, rowi) * 2 + jnp.where(rowi >= nh, 1, 0)) * hd
        qbd = jnp.where((coli >= lo) & (coli < lo + hd), q_ref[0], 0.0)
        qb = qbd.astype(BF16)
        tpos = lax.broadcasted_iota(jnp.int32, (1, page), 1)
        s_parts = []
        for p_i in range(pg):
            dist = (n_past - ((j * pg + p_i) * page + tpos)).astype(F32)
            s_parts.append(_dot(qb, k_refs[p_i][0].astype(BF16)) - slope * dist)
        s = jnp.where(live, jnp.concatenate(s_parts, axis=1), NEG_BIG)
        first = jnp.logical_and(j == 0, live)
        m_old = jnp.where(first, NEG_BIG, m_ref[...])
        m_new = jnp.maximum(m_old, jnp.max(s, axis=-1, keepdims=True))
        p = jnp.exp(s - m_new).astype(BF16)
        alpha = jnp.exp(m_old - m_new)
        l = alpha * jnp.where(first, 0.0, l_ref[...]) + jnp.sum(p.astype(F32), axis=-1,
                                                                  keepdims=True)
        st.update(qbd=qbd, p=p, alpha=alpha, first=first, m=m_new, l=l)
        m_ref[...] = m_new
        l_ref[...] = l

    def spread():
        rowe = lax.broadcasted_iota(jnp.int32, (nr, nh * page), 0)
        cole = lax.broadcasted_iota(jnp.int32, (nr, nh * page), 1)
        own_head = (cole & (nh - 1)) == jnp.where(rowe >= nh, rowe - nh, rowe)
        p = st["p"]
        st["pbig"] = [jnp.where(own_head, _dot(p[:, p_i * page:(p_i + 1) * page], expand_ref[...]),
                                0.0).astype(BF16) for p_i in range(pg)]

    def values():
        pv = [_dot(st["pbig"][p_i], v_refs[p_i][0].astype(BF16)) for p_i in range(pg)]
        acc = st["alpha"] * jnp.where(st["first"], 0.0, acc_ref[...]) + sum(pv[1:], pv[0])
        acc_ref[...] = acc
        s = jnp.sum(st["qbd"] * kn_ref[0], axis=-1, keepdims=True)
        m_new = jnp.maximum(st["m"], s)
        p = jnp.exp(s - m_new)
        alpha = jnp.exp(st["m"] - m_new)
        l = alpha * st["l"] + p
        vn = vn_ref[0]
        acc = alpha * acc + p * jnp.concatenate([vn, vn], axis=0)
        lam = _lambda_value(lam_ref, lam_init)
        att = acc[0:nh, :] / l[0:nh, :] - lam * (acc[nh:nr, :] / l[nh:nr, :])
        o_ref[0] = (_rms(att, subln_ref[...]) * (1.0 - lam_init)).astype(BF16)

    return scores, spread, values


def _decode_scratch_init(m_ref, l_ref, acc_ref):
    m_ref[...] = jnp.full_like(m_ref, NEG_BIG)
    l_ref[...] = jnp.zeros_like(l_ref)
    acc_ref[...] = jnp.zeros_like(acc_ref)


def _attn_decode_kernel(pt_ref, q_ref, kn_ref, vn_ref, slope_ref, expand_ref, lam_ref, subln_ref,
                        *rest, pg, n_past, lam_init):
    k_refs = rest[0:pg]
    v_refs = rest[pg:2 * pg]
    o_ref, m_ref, l_ref, acc_ref = rest[2 * pg:]

    @pl.when((pl.program_id(0) == 0) & (pl.program_id(1) == 0))
    def _():
        _decode_scratch_init(m_ref, l_ref, acc_ref)

    for stage in _decode_stages(q_ref, kn_ref, vn_ref, slope_ref, expand_ref, lam_ref, subln_ref,
                                k_refs, v_refs, o_ref, m_ref, l_ref, acc_ref,
                                j=pl.program_id(1), live=True, n_past=n_past, lam_init=lam_init):
        stage()


def _attn_fused_kernel(pt_ref, slope2_ref, q_ref, k_ref, vt_ref, lam_ref, subln_ref,
                       dq_ref, dkn_ref, dvn_ref, dslope_ref, expand_ref, *rest,
                       pg, steps_per_seq, n_dec, n_past, lam_init, **attn_kw):
    k_refs = rest[0:pg]
    v_refs = rest[pg:2 * pg]
    o_ref, do_ref = rest[2 * pg:2 * pg + 2]
    attn_scratch = rest[2 * pg + 2:-3]
    dm_ref, dl_ref, dacc_ref = rest[-3:]
    g = (pl.program_id(0) * pl.num_programs(1) + pl.program_id(1)) * pl.num_programs(2) \
        + pl.program_id(2)

    @pl.when(g == 0)
    def _():
        _decode_scratch_init(dm_ref, dl_ref, dacc_ref)

    side_work = _decode_stages(dq_ref, dkn_ref, dvn_ref, dslope_ref, expand_ref, lam_ref,
                               subln_ref, k_refs, v_refs, do_ref, dm_ref, dl_ref, dacc_ref,
                               j=jnp.minimum(g, n_dec - 1) % steps_per_seq, live=g < n_dec,
                               n_past=n_past, lam_init=lam_init)
    _attn_prompt_kernel(slope2_ref, q_ref, k_ref, vt_ref, lam_ref, subln_ref, o_ref,
                        *attn_scratch, lam_init=lam_init, side_work=side_work, **attn_kw)


def _attn_decode(page_table, q, kn, vn, slope_tile, lam_p, subln, cache_kt, cache_v2, *, pg,
                 lam_init):
    bs, n_pages = page_table.shape
    _, d, page = cache_kt.shape
    hw = d // C_HEADS
    q3, kn3 = (a.reshape(bs, 1, d) for a in (q, kn))
    vn3 = vn.reshape(bs, C_HEADS, hw)
    expand = (jnp.arange(page * C_HEADS)[None, :] // C_HEADS
              == jnp.arange(page)[:, None]).astype(BF16)
    row = lambda b, j, pt: (b, 0, 0)
    const2 = lambda b, j, pt: (0, 0)

    def page_map(p_i):
        return lambda b, j, pt: (pt[b, j * pg + p_i], 0, 0)

    k_specs = [pl.BlockSpec((1, d, page), page_map(p_i)) for p_i in range(pg)]
    v_specs = [pl.BlockSpec((1, page * C_HEADS, hw), page_map(p_i)) for p_i in range(pg)]
    grid_spec = pltpu.PrefetchScalarGridSpec(
        num_scalar_prefetch=1,
        grid=(bs, n_pages // pg),
        in_specs=[pl.BlockSpec((1, 1, d), row), pl.BlockSpec((1, 1, d), row),
                  pl.BlockSpec((1, C_HEADS, hw), row), pl.BlockSpec(slope_tile.shape, const2),
                  pl.BlockSpec(expand.shape, const2), pl.BlockSpec(lam_p.shape, const2),
                  pl.BlockSpec(subln.shape, const2)] + k_specs + v_specs,
        out_specs=pl.BlockSpec((1, C_HEADS, hw), row),
        scratch_shapes=[pltpu.VMEM((2 * C_HEADS, 1), F32), pltpu.VMEM((2 * C_HEADS, 1), F32),
                        pltpu.VMEM((2 * C_HEADS, hw), F32)],
    )
    out = pl.pallas_call(
        functools.partial(_attn_decode_kernel, pg=pg, n_past=n_pages * page, lam_init=lam_init),
        grid_spec=grid_spec,
        out_shape=jax.ShapeDtypeStruct((bs, C_HEADS, hw), BF16),
        compiler_params=_cparams(("arbitrary", "arbitrary")),
        name="attn_decode",
    )(page_table, q3, kn3, vn3, slope_tile, expand, lam_p, subln, *([cache_kt] * pg),
      *([cache_v2] * pg))
    return out.reshape(bs, d)


def _attn_fused(slopes, qh, kh, vt, lam_p, subln, page_table, q, kn, vn, slope_tile, cache_kt,
                cache_v2, *, tq, pg, lam_init):
    b, nh, t, hw = qh.shape
    tk = vt.shape[4]
    assert tq == tk
    nq = t // tq
    bs, n_pages = page_table.shape
    _, d, page = cache_kt.shape
    spp = n_pages // pg
    n_dec = bs * spp
    assert n_dec <= b * nh * nq
    q3, kn3 = (a.reshape(bs, 1, d) for a in (q, kn))
    vn3 = vn.reshape(bs, C_HEADS, hw)
    expand = (jnp.arange(page * C_HEADS)[None, :] // C_HEADS
              == jnp.arange(page)[:, None]).astype(BF16)

    def dec(bi, h, iq):
        ds = jnp.minimum((bi * nh + h) * nq + iq, n_dec - 1)
        return ds // spp, ds % spp

    seq_row = lambda bi, h, iq, pt: (dec(bi, h, iq)[0], 0, 0)
    const2 = lambda bi, h, iq, pt: (0, 0)
    head = lambda bi, h, iq, pt: (bi, h, 0, 0)

    def page_map(p_i):
        def index(bi, h, iq, pt):
            sb, j = dec(bi, h, iq)
            return pt[sb, j * pg + p_i], 0, 0
        return index

    k_specs = [pl.BlockSpec((1, d, page), page_map(p_i)) for p_i in range(pg)]
    v_specs = [pl.BlockSpec((1, page * C_HEADS, hw), page_map(p_i)) for p_i in range(pg)]
    grid_spec = pltpu.PrefetchScalarGridSpec(
        num_scalar_prefetch=1,
        grid=(b, nh, nq),
        in_specs=[pl.BlockSpec(memory_space=pltpu.SMEM),
                  pl.BlockSpec((1, 1, t, hw), head), pl.BlockSpec((1, 1, t, 2 * hw), head),
                  pl.BlockSpec((1, 1, t // tk, hw + 16, tk), lambda bi, h, iq, pt: (bi, h, 0, 0, 0)),
                  pl.BlockSpec(lam_p.shape, const2), pl.BlockSpec(subln.shape, const2),
                  pl.BlockSpec((1, 1, d), seq_row), pl.BlockSpec((1, 1, d), seq_row),
                  pl.BlockSpec((1, C_HEADS, hw), seq_row), pl.BlockSpec(slope_tile.shape, const2),
                  pl.BlockSpec(expand.shape, const2)] + k_specs + v_specs,
        out_specs=[pl.BlockSpec((1, tq, hw), lambda bi, h, iq, pt: (bi, iq, h)),
                   pl.BlockSpec((1, C_HEADS, hw), seq_row)],
        scratch_shapes=[pltpu.VMEM((2 * tq, 2 * hw), BF16), pltpu.VMEM((2 * tq, 2 * hw), BF16),
                        pltpu.VMEM((tk, 2 * tq), F32), pltpu.VMEM((1, 2 * tq), F32),
                        pltpu.VMEM((hw + 16, 2 * tq), F32),
                        pltpu.VMEM((2 * C_HEADS, 1), F32), pltpu.VMEM((2 * C_HEADS, 1), F32),
                        pltpu.VMEM((2 * C_HEADS, hw), F32)],
    )
    att, att_s = pl.pallas_call(
        functools.partial(_attn_fused_kernel, pg=pg, steps_per_seq=spp, n_dec=n_dec,
                          n_past=n_pages * page, lam_init=lam_init, tq=tq, tk=tk,
                          qb=min(256, tq)),
        grid_spec=grid_spec,
        out_shape=[jax.ShapeDtypeStruct((b, t, nh * hw), BF16),
                   jax.ShapeDtypeStruct((bs, C_HEADS, hw), BF16)],
        compiler_params=_cparams(("arbitrary", "arbitrary", "arbitrary")),
        name="attn_fused",
    )(page_table, slopes * LOG2E, qh, kh, vt, lam_p, subln, q3, kn3, vn3, slope_tile, expand,
      *([cache_kt] * pg), *([cache_v2] * pg))
    return att, att_s.reshape(bs, d)


def _row_tile(m, pref):
    tm = min(pref, m)
    assert m % tm == 0, (m, tm)
    return tm


def kernel(x_prompt, x_sample, cache_k, cache_v, page_table, state_gdn, state_gdn_conv, state_ffn_conv, norm_mix_pre, norm_mix_post, norm_ffn_pre, norm_ffn_post, w_in_even, a_v_norm, a_w_s, a_b_s, b_conv_w, b_a_log, b_dt_bias, b_out_norm, w_out_even, w_in_odd, c_lambda, c_subln, w_out_odd, w_ffn_gate, w_ffn_up, w_ffn_conv, b_ffn_conv, w_ffn_down):
    b, t, d = x_prompt.shape
    bs = x_sample.shape[0]
    assert x_sample.shape[1] == 1 and t % GDN_STEP == 0
    aw = a_v_norm.shape[-1]
    bw = b_conv_w.shape[-1] // 3
    ff = w_ffn_gate.shape[-1]
    n_split = 2 * aw + 4 * bw
    hw = d // C_HEADS
    _, n_pool, page, _, _, hd = cache_k.shape
    row2 = lambda a: a.reshape(1, -1)

    w_even = w_in_even[0].astype(BF16)
    w_ba = jnp.pad(w_in_even[0, :, n_split:], ((0, 0), (0, 128 - 2 * B_HEADS))).astype(BF16)
    prm = jnp.zeros((8, 128), F32)
    prm = prm.at[0, B_HEADS:2 * B_HEADS].set(b_dt_bias[0]).at[1, B_HEADS:2 * B_HEADS].set(b_a_log[0])
    onorm = row2(b_out_norm[0])
    w_oe = w_out_even[0].astype(BF16)
    w_odd = w_in_odd[0].astype(BF16)
    w_oo = w_out_odd[0].astype(BF16)
    wg = w_ffn_gate.astype(BF16)
    wu = w_ffn_up.astype(BF16)
    wd = w_ffn_down.astype(BF16)
    slopes = jnp.exp2(-8.0 * jnp.arange(1, C_HEADS + 1, dtype=F32) / C_HEADS)
    slope_tile = jnp.broadcast_to(jnp.tile(slopes, 2)[:, None], (2 * C_HEADS, 128))
    lam_init = 0.8 - 0.6 * math.exp(-0.3 * 1)
    ws_step = row2(jnp.repeat(a_w_s[0, :, 0, 0], aw // A_GROUPS))
    bs_step = row2(jnp.repeat(a_b_s[0, :, 0], aw // A_GROUPS))
    bs_t = a_b_s[0].T

    def ffn_layer(layer, x1, **kw):
        return _ffn(x1, row2(norm_ffn_pre[layer]), wg, wu, wd, w_ffn_conv[layer],
                    row2(b_ffn_conv[layer]), row2(norm_ffn_post[layer]), layer=layer, **kw)

    tm = _row_tile(b * t, 512)
    tps = t // tm
    xp = x_prompt.reshape(b * t, d)
    a_out, qkv, z, ba = _even_in(xp, row2(norm_mix_pre[0]), w_even, w_ba, row2(a_v_norm[0]),
                                 a_w_s[0], bs_t, chunk=A_CHUNK, tm=tm)
    qkv3 = qkv.reshape(b, t, 3 * bw)
    o, gdn_state_p = _gdn_prompt(qkv3, ba.reshape(b, t, 128), z.reshape(b, t, bw), b_conv_w[0],
                                 prm, onorm)
    gdn_conv_p = qkv3[:, t - (B_CONV - 1):, :]
    x1 = _mix_out([a_out, o.reshape(b * t, bw)], xp, w_oe, row2(norm_mix_post[0]), tm=tm)
    x2, gt0 = ffn_layer(0, x1, tm=tm, tiles_per_seq=tps)
    qh, kh, vt, kt_p, v_p = _odd_in(x2, row2(norm_mix_pre[1]), w_odd, tm=tm, seq_len=t)

    xs = x_sample.reshape(bs, d)
    a_out_s, v_s, qkv_s, z_s, ba_s = _even_in(xs, row2(norm_mix_pre[0]), w_even, w_ba,
                                              row2(a_v_norm[0]), ws_step, bs_step, chunk=1, tm=bs)
    o_s, gdn_state_s = _gdn_step(qkv_s, state_gdn_conv[0].reshape(bs, -1), ba_s, z_s, state_gdn[0],
                                 b_conv_w[0], prm, onorm, nb=min(16, bs))
    x1s = _mix_out([a_out_s, o_s], xs, w_oe, row2(norm_mix_post[0]), tm=bs)
    x2s, g0s = ffn_layer(0, x1s, tm=bs, hist=state_ffn_conv[0].reshape(bs, -1))
    q_s, k_s, v_sn = _odd_in(x2s, row2(norm_mix_pre[1]), w_odd, tm=bs)

    n_pages = page_table.shape[1]
    cache_kt = jnp.transpose(cache_k[0], (0, 2, 3, 4, 1)).reshape(n_pool, d, page)
    cache_v2 = cache_v[0].reshape(n_pool, page * C_HEADS, hw)
    lam_p, subln = c_lambda[0], row2(c_subln[0])
    ride = [c for c in (1, 2, 4, 8) if n_pages % c == 0
            and bs * (n_pages // c) <= b * C_HEADS * (t // tm)]
    if ride:
        att, att_s = _attn_fused(slopes, qh, kh, vt, lam_p, subln, page_table, q_s, k_s, v_sn,
                                 slope_tile, cache_kt, cache_v2, tq=tm, pg=ride[0],
                                 lam_init=lam_init)
    else:
        att = _attn_prompt(slopes, qh, kh, vt, lam_p, subln, tq=tm, lam_init=lam_init)
        pg = next(c for c in (16, 8, 4, 2, 1) if n_pages % c == 0)
        att_s = _attn_decode(page_table, q_s, k_s, v_sn, slope_tile, lam_p, subln, cache_kt,
                             cache_v2, pg=pg, lam_init=lam_init)

    x3 = _mix_out([att.reshape(b * t, d)], x2, w_oo, row2(norm_mix_post[1]), tm=tm)
    y_p, gt1 = ffn_layer(1, x3, tm=tm, tiles_per_seq=tps)
    ffn_conv_p = jnp.stack([g.reshape(b, tps, 8, ff)[:, -1, 8 - (FFN_CONV - 1):, :]
                            for g in (gt0, gt1)])
    x3s = _mix_out([att_s], x2s, w_oo, row2(norm_mix_post[1]), tm=bs)
    y_s, g1s = ffn_layer(1, x3s, tm=bs, hist=state_ffn_conv[1].reshape(bs, -1))
    ffn_conv_s = jnp.stack([jnp.concatenate([state_ffn_conv[l][:, 1:], g[:, None, :]], axis=1)
                            for l, g in ((0, g0s), (1, g1s))])

    return (y_p.reshape(b, t, d), y_s.reshape(bs, 1, d),
            gdn_state_p[None], gdn_state_s[None],
            gdn_conv_p[None],
            jnp.concatenate([state_gdn_conv[0][:, 1:], qkv_s[:, None, :]], axis=1)[None],
            v_s.reshape(1, bs, 1, aw),
            jnp.transpose(kt_p.reshape(1, b, C_HEADS, 2, hd, t), (0, 1, 5, 2, 3, 4)),
            v_p.reshape(1, b, t, C_HEADS, hw),
            k_s.reshape(1, bs, 1, C_HEADS, 2, hd), v_sn.reshape(1, bs, 1, C_HEADS, hw),
            ffn_conv_p, ffn_conv_s)
```

```python
import functools
import math

import jax
import jax.numpy as jnp
from jax import lax
from jax.experimental import pallas as pl
from jax.experimental.pallas import tpu as pltpu

F32 = jnp.float32
BF16 = jnp.bfloat16
EPS = 1e-6

A_GROUPS = 4
A_CHUNK = 128
B_HEADS = 4
B_CONV = 4
GDN_STEP = 128
GDN_CHUNK = 64
C_HEADS = 8
FFN_CONV = 3
NEG_BIG = -1e30
LOG2E = 1.4426950408889634
POS_SPLITS = 3

VMEM_LIMIT_BYTES = 56 * 1024 * 1024
HIGHEST = lax.Precision.HIGHEST


def _cparams(sem):
    return pltpu.CompilerParams(dimension_semantics=sem, vmem_limit_bytes=VMEM_LIMIT_BYTES)


def _gelu(x):
    return 0.5 * x * (1.0 + jnp.tanh(0.7978845608028654 * (x + 0.044715 * (x * x * x))))


def _sigmoid(x):
    return 1.0 / (1.0 + jnp.exp(-x))


def _softplus(x):
    return jnp.maximum(x, 0.0) + jnp.log(1.0 + jnp.exp(-jnp.abs(x)))


def _rms(x, gain):
    return x * lax.rsqrt(jnp.mean(x * x, axis=-1, keepdims=True) + EPS) * gain


def _dot(a, b, precision=None):
    return jnp.dot(a, b, preferred_element_type=F32, precision=precision)


def _dot_nt(a, b, precision=None):
    return lax.dot_general(a, b, (((1,), (1,)), ((), ())), preferred_element_type=F32,
                           precision=precision)


def _split(a):
    hi = a.astype(BF16)
    return hi, (a - hi.astype(F32)).astype(BF16)


def _dot3(a, b):
    (ah, al), (bh, bl) = a, b
    return _dot(ah, bh) + _dot(ah, bl) + _dot(al, bh)


def _const_spec(shape):
    nd = len(shape)
    return pl.BlockSpec(shape, lambda *_: (0,) * nd, pipeline_mode=pl.Buffered(1))


def _even_in_kernel(x_ref, gpre_ref, w_ref, wba_ref, avn_ref, ws_ref, bs_ref, *rest,
                    chunk, aw, bw):
    tm = x_ref.shape[0]
    if chunk == 1:
        aout_ref, v_ref, qkv_ref, z_ref, ba_ref = rest
    else:
        aout_ref, qkv_ref, z_ref, ba_ref = rest
    xb = _rms(x_ref[...], gpre_ref[...]).astype(BF16)
    u = _gelu(_dot(xb, w_ref[:, 0:aw]))
    v = _rms(_gelu(_dot(xb, w_ref[:, aw:2 * aw])), avn_ref[...])
    qkv_ref[...] = _dot(xb, w_ref[:, 2 * aw:2 * aw + 3 * bw])
    zz = _dot(xb, w_ref[:, 2 * aw + 3 * bw:2 * aw + 4 * bw])
    z_ref[...] = (zz * _sigmoid(zz)).astype(BF16)
    ba_ref[...] = _dot(xb, wba_ref[...])
    gd = aw // A_GROUPS
    if chunk == 1:
        v_ref[...] = v
        aout_ref[...] = (u * (v * ws_ref[...] + bs_ref[...])).astype(BF16)
    else:
        row = lax.broadcasted_iota(jnp.int32, (chunk, chunk), 0)
        col = lax.broadcasted_iota(jnp.int32, (chunk, chunk), 1)
        for g in range(A_GROUPS):
            wt = jnp.where(col <= row, ws_ref[g], 0.0).astype(BF16)
            bcol = bs_ref[:, g:g + 1]
            for n in range(tm // chunk):
                rs = slice(n * chunk, (n + 1) * chunk)
                cs = slice(g * gd, (g + 1) * gd)
                mixed = _dot(wt, v[rs, cs].astype(BF16)) + bcol
                aout_ref[rs, cs] = (u[rs, cs] * mixed).astype(BF16)


def _even_in(x, gpre, w_main, w_ba, avn, ws, bs, *, chunk, tm):
    m, d = x.shape
    aw = avn.shape[-1]
    bw = (w_main.shape[1] - 2 * aw - 2 * B_HEADS) // 4
    row = lambda i: (i, 0)
    in_specs = [pl.BlockSpec((tm, d), row), _const_spec(gpre.shape), _const_spec(w_main.shape),
                _const_spec(w_ba.shape), _const_spec(avn.shape), _const_spec(ws.shape),
                _const_spec(bs.shape)]
    args = [x, gpre, w_main, w_ba, avn, ws, bs]
    tail_specs = [pl.BlockSpec((tm, 3 * bw), row), pl.BlockSpec((tm, bw), row),
                  pl.BlockSpec((tm, 128), row)]
    tail_shape = [jax.ShapeDtypeStruct((m, 3 * bw), F32), jax.ShapeDtypeStruct((m, bw), BF16),
                  jax.ShapeDtypeStruct((m, 128), F32)]
    if chunk == 1:
        out_specs = [pl.BlockSpec((tm, aw), row), pl.BlockSpec((tm, aw), row)] + tail_specs
        out_shape = ([jax.ShapeDtypeStruct((m, aw), BF16), jax.ShapeDtypeStruct((m, aw), F32)]
                     + tail_shape)
    else:
        out_specs = [pl.BlockSpec((tm, aw), row)] + tail_specs
        out_shape = [jax.ShapeDtypeStruct((m, aw), BF16)] + tail_shape
    return pl.pallas_call(
        functools.partial(_even_in_kernel, chunk=chunk, aw=aw, bw=bw),
        grid=(m // tm,),
        in_specs=in_specs, out_specs=out_specs, out_shape=out_shape,
        compiler_params=_cparams(("parallel",)),
        name="even_in",
    )(*args)


def _gdn_prompt_kernel(qkv_ref, ba_ref, z_ref, cw_ref, prm_ref, onorm_ref,
                       o_ref, st_ref,
                       ext_ref, wm_ref, u0_ref, qe_ref, qk_ref, kwt_ref, egl_ref, *, bw):
    i = pl.program_id(1)
    n = GDN_STEP
    c = GDN_CHUNK
    hd = bw // B_HEADS

    @pl.when(i == 0)
    def _():
        st_ref[...] = jnp.zeros_like(st_ref)
        ext_ref[0:8, :] = jnp.zeros((8, ext_ref.shape[1]), F32)
        ext_ref[n:n + 8, :] = jnp.zeros((8, ext_ref.shape[1]), F32)
        wm_ref[...] = jnp.zeros_like(wm_ref)
        u0_ref[...] = jnp.zeros_like(u0_ref)
        qe_ref[...] = jnp.zeros_like(qe_ref)
        qk_ref[...] = jnp.zeros_like(qk_ref)
        kwt_ref[...] = jnp.zeros_like(kwt_ref)
        egl_ref[...] = jnp.zeros_like(egl_ref)

    heads = range(B_HEADS)
    zero_half = jnp.zeros((c, hd), F32)
    egl = [(egl_ref[0:1, B_HEADS + h:B_HEADS + h + 1], egl_ref[c:c + 1, B_HEADS + h:B_HEADS + h + 1])
           for h in heads]
    state = {"s": [st_ref[0, h] for h in heads]}

    def advance_stage_u(r0):
        state["sb"] = [s.astype(BF16) for s in state["s"]]
        state["u"] = [u0_ref[h, r0:r0 + c, :] - _dot(wm_ref[h, r0:r0 + c, :], state["sb"][h])
                      for h in heads]

    def advance_stage_s(r0, part):
        halves = [[u, zero_half] if part == 0 else [zero_half, u] for u in state["u"]]
        uf = [jnp.concatenate(hv, axis=0).astype(BF16) for hv in halves]
        o = [_dot(qe_ref[h, r0:r0 + c, :], state["sb"][h]) + _dot(qk_ref[h, r0:r0 + c, :], uf[h])
             for h in heads]
        state["s"] = [egl[h][part] * state["s"][h] + _dot(kwt_ref[h], uf[h]) for h in heads]
        return o

    ext_ref[5:8, :] = ext_ref[n + 5:n + 8, :]
    cur = qkv_ref[0]
    ext_ref[8:n + 8, :] = cur
    ext = ext_ref[...]
    y = cw_ref[B_CONV - 1:B_CONV, :] * cur
    for back in range(1, B_CONV):
        y = y + cw_ref[B_CONV - 1 - back:B_CONV - back, :] * pltpu.roll(ext, back, 0)[8:n + 8, :]
    y = y * _sigmoid(y)

    ba = ba_ref[0]
    beta = _sigmoid(ba)
    gfull = -jnp.exp(prm_ref[1:2, :]) * _softplus(ba + prm_ref[0:1, :])
    row = lax.broadcasted_iota(jnp.int32, (n, n), 0)
    col = lax.broadcasted_iota(jnp.int32, (n, n), 1)
    same = (row < c) == (col < c)
    incl = same & (col <= row)
    strict = same & (col < row)
    eye = jnp.where(row == col, 1.0, 0.0)

    advance_stage_u(0)
    g_hi, g_lo = _split(gfull)
    ones_incl = jnp.where(incl, 1.0, 0.0).astype(BF16)
    ones_same = jnp.where(same, 1.0, 0.0).astype(BF16)
    gcum = _dot(ones_incl, g_hi) + _dot(ones_incl, g_lo)
    glast = _dot(ones_same, g_hi) + _dot(ones_same, g_lo)
    gcum_t = gcum.T

    qs, ks, vs, kbs, decays = [], [], [], [], []
    for h in heads:
        q = y[:, h * hd:(h + 1) * hd]
        k = y[:, bw + h * hd:bw + (h + 1) * hd]
        qs.append(q * lax.rsqrt(jnp.sum(q * q, axis=-1, keepdims=True) + EPS) * (hd ** -0.5))
        ks.append(k * lax.rsqrt(jnp.sum(k * k, axis=-1, keepdims=True) + EPS))
        vs.append(y[:, 2 * bw + h * hd:2 * bw + (h + 1) * hd])
        kbs.append(ks[h].astype(BF16))
        gcol = gcum[:, B_HEADS + h:B_HEADS + h + 1]
        grow = gcum_t[B_HEADS + h:B_HEADS + h + 1, :]
        decays.append(jnp.where(incl, jnp.exp(jnp.where(incl, gcol - grow, 0.0)), 0.0))
    bcols = [beta[:, h:h + 1] for h in heads]
    gcols = [gcum[:, B_HEADS + h:B_HEADS + h + 1] for h in heads]
    glcols = [glast[:, B_HEADS + h:B_HEADS + h + 1] for h in heads]

    kk = [_dot_nt(kbs[h], kbs[h]) for h in heads]
    o_a = advance_stage_s(0, 0)
    x = [-jnp.where(strict, bcols[h] * decays[h] * kk[h], 0.0) for h in heads]
    p = [eye + x[h] for h in heads]
    xs = [_split(x[h]) for h in heads]
    o_b = None
    for it in range(int(math.log2(c)) - 1):
        xs = [_split(_dot3(xs[h], xs[h])) for h in heads]
        if it == 0:
            advance_stage_u(c)
        p = [p[h] + _dot3(_split(p[h]), xs[h]) for h in heads]
        if it == 1:
            o_b = advance_stage_s(c, 1)
    egs = [jnp.exp(gcols[h]) for h in heads]
    ps = [_split(p[h]) for h in heads]
    wm = [_dot3(ps[h], _split(bcols[h] * egs[h] * ks[h])) for h in heads]
    u0 = [_dot3(ps[h], _split(bcols[h] * vs[h])) for h in heads]
    qk = [_dot_nt(qs[h].astype(BF16), kbs[h]) for h in heads]

    for h in heads:
        st_ref[0, h] = state["s"][h]
        o = jnp.concatenate([o_a[h], o_b[h]], axis=0)
        cs = slice(h * hd, (h + 1) * hd)
        o_ref[0, :, cs] = (_rms(o, onorm_ref[...]) * z_ref[0, :, cs].astype(F32)).astype(BF16)
        wm_ref[h] = wm[h].astype(BF16)
        u0_ref[h] = u0[h]
        qe_ref[h] = (qs[h] * egs[h]).astype(BF16)
        qk_ref[h] = (qk[h] * decays[h]).astype(BF16)
        kwt_ref[h] = (ks[h] * jnp.exp(glcols[h] - gcols[h])).T.astype(BF16)
    egl_ref[...] = jnp.exp(glast)


def _gdn_prompt(qkv, ba, z, cw, prm, onorm):
    b, t, c3 = qkv.shape
    bw = c3 // 3
    hd = bw // B_HEADS
    n = GDN_STEP
    nt = t // n
    cur = lambda bi, i: (bi, jnp.minimum(i, nt - 1), 0)
    prev = lambda bi, i: (bi, jnp.maximum(i - 1, 0), 0)
    return pl.pallas_call(
        functools.partial(_gdn_prompt_kernel, bw=bw),
        grid=(b, nt + 1),
        in_specs=[pl.BlockSpec((1, n, c3), cur), pl.BlockSpec((1, n, 128), cur),
                  pl.BlockSpec((1, n, bw), prev), _const_spec(cw.shape), _const_spec(prm.shape),
                  _const_spec(onorm.shape)],
        out_specs=[pl.BlockSpec((1, n, bw), prev),
                   pl.BlockSpec((1, B_HEADS, hd, hd), lambda bi, i: (bi, 0, 0, 0))],
        out_shape=[jax.ShapeDtypeStruct((b, t, bw), BF16),
                   jax.ShapeDtypeStruct((b, B_HEADS, hd, hd), F32)],
        scratch_shapes=[pltpu.VMEM((n + 8, c3), F32),
                        pltpu.VMEM((B_HEADS, n, hd), BF16), pltpu.VMEM((B_HEADS, n, hd), F32),
                        pltpu.VMEM((B_HEADS, n, hd), BF16), pltpu.VMEM((B_HEADS, n, n), BF16),
                        pltpu.VMEM((B_HEADS, hd, n), BF16), pltpu.VMEM((n, 128), F32)],
        compiler_params=_cparams(("arbitrary", "arbitrary")),
        name="gdn_prompt",
    )(qkv, ba, z, cw, prm, onorm)


def _gdn_step_kernel(qkv_ref, cst_ref, ba_ref, z_ref, st_ref, cw_ref, prm_ref, onorm_ref,
                     o_ref, sto_ref, *, bw):
    nb = qkv_ref.shape[0]
    hd = bw // B_HEADS
    c3 = 3 * bw
    y = (cw_ref[0:1, :] * cst_ref[:, 0:c3] + cw_ref[1:2, :] * cst_ref[:, c3:2 * c3]
         + cw_ref[2:3, :] * cst_ref[:, 2 * c3:3 * c3] + cw_ref[3:4, :] * qkv_ref[...])
    y = y * _sigmoid(y)
    ba = ba_ref[...]
    beta = _sigmoid(ba)
    eg_all = jnp.exp(-jnp.exp(prm_ref[1:2, :]) * _softplus(ba + prm_ref[0:1, :]))
    row = lax.broadcasted_iota(jnp.int32, (hd, hd), 0)
    col = lax.broadcasted_iota(jnp.int32, (hd, hd), 1)
    eye = jnp.where(row == col, 1.0, 0.0)
    for h in range(B_HEADS):
        q = y[:, h * hd:(h + 1) * hd]
        k = y[:, bw + h * hd:bw + (h + 1) * hd]
        v = y[:, 2 * bw + h * hd:2 * bw + (h + 1) * hd]
        q = q * lax.rsqrt(jnp.sum(q * q, axis=-1, keepdims=True) + EPS) * (hd ** -0.5)
        k = k * lax.rsqrt(jnp.sum(k * k, axis=-1, keepdims=True) + EPS)
        qk = jnp.sum(q * k, axis=-1, keepdims=True)
        k_t = _dot_nt(eye, k, HIGHEST)
        q_t = _dot_nt(eye, q, HIGHEST)
        o_rows = []
        for j in range(nb):
            s = st_ref[j, h]
            eg = eg_all[j:j + 1, B_HEADS + h:B_HEADS + h + 1]
            bt = beta[j:j + 1, h:h + 1]
            kc = k_t[:, j:j + 1]
            ks = jnp.sum(kc * s, axis=0, keepdims=True)
            qs = jnp.sum(q_t[:, j:j + 1] * s, axis=0, keepdims=True)
            u = bt * (v[j:j + 1, :] - eg * ks)
            o_rows.append(eg * qs + qk[j:j + 1, :] * u)
            sto_ref[j, h] = eg * s + kc * u
        o = jnp.concatenate(o_rows, axis=0)
        cs = slice(h * hd, (h + 1) * hd)
        o_ref[:, cs] = (_rms(o, onorm_ref[...]) * z_ref[:, cs].astype(F32)).astype(BF16)


def _gdn_step(qkv, cst, ba, z, state, cw, prm, onorm, *, nb):
    m, c3 = qkv.shape
    bw = c3 // 3
    hd = bw // B_HEADS
    row = lambda i: (i, 0)
    st = lambda i: (i, 0, 0, 0)
    return pl.pallas_call(
        functools.partial(_gdn_step_kernel, bw=bw),
        grid=(m // nb,),
        in_specs=[pl.BlockSpec((nb, c3), row), pl.BlockSpec((nb, 3 * c3), row),
                  pl.BlockSpec((nb, 128), row), pl.BlockSpec((nb, bw), row),
                  pl.BlockSpec((nb, B_HEADS, hd, hd), st), _const_spec(cw.shape),
                  _const_spec(prm.shape), _const_spec(onorm.shape)],
        out_specs=[pl.BlockSpec((nb, bw), row), pl.BlockSpec((nb, B_HEADS, hd, hd), st)],
        out_shape=[jax.ShapeDtypeStruct((m, bw), BF16),
                   jax.ShapeDtypeStruct((m, B_HEADS, hd, hd), F32)],
        compiler_params=_cparams(("parallel",)),
        name="gdn_step",
    )(qkv, cst, ba, z, state, cw, prm, onorm)


def _mix_out_kernel(*refs, n_in):
    ins = refs[:n_in]
    x_ref, w_ref, gpost_ref, x1_ref = refs[n_in:]
    mix = None
    r0 = 0
    for a_ref in ins:
        kk = a_ref.shape[1]
        part = _dot(a_ref[...], w_ref[r0:r0 + kk, :])
        mix = part if mix is None else mix + part
        r0 += kk
    x1_ref[...] = x_ref[...] + _rms(mix, gpost_ref[...])


def _mix_out(ins, x, w, gpost, *, tm):
    m, d = x.shape
    row = lambda i: (i, 0)
    return pl.pallas_call(
        functools.partial(_mix_out_kernel, n_in=len(ins)),
        grid=(m // tm,),
        in_specs=[pl.BlockSpec((tm, a.shape[1]), row) for a in ins]
        + [pl.BlockSpec((tm, d), row), _const_spec(w.shape), _const_spec(gpost.shape)],
        out_specs=pl.BlockSpec((tm, d), row),
        out_shape=jax.ShapeDtypeStruct((m, d), F32),
        compiler_params=_cparams(("parallel",)),
        name="mix_out",
    )(*ins, x, w, gpost)


def _ffn_kernel(x_ref, gpre_ref, wg_ref, wu_ref, wd_ref, wc_ref, bc_ref, gpost_ref, *rest,
                tiles_per_seq, fc):
    tm = x_ref.shape[0]
    ff = wg_ref.shape[1]
    if tiles_per_seq is None:
        hist_ref, x2_ref, gout_ref, a_ref = rest
    else:
        x2_ref, gout_ref, carry_ref, gbuf_ref, a_ref = rest

        @pl.when(pl.program_id(0) % tiles_per_seq == 0)
        def _():
            carry_ref[...] = jnp.zeros_like(carry_ref)

    hb = _rms(x_ref[...], gpre_ref[...]).astype(BF16)

    for c0 in range(0, ff, fc):
        cs = slice(c0, min(c0 + fc, ff))
        g = _dot(hb, wg_ref[:, cs])
        up = _dot(hb, wu_ref[:, cs])
        if tiles_per_seq is None:
            gout_ref[:, cs] = g
            sh2 = hist_ref[:, cs]
            sh1 = hist_ref[:, ff + cs.start:ff + cs.stop]
        else:
            gbuf_ref[0:8, cs] = carry_ref[:, cs]
            gbuf_ref[8:tm + 8, cs] = g
            sh2 = gbuf_ref[6:tm + 6, cs]
            sh1 = gbuf_ref[7:tm + 7, cs]
            tail = gbuf_ref[tm:tm + 8, cs]
            carry_ref[:, cs] = tail
            gout_ref[0, :, cs] = tail
        conv = (wc_ref[0:1, cs] * sh2 + wc_ref[1:2, cs] * sh1 + wc_ref[2:3, cs] * g
                + bc_ref[:, cs])
        a_ref[:, cs] = (_gelu(conv) * up).astype(BF16)
    x2_ref[...] = x_ref[...] + _rms(_dot(a_ref[...], wd_ref[...]), gpost_ref[...])


def _ffn(x1, gpre, wg, wu, wd, wc, bc, gpost, *, layer, tm, tiles_per_seq=None, hist=None, fc=256):
    m, d = x1.shape
    ff = wg.shape[2]
    row = lambda i: (i, 0)

    def layer_spec(w):
        return pl.BlockSpec((None,) + w.shape[1:], lambda i: (layer, 0, 0),
                            pipeline_mode=pl.Buffered(1))

    in_specs = [pl.BlockSpec((tm, d), row), _const_spec(gpre.shape), layer_spec(wg),
                layer_spec(wu), layer_spec(wd), _const_spec(wc.shape),
                _const_spec(bc.shape), _const_spec(gpost.shape)]
    args = [x1, gpre, wg, wu, wd, wc, bc, gpost]
    if tiles_per_seq is None:
        in_specs.append(pl.BlockSpec((tm, 2 * ff), row))
        args.append(hist)
        out_specs = [pl.BlockSpec((tm, d), row), pl.BlockSpec((tm, ff), row)]
        out_shape = [jax.ShapeDtypeStruct((m, d), F32), jax.ShapeDtypeStruct((m, ff), F32)]
        scratch = []
    else:
        out_specs = [pl.BlockSpec((tm, d), row), pl.BlockSpec((1, 8, ff), lambda i: (i, 0, 0))]
        out_shape = [jax.ShapeDtypeStruct((m, d), F32),
                     jax.ShapeDtypeStruct((m // tm, 8, ff), F32)]
        scratch = [pltpu.VMEM((8, ff), F32), pltpu.VMEM((tm + 8, ff), F32)]
    scratch.append(pltpu.VMEM((tm, ff), BF16))
    return pl.pallas_call(
        functools.partial(_ffn_kernel, tiles_per_seq=tiles_per_seq, fc=fc),
        grid=(m // tm,),
        in_specs=in_specs, out_specs=out_specs, out_shape=out_shape, scratch_shapes=scratch,
        compiler_params=_cparams(("arbitrary",)),
        name="conv_ffn",
    )(*args)


def _odd_in_kernel(x_ref, gpre_ref, w_ref, *outs, head_major, qscale):
    d = x_ref.shape[1]
    xb = _rms(x_ref[...], gpre_ref[...]).astype(BF16)
    q = _dot(xb, w_ref[:, 0:d]) * qscale
    k = _dot(xb, w_ref[:, d:2 * d])
    v = _dot(xb, w_ref[:, 2 * d:3 * d])
    if head_major:
        qh_ref, kh_ref, vt_ref, kt_ref, v_ref = outs
        tm = x_ref.shape[0]
        kt_ref[0] = k.T
        v_ref[...] = v
        vt = v.T
        hw = d // C_HEADS
        rowi = lax.broadcasted_iota(jnp.int32, (tm, hw), 0)
        lane = lax.broadcasted_iota(jnp.int32, (tm, hw), 1)
        pos = jnp.where((lane & 1) == 0, lax.shift_right_logical(rowi, 4), rowi & 15)
        feat = jnp.where(lane < 2 * POS_SPLITS, pos, 0).astype(F32).astype(BF16)
        ones_pad = jnp.where(lax.broadcasted_iota(jnp.int32, (16, tm), 0) == 0, 1.0, 0.0)
        for h in range(C_HEADS):
            cs = slice(h * hw, (h + 1) * hw)
            qh_ref[0, h] = q[:, cs].astype(BF16)
            kh_ref[0, h, :, 0:hw] = k[:, cs].astype(BF16)
            kh_ref[0, h, :, hw:2 * hw] = feat
            vt_ref[0, h, 0] = jnp.concatenate([vt[cs, :], ones_pad], axis=0).astype(BF16)
    else:
        q_ref, k_ref, v_ref = outs
        q_ref[...] = q
        k_ref[...] = k
        v_ref[...] = v


def _odd_in(x, gpre, w, *, tm, seq_len=None):
    m, d = x.shape
    qscale = (d // (2 * C_HEADS)) ** -0.5
    row = lambda i: (i, 0)
    head_major = seq_len is not None
    if head_major:
        assert tm <= 512
        qscale *= LOG2E
        tps = seq_len // tm
        hw = d // C_HEADS
        nb = m // seq_len
        hm = lambda i: (i // tps, 0, i % tps, 0)
        out_specs = [pl.BlockSpec((1, C_HEADS, tm, hw), hm),
                     pl.BlockSpec((1, C_HEADS, tm, 2 * hw), hm),
                     pl.BlockSpec((1, C_HEADS, 1, hw + 16, tm),
                                  lambda i: (i // tps, 0, i % tps, 0, 0)),
                     pl.BlockSpec((1, d, tm), lambda i: (i // tps, 0, i % tps)),
                     pl.BlockSpec((tm, d), row)]
        out_shape = [jax.ShapeDtypeStruct((nb, C_HEADS, seq_len, hw), BF16),
                     jax.ShapeDtypeStruct((nb, C_HEADS, seq_len, 2 * hw), BF16),
                     jax.ShapeDtypeStruct((nb, C_HEADS, tps, hw + 16, tm), BF16),
                     jax.ShapeDtypeStruct((nb, d, seq_len), F32),
                     jax.ShapeDtypeStruct((m, d), F32)]
    else:
        out_specs = [pl.BlockSpec((tm, d), row)] * 3
        out_shape = [jax.ShapeDtypeStruct((m, d), F32)] * 3
    return pl.pallas_call(
        functools.partial(_odd_in_kernel, head_major=head_major, qscale=qscale),
        grid=(m // tm,),
        in_specs=[pl.BlockSpec((tm, d), row), _const_spec(gpre.shape), _const_spec(w.shape)],
        out_specs=out_specs, out_shape=out_shape,
        compiler_params=_cparams(("parallel",)),
        name="odd_in",
    )(x, gpre, w)


def _lambda_value(lam_ref, lam_init):
    l = lam_ref[...]
    a = jnp.sum(l[0:1, :] * l[1:2, :], axis=-1, keepdims=True)
    b = jnp.sum(l[2:3, :] * l[3:4, :], axis=-1, keepdims=True)
    return jnp.exp(a) - jnp.exp(b) + lam_init


def _attn_prompt_kernel(slope2_ref, q_ref, k_ref, vt_ref, lam_ref, subln_ref, o_ref,
                        qs_ref, qn_ref, s_ref, m_ref, acc_ref, *, tq, tk, qb, lam_init,
                        side_work=None):
    h = pl.program_id(1)
    iq = pl.program_id(2)
    nq = pl.num_programs(2)
    hw = q_ref.shape[3]
    hd = hw // 2
    r = 2 * tq
    slope2 = slope2_ref[h]

    lane1 = lax.broadcasted_iota(jnp.int32, (1, hw), 1)
    rest = jnp.full((1, hw), slope2, F32)
    feat = jnp.zeros((1, hw), F32)
    for i in range(POS_SPLITS):
        piece = rest.astype(BF16).astype(F32)
        rest = rest - piece
        feat = jnp.where(lane1 == 2 * i, 16.0 * piece, jnp.where(lane1 == 2 * i + 1, piece, feat))
    feat = jnp.broadcast_to(feat, (r, hw)).astype(BF16)
    lane = lax.broadcasted_iota(jnp.int32, (tq, hw), 1)

    def stack_queries(dst_ref, tile):
        q = q_ref[0, 0, pl.ds(pl.multiple_of(tile * tq, tq), tq), :]
        dst_ref[0:tq, 0:hw] = jnp.where(lane < hd, q, jnp.zeros_like(q))
        dst_ref[tq:r, 0:hw] = jnp.where(lane >= hd, q, jnp.zeros_like(q))
        dst_ref[:, hw:2 * hw] = feat

    stack_queries(qs_ref, iq)
    stack_queries(qn_ref, jnp.minimum(iq + 1, nq - 1))
    m_ref[...] = jnp.full_like(m_ref, NEG_BIG)
    acc_ref[...] = jnp.zeros_like(acc_ref)

    def keys(kc):
        return k_ref[0, 0, pl.ds(pl.multiple_of(kc * tk, tk), tk), :]

    def step(kc, kb_next, qsrc_ref, masked, between=None):
        vtb = vt_ref[0, 0, kc]
        cshift = slope2 * (kc * tk).astype(F32)
        for c0 in range(0, r, qb):
            if between is not None and c0 == r // 2:
                between()
            cols = slice(c0, c0 + qb)
            kn = min(tk, (c0 % tq) + qb) if masked else tk
            s = s_ref[0:kn, cols]
            s_ref[:, cols] = _dot_nt(kb_next, qsrc_ref[cols, :])
            if masked:
                key_i = lax.broadcasted_iota(jnp.int32, (kn, qb), 0)
                qry_i = lax.broadcasted_iota(jnp.int32, (kn, qb), 1)
                s = jnp.where(key_i <= (c0 % tq) + qry_i, s, NEG_BIG)
            m_old = m_ref[:, cols]
            m_new = jnp.maximum(m_old, jnp.max(s, axis=0, keepdims=True) + cshift)
            p = jnp.exp2(s - (m_new - cshift)).astype(BF16)
            alpha = jnp.exp2(m_old - m_new)
            acc_ref[:, cols] = alpha * acc_ref[:, cols] + _dot(vtb[:, 0:kn], p)
            m_ref[:, cols] = m_new

    @pl.when(iq == 0)
    def _():
        s_ref[...] = _dot_nt(keys(0), qs_ref[...])

    done = 0
    for width in (8, 4, 2, 1):
        trips = lax.shift_right_logical(iq - done, width.bit_length() - 1)

        def body(j, carry, first=done, width=width):
            for u in range(width):
                kc = first + width * j + u
                step(kc, keys(kc + 1), qs_ref, False)
            return carry

        lax.fori_loop(0, trips, body, 0)
        done = done + width * trips
    before, between, after = side_work or (None, None, None)
    if before is not None:
        before()
    step(iq, keys(0), qn_ref, True, between)
    if after is not None:
        after()

    lam = _lambda_value(lam_ref, lam_init)
    o1 = acc_ref[0:hw, 0:tq] / acc_ref[hw:hw + 1, 0:tq]
    o2 = acc_ref[0:hw, tq:r] / acc_ref[hw:hw + 1, tq:r]
    att = (o1 - lam * o2).T
    o_ref[0] = (_rms(att, subln_ref[...]) * (1.0 - lam_init)).astype(BF16)


def _attn_prompt(slopes, qh, kh, vt, lam_p, subln, *, tq, lam_init):
    b, nh, t, hw = qh.shape
    tk = vt.shape[4]
    assert tq == tk
    return pl.pallas_call(
        functools.partial(_attn_prompt_kernel, tq=tq, tk=tk, qb=min(256, tq), lam_init=lam_init),
        grid=(b, nh, t // tq),
        in_specs=[pl.BlockSpec(memory_space=pltpu.SMEM),
                  pl.BlockSpec((1, 1, t, hw), lambda bi, h, iq: (bi, h, 0, 0)),
                  pl.BlockSpec((1, 1, t, 2 * hw), lambda bi, h, iq: (bi, h, 0, 0)),
                  pl.BlockSpec((1, 1, t // tk, hw + 16, tk), lambda bi, h, iq: (bi, h, 0, 0, 0)),
                  _const_spec(lam_p.shape), _const_spec(subln.shape)],
        out_specs=pl.BlockSpec((1, tq, hw), lambda bi, h, iq: (bi, iq, h)),
        out_shape=jax.ShapeDtypeStruct((b, t, nh * hw), BF16),
        scratch_shapes=[pltpu.VMEM((2 * tq, 2 * hw), BF16), pltpu.VMEM((2 * tq, 2 * hw), BF16),
                        pltpu.VMEM((tk, 2 * tq), F32), pltpu.VMEM((1, 2 * tq), F32),
                        pltpu.VMEM((hw + 16, 2 * tq), F32)],
        compiler_params=_cparams(("arbitrary", "arbitrary", "arbitrary")),
        name="attn_prompt",
    )(slopes * LOG2E, qh, kh, vt, lam_p, subln)


def _decode_stages(q_ref, kn_ref, vn_ref, slope_ref, expand_ref, lam_ref, subln_ref, k_refs,
                   v_refs, o_ref, m_ref, l_ref, acc_ref, *, j, live, n_past, lam_init):
    pg = len(k_refs)
    _, d, page = k_refs[0].shape
    nh = C_HEADS
    nr = 2 * nh
    hd = d // nr
    st = {}

    def scores():
        slope = slope_ref[:, 0:1]
        rowi = lax.broadcasted_iota(jnp.int32, (nr, d), 0)
        coli = lax.broadcasted_iota(jnp.int32, (nr, d), 1)
        lo = (jnp.where(rowi >= nh, rowi - nh, rowi) * 2 + jnp.where(rowi >= nh, 1, 0)) * hd
        qbd = jnp.where((coli >= lo) & (coli < lo + hd), q_ref[0], 0.0)
        qb = qbd.astype(BF16)
        tpos = lax.broadcasted_iota(jnp.int32, (1, page), 1)
        s_parts = []
        for p_i in range(pg):
            dist = (n_past - ((j * pg + p_i) * page + tpos)).astype(F32)
            s_parts.append(_dot(qb, k_refs[p_i][0].astype(BF16)) - slope * dist)
        s = jnp.where(live, jnp.concatenate(s_parts, axis=1), NEG_BIG)
        first = jnp.logical_and(j == 0, live)
        m_old = jnp.where(first, NEG_BIG, m_ref[...])
        m_new = jnp.maximum(m_old, jnp.max(s, axis=-1, keepdims=True))
        p = jnp.exp(s - m_new).astype(BF16)
        alpha = jnp.exp(m_old - m_new)
        l = alpha * jnp.where(first, 0.0, l_ref[...]) + jnp.sum(p.astype(F32), axis=-1,
                                                                  keepdims=True)
        st.update(qbd=qbd, p=p, alpha=alpha, first=first, m=m_new, l=l)
        m_ref[...] = m_new
        l_ref[...] = l

    def spread():
        rowe = lax.broadcasted_iota(jnp.int32, (nr, nh * page), 0)
        cole = lax.broadcasted_iota(jnp.int32, (nr, nh * page), 1)
        own_head = (cole & (nh - 1)) == jnp.where(rowe >= nh, rowe - nh, rowe)
        p = st["p"]
        stacked = jnp.concatenate([p[:, p_i * page:(p_i + 1) * page] for p_i in range(pg)], axis=0)
        wide = _dot(stacked, expand_ref[...])
        st["pbig"] = [jnp.where(own_head, wide[p_i * nr:(p_i + 1) * nr, :], 0.0).astype(BF16)
                      for p_i in range(pg)]

    def values():
        pv = [_dot(st["pbig"][p_i], v_refs[p_i][0].astype(BF16)) for p_i in range(pg)]
        acc = st["alpha"] * jnp.where(st["first"], 0.0, acc_ref[...]) + sum(pv[1:], pv[0])
        acc_ref[...] = acc
        s = jnp.sum(st["qbd"] * kn_ref[0], axis=-1, keepdims=True)
        m_new = jnp.maximum(st["m"], s)
        p = jnp.exp(s - m_new)
        alpha = jnp.exp(st["m"] - m_new)
        l = alpha * st["l"] + p
        vn = vn_ref[0]
        acc = alpha * acc + p * jnp.concatenate([vn, vn], axis=0)
        lam = _lambda_value(lam_ref, lam_init)
        att = acc[0:nh, :] / l[0:nh, :] - lam * (acc[nh:nr, :] / l[nh:nr, :])
        o_ref[0] = (_rms(att, subln_ref[...]) * (1.0 - lam_init)).astype(BF16)

    return scores, spread, values


def _decode_scratch_init(m_ref, l_ref, acc_ref):
    m_ref[...] = jnp.full_like(m_ref, NEG_BIG)
    l_ref[...] = jnp.zeros_like(l_ref)
    acc_ref[...] = jnp.zeros_like(acc_ref)


def _attn_decode_kernel(pt_ref, q_ref, kn_ref, vn_ref, slope_ref, expand_ref, lam_ref, subln_ref,
                        *rest, pg, n_past, lam_init):
    k_refs = rest[0:pg]
    v_refs = rest[pg:2 * pg]
    o_ref, m_ref, l_ref, acc_ref = rest[2 * pg:]

    @pl.when((pl.program_id(0) == 0) & (pl.program_id(1) == 0))
    def _():
        _decode_scratch_init(m_ref, l_ref, acc_ref)

    for stage in _decode_stages(q_ref, kn_ref, vn_ref, slope_ref, expand_ref, lam_ref, subln_ref,
                                k_refs, v_refs, o_ref, m_ref, l_ref, acc_ref,
                                j=pl.program_id(1), live=True, n_past=n_past, lam_init=lam_init):
        stage()


def _attn_fused_kernel(pt_ref, slope2_ref, q_ref, k_ref, vt_ref, lam_ref, subln_ref,
                       dq_ref, dkn_ref, dvn_ref, dslope_ref, expand_ref, *rest,
                       pg, steps_per_seq, n_dec, n_past, lam_init, **attn_kw):
    k_refs = rest[0:pg]
    v_refs = rest[pg:2 * pg]
    o_ref, do_ref = rest[2 * pg:2 * pg + 2]
    attn_scratch = rest[2 * pg + 2:-3]
    dm_ref, dl_ref, dacc_ref = rest[-3:]
    g = (pl.program_id(0) * pl.num_programs(1) + pl.program_id(1)) * pl.num_programs(2) \
        + pl.program_id(2)

    @pl.when(g == 0)
    def _():
        _decode_scratch_init(dm_ref, dl_ref, dacc_ref)

    side_work = _decode_stages(dq_ref, dkn_ref, dvn_ref, dslope_ref, expand_ref, lam_ref,
                               subln_ref, k_refs, v_refs, do_ref, dm_ref, dl_ref, dacc_ref,
                               j=jnp.minimum(g, n_dec - 1) % steps_per_seq, live=g < n_dec,
                               n_past=n_past, lam_init=lam_init)
    _attn_prompt_kernel(slope2_ref, q_ref, k_ref, vt_ref, lam_ref, subln_ref, o_ref,
                        *attn_scratch, lam_init=lam_init, side_work=side_work, **attn_kw)


def _attn_decode(page_table, q, kn, vn, slope_tile, lam_p, subln, cache_kt, cache_v2, *, pg,
                 lam_init):
    bs, n_pages = page_table.shape
    _, d, page = cache_kt.shape
    hw = d // C_HEADS
    q3, kn3 = (a.reshape(bs, 1, d) for a in (q, kn))
    vn3 = vn.reshape(bs, C_HEADS, hw)
    expand = (jnp.arange(page * C_HEADS)[None, :] // C_HEADS
              == jnp.arange(page)[:, None]).astype(BF16)
    row = lambda b, j, pt: (b, 0, 0)
    const2 = lambda b, j, pt: (0, 0)

    def page_map(p_i):
        return lambda b, j, pt: (pt[b, j * pg + p_i], 0, 0)

    k_specs = [pl.BlockSpec((1, d, page), page_map(p_i)) for p_i in range(pg)]
    v_specs = [pl.BlockSpec((1, page * C_HEADS, hw), page_map(p_i)) for p_i in range(pg)]
    grid_spec = pltpu.PrefetchScalarGridSpec(
        num_scalar_prefetch=1,
        grid=(bs, n_pages // pg),
        in_specs=[pl.BlockSpec((1, 1, d), row), pl.BlockSpec((1, 1, d), row),
                  pl.BlockSpec((1, C_HEADS, hw), row), pl.BlockSpec(slope_tile.shape, const2),
                  pl.BlockSpec(expand.shape, const2), pl.BlockSpec(lam_p.shape, const2),
                  pl.BlockSpec(subln.shape, const2)] + k_specs + v_specs,
        out_specs=pl.BlockSpec((1, C_HEADS, hw), row),
        scratch_shapes=[pltpu.VMEM((2 * C_HEADS, 1), F32), pltpu.VMEM((2 * C_HEADS, 1), F32),
                        pltpu.VMEM((2 * C_HEADS, hw), F32)],
    )
    out = pl.pallas_call(
        functools.partial(_attn_decode_kernel, pg=pg, n_past=n_pages * page, lam_init=lam_init),
        grid_spec=grid_spec,
        out_shape=jax.ShapeDtypeStruct((bs, C_HEADS, hw), BF16),
        compiler_params=_cparams(("arbitrary", "arbitrary")),
        name="attn_decode",
    )(page_table, q3, kn3, vn3, slope_tile, expand, lam_p, subln, *([cache_kt] * pg),
      *([cache_v2] * pg))
    return out.reshape(bs, d)


def _attn_fused(slopes, qh, kh, vt, lam_p, subln, page_table, q, kn, vn, slope_tile, cache_kt,
                cache_v2, *, tq, pg, lam_init):
    b, nh, t, hw = qh.shape
    tk = vt.shape[4]
    assert tq == tk
    nq = t // tq
    bs, n_pages = page_table.shape
    _, d, page = cache_kt.shape
    spp = n_pages // pg
    n_dec = bs * spp
    assert n_dec <= b * nh * nq
    q3, kn3 = (a.reshape(bs, 1, d) for a in (q, kn))
    vn3 = vn.reshape(bs, C_HEADS, hw)
    expand = (jnp.arange(page * C_HEADS)[None, :] // C_HEADS
              == jnp.arange(page)[:, None]).astype(BF16)

    def dec(bi, h, iq):
        ds = jnp.minimum((bi * nh + h) * nq + iq, n_dec - 1)
        return ds // spp, ds % spp

    seq_row = lambda bi, h, iq, pt: (dec(bi, h, iq)[0], 0, 0)
    const2 = lambda bi, h, iq, pt: (0, 0)
    head = lambda bi, h, iq, pt: (bi, h, 0, 0)

    def page_map(p_i):
        def index(bi, h, iq, pt):
            sb, j = dec(bi, h, iq)
            return pt[sb, j * pg + p_i], 0, 0
        return index

    k_specs = [pl.BlockSpec((1, d, page), page_map(p_i)) for p_i in range(pg)]
    v_specs = [pl.BlockSpec((1, page * C_HEADS, hw), page_map(p_i)) for p_i in range(pg)]
    grid_spec = pltpu.PrefetchScalarGridSpec(
        num_scalar_prefetch=1,
        grid=(b, nh, nq),
        in_specs=[pl.BlockSpec(memory_space=pltpu.SMEM),
                  pl.BlockSpec((1, 1, t, hw), head), pl.BlockSpec((1, 1, t, 2 * hw), head),
                  pl.BlockSpec((1, 1, t // tk, hw + 16, tk), lambda bi, h, iq, pt: (bi, h, 0, 0, 0)),
                  pl.BlockSpec(lam_p.shape, const2), pl.BlockSpec(subln.shape, const2),
                  pl.BlockSpec((1, 1, d), seq_row), pl.BlockSpec((1, 1, d), seq_row),
                  pl.BlockSpec((1, C_HEADS, hw), seq_row), pl.BlockSpec(slope_tile.shape, const2),
                  pl.BlockSpec(expand.shape, const2)] + k_specs + v_specs,
        out_specs=[pl.BlockSpec((1, tq, hw), lambda bi, h, iq, pt: (bi, iq, h)),
                   pl.BlockSpec((1, C_HEADS, hw), seq_row)],
        scratch_shapes=[pltpu.VMEM((2 * tq, 2 * hw), BF16), pltpu.VMEM((2 * tq, 2 * hw), BF16),
                        pltpu.VMEM((tk, 2 * tq), F32), pltpu.VMEM((1, 2 * tq), F32),
                        pltpu.VMEM((hw + 16, 2 * tq), F32),
                        pltpu.VMEM((2 * C_HEADS, 1), F32), pltpu.VMEM((2 * C_HEADS, 1), F32),
                        pltpu.VMEM((2 * C_HEADS, hw), F32)],
    )
    att, att_s = pl.pallas_call(
        functools.partial(_attn_fused_kernel, pg=pg, steps_per_seq=spp, n_dec=n_dec,
                          n_past=n_pages * page, lam_init=lam_init, tq=tq, tk=tk,
                          qb=min(256, tq)),
        grid_spec=grid_spec,
        out_shape=[jax.ShapeDtypeStruct((b, t, nh * hw), BF16),
                   jax.ShapeDtypeStruct((bs, C_HEADS, hw), BF16)],
        compiler_params=_cparams(("arbitrary", "arbitrary", "arbitrary")),
        name="attn_fused",
    )(page_table, slopes * LOG2E, qh, kh, vt, lam_p, subln, q3, kn3, vn3, slope_tile, expand,
      *([cache_kt] * pg), *([cache_v2] * pg))
    return att, att_s.reshape(bs, d)


def _row_tile(m, pref):
    tm = min(pref, m)
    assert m % tm == 0, (m, tm)
    return tm


def kernel(x_prompt, x_sample, cache_k, cache_v, page_table, state_gdn, state_gdn_conv, state_ffn_conv, norm_mix_pre, norm_mix_post, norm_ffn_pre, norm_ffn_post, w_in_even, a_v_norm, a_w_s, a_b_s, b_conv_w, b_a_log, b_dt_bias, b_out_norm, w_out_even, w_in_odd, c_lambda, c_subln, w_out_odd, w_ffn_gate, w_ffn_up, w_ffn_conv, b_ffn_conv, w_ffn_down):
    b, t, d = x_prompt.shape
    bs = x_sample.shape[0]
    assert x_sample.shape[1] == 1 and t % GDN_STEP == 0
    aw = a_v_norm.shape[-1]
    bw = b_conv_w.shape[-1] // 3
    ff = w_ffn_gate.shape[-1]
    n_split = 2 * aw + 4 * bw
    hw = d // C_HEADS
    _, n_pool, page, _, _, hd = cache_k.shape
    row2 = lambda a: a.reshape(1, -1)

    w_even = w_in_even[0].astype(BF16)
    w_ba = jnp.pad(w_in_even[0, :, n_split:], ((0, 0), (0, 128 - 2 * B_HEADS))).astype(BF16)
    prm = jnp.zeros((8, 128), F32)
    prm = prm.at[0, B_HEADS:2 * B_HEADS].set(b_dt_bias[0]).at[1, B_HEADS:2 * B_HEADS].set(b_a_log[0])
    onorm = row2(b_out_norm[0])
    w_oe = w_out_even[0].astype(BF16)
    w_odd = w_in_odd[0].astype(BF16)
    w_oo = w_out_odd[0].astype(BF16)
    wg = w_ffn_gate.astype(BF16)
    wu = w_ffn_up.astype(BF16)
    wd = w_ffn_down.astype(BF16)
    slopes = jnp.exp2(-8.0 * jnp.arange(1, C_HEADS + 1, dtype=F32) / C_HEADS)
    slope_tile = jnp.broadcast_to(jnp.tile(slopes, 2)[:, None], (2 * C_HEADS, 128))
    lam_init = 0.8 - 0.6 * math.exp(-0.3 * 1)
    ws_step = row2(jnp.repeat(a_w_s[0, :, 0, 0], aw // A_GROUPS))
    bs_step = row2(jnp.repeat(a_b_s[0, :, 0], aw // A_GROUPS))
    bs_t = a_b_s[0].T

    def ffn_layer(layer, x1, **kw):
        return _ffn(x1, row2(norm_ffn_pre[layer]), wg, wu, wd, w_ffn_conv[layer],
                    row2(b_ffn_conv[layer]), row2(norm_ffn_post[layer]), layer=layer, **kw)

    tm = _row_tile(b * t, 512)
    tps = t // tm
    xp = x_prompt.reshape(b * t, d)
    a_out, qkv, z, ba = _even_in(xp, row2(norm_mix_pre[0]), w_even, w_ba, row2(a_v_norm[0]),
                                 a_w_s[0], bs_t, chunk=A_CHUNK, tm=tm)
    qkv3 = qkv.reshape(b, t, 3 * bw)
    o, gdn_state_p = _gdn_prompt(qkv3, ba.reshape(b, t, 128), z.reshape(b, t, bw), b_conv_w[0],
                                 prm, onorm)
    gdn_conv_p = qkv3[:, t - (B_CONV - 1):, :]
    x1 = _mix_out([a_out, o.reshape(b * t, bw)], xp, w_oe, row2(norm_mix_post[0]), tm=tm)
    x2, gt0 = ffn_layer(0, x1, tm=tm, tiles_per_seq=tps)
    qh, kh, vt, kt_p, v_p = _odd_in(x2, row2(norm_mix_pre[1]), w_odd, tm=tm, seq_len=t)

    xs = x_sample.reshape(bs, d)
    a_out_s, v_s, qkv_s, z_s, ba_s = _even_in(xs, row2(norm_mix_pre[0]), w_even, w_ba,
                                              row2(a_v_norm[0]), ws_step, bs_step, chunk=1, tm=bs)
    o_s, gdn_state_s = _gdn_step(qkv_s, state_gdn_conv[0].reshape(bs, -1), ba_s, z_s, state_gdn[0],
                                 b_conv_w[0], prm, onorm, nb=min(16, bs))
    x1s = _mix_out([a_out_s, o_s], xs, w_oe, row2(norm_mix_post[0]), tm=bs)
    x2s, g0s = ffn_layer(0, x1s, tm=bs, hist=state_ffn_conv[0].reshape(bs, -1))
    q_s, k_s, v_sn = _odd_in(x2s, row2(norm_mix_pre[1]), w_odd, tm=bs)

    n_pages = page_table.shape[1]
    cache_kt = jnp.transpose(cache_k[0], (0, 2, 3, 4, 1)).reshape(n_pool, d, page)
    cache_v2 = cache_v[0].reshape(n_pool, page * C_HEADS, hw)
    lam_p, subln = c_lambda[0], row2(c_subln[0])
    ride = [c for c in (1, 2, 4, 8) if n_pages % c == 0
            and bs * (n_pages // c) <= b * C_HEADS * (t // tm)]
    if ride:
        att, att_s = _attn_fused(slopes, qh, kh, vt, lam_p, subln, page_table, q_s, k_s, v_sn,
                                 slope_tile, cache_kt, cache_v2, tq=tm, pg=ride[0],
                                 lam_init=lam_init)
    else:
        att = _attn_prompt(slopes, qh, kh, vt, lam_p, subln, tq=tm, lam_init=lam_init)
        pg = next(c for c in (16, 8, 4, 2, 1) if n_pages % c == 0)
        att_s = _attn_decode(page_table, q_s, k_s, v_sn, slope_tile, lam_p, subln, cache_kt,
                             cache_v2, pg=pg, lam_init=lam_init)

    x3 = _mix_out([att.reshape(b * t, d)], x2, w_oo, row2(norm_mix_post[1]), tm=tm)
    y_p, gt1 = ffn_layer(1, x3, tm=tm, tiles_per_seq=tps)
    ffn_conv_p = jnp.stack([g.reshape(b, tps, 8, ff)[:, -1, 8 - (FFN_CONV - 1):, :]
                            for g in (gt0, gt1)])
    x3s = _mix_out([att_s], x2s, w_oo, row2(norm_mix_post[1]), tm=bs)
    y_s, g1s = ffn_layer(1, x3s, tm=bs, hist=state_ffn_conv[1].reshape(bs, -1))
    ffn_conv_s = jnp.stack([jnp.concatenate([state_ffn_conv[l][:, 1:], g[:, None, :]], axis=1)
                            for l, g in ((0, g0s), (1, g1s))])

    return (y_p.reshape(b, t, d), y_s.reshape(bs, 1, d),
            gdn_state_p[None], gdn_state_s[None],
            gdn_conv_p[None],
            jnp.concatenate([state_gdn_conv[0][:, 1:], qkv_s[:, None, :]], axis=1)[None],
            v_s.reshape(1, bs, 1, aw),
            jnp.transpose(kt_p.reshape(1, b, C_HEADS, 2, hd, t), (0, 1, 5, 2, 3, 4)),
            v_p.reshape(1, b, t, C_HEADS, hw),
            k_s.reshape(1, bs, 1, C_HEADS, 2, hd), v_sn.reshape(1, bs, 1, C_HEADS, hw),
            ffn_conv_p, ffn_conv_s)
```

```python
import functools
import math

import jax
import jax.numpy as jnp
from jax import lax
from jax.experimental import pallas as pl
from jax.experimental.pallas import tpu as pltpu

F32 = jnp.float32
BF16 = jnp.bfloat16
EPS = 1e-6

A_GROUPS = 4
A_CHUNK = 128
B_HEADS = 4
B_CONV = 4
GDN_STEP = 128
GDN_CHUNK = 64
C_HEADS = 8
FFN_CONV = 3
NEG_BIG = -1e30
LOG2E = 1.4426950408889634
POS_SPLITS = 3

VMEM_LIMIT_BYTES = 56 * 1024 * 1024
HIGHEST = lax.Precision.HIGHEST


def _cparams(sem):
    return pltpu.CompilerParams(dimension_semantics=sem, vmem_limit_bytes=VMEM_LIMIT_BYTES)


def _gelu(x):
    return 0.5 * x * (1.0 + jnp.tanh(0.7978845608028654 * (x + 0.044715 * (x * x * x))))


def _sigmoid(x):
    return 1.0 / (1.0 + jnp.exp(-x))


def _softplus(x):
    return jnp.maximum(x, 0.0) + jnp.log(1.0 + jnp.exp(-jnp.abs(x)))


def _rms(x, gain):
    return x * lax.rsqrt(jnp.mean(x * x, axis=-1, keepdims=True) + EPS) * gain


def _dot(a, b, precision=None):
    return jnp.dot(a, b, preferred_element_type=F32, precision=precision)


def _dot_nt(a, b, precision=None):
    return lax.dot_general(a, b, (((1,), (1,)), ((), ())), preferred_element_type=F32,
                           precision=precision)


def _split(a):
    hi = a.astype(BF16)
    return hi, (a - hi.astype(F32)).astype(BF16)


def _dot3(a, b):
    (ah, al), (bh, bl) = a, b
    return _dot(ah, bh) + _dot(ah, bl) + _dot(al, bh)


def _const_spec(shape):
    nd = len(shape)
    return pl.BlockSpec(shape, lambda *_: (0,) * nd, pipeline_mode=pl.Buffered(1))


def _even_in_kernel(x_ref, gpre_ref, w_ref, wba_ref, avn_ref, ws_ref, bs_ref, *rest,
                    chunk, aw, bw):
    tm = x_ref.shape[0]
    if chunk == 1:
        aout_ref, v_ref, qkv_ref, z_ref, ba_ref = rest
    else:
        aout_ref, qkv_ref, z_ref, ba_ref = rest
    xb = _rms(x_ref[...], gpre_ref[...]).astype(BF16)
    u = _gelu(_dot(xb, w_ref[:, 0:aw]))
    v = _rms(_gelu(_dot(xb, w_ref[:, aw:2 * aw])), avn_ref[...])
    qkv_ref[...] = _dot(xb, w_ref[:, 2 * aw:2 * aw + 3 * bw])
    zz = _dot(xb, w_ref[:, 2 * aw + 3 * bw:2 * aw + 4 * bw])
    z_ref[...] = (zz * _sigmoid(zz)).astype(BF16)
    ba_ref[...] = _dot(xb, wba_ref[...])
    gd = aw // A_GROUPS
    if chunk == 1:
        v_ref[...] = v
        aout_ref[...] = (u * (v * ws_ref[...] + bs_ref[...])).astype(BF16)
    else:
        row = lax.broadcasted_iota(jnp.int32, (chunk, chunk), 0)
        col = lax.broadcasted_iota(jnp.int32, (chunk, chunk), 1)
        for g in range(A_GROUPS):
            wt = jnp.where(col <= row, ws_ref[g], 0.0).astype(BF16)
            bcol = bs_ref[:, g:g + 1]
            for n in range(tm // chunk):
                rs = slice(n * chunk, (n + 1) * chunk)
                cs = slice(g * gd, (g + 1) * gd)
                mixed = _dot(wt, v[rs, cs].astype(BF16)) + bcol
                aout_ref[rs, cs] = (u[rs, cs] * mixed).astype(BF16)


def _even_in(x, gpre, w_main, w_ba, avn, ws, bs, *, chunk, tm):
    m, d = x.shape
    aw = avn.shape[-1]
    bw = (w_main.shape[1] - 2 * aw - 2 * B_HEADS) // 4
    row = lambda i: (i, 0)
    in_specs = [pl.BlockSpec((tm, d), row), _const_spec(gpre.shape), _const_spec(w_main.shape),
                _const_spec(w_ba.shape), _const_spec(avn.shape), _const_spec(ws.shape),
                _const_spec(bs.shape)]
    args = [x, gpre, w_main, w_ba, avn, ws, bs]
    tail_specs = [pl.BlockSpec((tm, 3 * bw), row), pl.BlockSpec((tm, bw), row),
                  pl.BlockSpec((tm, 128), row)]
    tail_shape = [jax.ShapeDtypeStruct((m, 3 * bw), F32), jax.ShapeDtypeStruct((m, bw), BF16),
                  jax.ShapeDtypeStruct((m, 128), F32)]
    if chunk == 1:
        out_specs = [pl.BlockSpec((tm, aw), row), pl.BlockSpec((tm, aw), row)] + tail_specs
        out_shape = ([jax.ShapeDtypeStruct((m, aw), BF16), jax.ShapeDtypeStruct((m, aw), F32)]
                     + tail_shape)
    else:
        out_specs = [pl.BlockSpec((tm, aw), row)] + tail_specs
        out_shape = [jax.ShapeDtypeStruct((m, aw), BF16)] + tail_shape
    return pl.pallas_call(
        functools.partial(_even_in_kernel, chunk=chunk, aw=aw, bw=bw),
        grid=(m // tm,),
        in_specs=in_specs, out_specs=out_specs, out_shape=out_shape,
        compiler_params=_cparams(("parallel",)),
        name="even_in",
    )(*args)


def _gdn_prompt_kernel(qkv_ref, ba_ref, z_ref, cw_ref, prm_ref, onorm_ref,
                       o_ref, st_ref,
                       ext_ref, wm_ref, u0_ref, qe_ref, qk_ref, kwt_ref, egl_ref, *, bw):
    i = pl.program_id(1)
    n = GDN_STEP
    c = GDN_CHUNK
    hd = bw // B_HEADS

    @pl.when(i == 0)
    def _():
        st_ref[...] = jnp.zeros_like(st_ref)
        ext_ref[0:8, :] = jnp.zeros((8, ext_ref.shape[1]), F32)
        ext_ref[n:n + 8, :] = jnp.zeros((8, ext_ref.shape[1]), F32)
        wm_ref[...] = jnp.zeros_like(wm_ref)
        u0_ref[...] = jnp.zeros_like(u0_ref)
        qe_ref[...] = jnp.zeros_like(qe_ref)
        qk_ref[...] = jnp.zeros_like(qk_ref)
        kwt_ref[...] = jnp.zeros_like(kwt_ref)
        egl_ref[...] = jnp.zeros_like(egl_ref)

    heads = range(B_HEADS)
    zero_half = jnp.zeros((c, hd), F32)
    egl = [(egl_ref[0:1, B_HEADS + h:B_HEADS + h + 1], egl_ref[c:c + 1, B_HEADS + h:B_HEADS + h + 1])
           for h in heads]
    state = {"s": [st_ref[0, h] for h in heads]}

    def advance_stage_u(r0):
        state["sb"] = [s.astype(BF16) for s in state["s"]]
        state["u"] = [u0_ref[h, r0:r0 + c, :] - _dot(wm_ref[h, r0:r0 + c, :], state["sb"][h])
                      for h in heads]

    def advance_stage_s(r0, part):
        halves = [[u, zero_half] if part == 0 else [zero_half, u] for u in state["u"]]
        uf = [jnp.concatenate(hv, axis=0).astype(BF16) for hv in halves]
        o = [_dot(qe_ref[h, r0:r0 + c, :], state["sb"][h]) + _dot(qk_ref[h, r0:r0 + c, :], uf[h])
             for h in heads]
        state["s"] = [egl[h][part] * state["s"][h] + _dot(kwt_ref[h], uf[h]) for h in heads]
        return o

    ext_ref[5:8, :] = ext_ref[n + 5:n + 8, :]
    cur = qkv_ref[0]
    ext_ref[8:n + 8, :] = cur
    ext = ext_ref[...]
    y = cw_ref[B_CONV - 1:B_CONV, :] * cur
    for back in range(1, B_CONV):
        y = y + cw_ref[B_CONV - 1 - back:B_CONV - back, :] * pltpu.roll(ext, back, 0)[8:n + 8, :]
    y = y * _sigmoid(y)

    ba = ba_ref[0]
    beta = _sigmoid(ba)
    gfull = -jnp.exp(prm_ref[1:2, :]) * _softplus(ba + prm_ref[0:1, :])
    row = lax.broadcasted_iota(jnp.int32, (n, n), 0)
    col = lax.broadcasted_iota(jnp.int32, (n, n), 1)
    same = (row < c) == (col < c)
    incl = same & (col <= row)
    strict = same & (col < row)
    eye = jnp.where(row == col, 1.0, 0.0)

    advance_stage_u(0)
    g_hi, g_lo = _split(gfull)
    ones_incl = jnp.where(incl, 1.0, 0.0).astype(BF16)
    ones_same = jnp.where(same, 1.0, 0.0).astype(BF16)
    gcum = _dot(ones_incl, g_hi) + _dot(ones_incl, g_lo)
    glast = _dot(ones_same, g_hi) + _dot(ones_same, g_lo)
    gcum_t = gcum.T

    qs, ks, vs, kbs, decays = [], [], [], [], []
    for h in heads:
        q = y[:, h * hd:(h + 1) * hd]
        k = y[:, bw + h * hd:bw + (h + 1) * hd]
        qs.append(q * lax.rsqrt(jnp.sum(q * q, axis=-1, keepdims=True) + EPS) * (hd ** -0.5))
        ks.append(k * lax.rsqrt(jnp.sum(k * k, axis=-1, keepdims=True) + EPS))
        vs.append(y[:, 2 * bw + h * hd:2 * bw + (h + 1) * hd])
        kbs.append(ks[h].astype(BF16))
        gcol = gcum[:, B_HEADS + h:B_HEADS + h + 1]
        grow = gcum_t[B_HEADS + h:B_HEADS + h + 1, :]
        decays.append(jnp.where(incl, jnp.exp(jnp.where(incl, gcol - grow, 0.0)), 0.0))
    bcols = [beta[:, h:h + 1] for h in heads]
    gcols = [gcum[:, B_HEADS + h:B_HEADS + h + 1] for h in heads]
    glcols = [glast[:, B_HEADS + h:B_HEADS + h + 1] for h in heads]

    kk = [_dot_nt(kbs[h], kbs[h]) for h in heads]
    o_a = advance_stage_s(0, 0)
    x = [-jnp.where(strict, bcols[h] * decays[h] * kk[h], 0.0) for h in heads]
    p = [eye + x[h] for h in heads]
    xs = [_split(x[h]) for h in heads]
    o_b = None
    for it in range(int(math.log2(c)) - 1):
        xs = [_split(_dot3(xs[h], xs[h])) for h in heads]
        if it == 0:
            advance_stage_u(c)
        p = [p[h] + _dot3(_split(p[h]), xs[h]) for h in heads]
        if it == 1:
            o_b = advance_stage_s(c, 1)
    egs = [jnp.exp(gcols[h]) for h in heads]
    ps = [_split(p[h]) for h in heads]
    wm = [_dot3(ps[h], _split(bcols[h] * egs[h] * ks[h])) for h in heads]
    u0 = [_dot3(ps[h], _split(bcols[h] * vs[h])) for h in heads]
    qk = [_dot_nt(qs[h].astype(BF16), kbs[h]) for h in heads]

    for h in heads:
        st_ref[0, h] = state["s"][h]
        o = jnp.concatenate([o_a[h], o_b[h]], axis=0)
        cs = slice(h * hd, (h + 1) * hd)
        o_ref[0, :, cs] = (_rms(o, onorm_ref[...]) * z_ref[0, :, cs].astype(F32)).astype(BF16)
        wm_ref[h] = wm[h].astype(BF16)
        u0_ref[h] = u0[h]
        qe_ref[h] = (qs[h] * egs[h]).astype(BF16)
        qk_ref[h] = (qk[h] * decays[h]).astype(BF16)
        kwt_ref[h] = (ks[h] * jnp.exp(glcols[h] - gcols[h])).T.astype(BF16)
    egl_ref[...] = jnp.exp(glast)


def _gdn_prompt(qkv, ba, z, cw, prm, onorm):
    b, t, c3 = qkv.shape
    bw = c3 // 3
    hd = bw // B_HEADS
    n = GDN_STEP
    nt = t // n
    cur = lambda bi, i: (bi, jnp.minimum(i, nt - 1), 0)
    prev = lambda bi, i: (bi, jnp.maximum(i - 1, 0), 0)
    return pl.pallas_call(
        functools.partial(_gdn_prompt_kernel, bw=bw),
        grid=(b, nt + 1),
        in_specs=[pl.BlockSpec((1, n, c3), cur), pl.BlockSpec((1, n, 128), cur),
                  pl.BlockSpec((1, n, bw), prev), _const_spec(cw.shape), _const_spec(prm.shape),
                  _const_spec(onorm.shape)],
        out_specs=[pl.BlockSpec((1, n, bw), prev),
                   pl.BlockSpec((1, B_HEADS, hd, hd), lambda bi, i: (bi, 0, 0, 0))],
        out_shape=[jax.ShapeDtypeStruct((b, t, bw), BF16),
                   jax.ShapeDtypeStruct((b, B_HEADS, hd, hd), F32)],
        scratch_shapes=[pltpu.VMEM((n + 8, c3), F32),
                        pltpu.VMEM((B_HEADS, n, hd), BF16), pltpu.VMEM((B_HEADS, n, hd), F32),
                        pltpu.VMEM((B_HEADS, n, hd), BF16), pltpu.VMEM((B_HEADS, n, n), BF16),
                        pltpu.VMEM((B_HEADS, hd, n), BF16), pltpu.VMEM((n, 128), F32)],
        compiler_params=_cparams(("arbitrary", "arbitrary")),
        name="gdn_prompt",
    )(qkv, ba, z, cw, prm, onorm)


def _gdn_step_kernel(qkv_ref, cst_ref, ba_ref, z_ref, st_ref, cw_ref, prm_ref, onorm_ref,
                     o_ref, sto_ref, *, bw):
    nb = qkv_ref.shape[0]
    hd = bw // B_HEADS
    c3 = 3 * bw
    y = (cw_ref[0:1, :] * cst_ref[:, 0:c3] + cw_ref[1:2, :] * cst_ref[:, c3:2 * c3]
         + cw_ref[2:3, :] * cst_ref[:, 2 * c3:3 * c3] + cw_ref[3:4, :] * qkv_ref[...])
    y = y * _sigmoid(y)
    ba = ba_ref[...]
    beta = _sigmoid(ba)
    eg_all = jnp.exp(-jnp.exp(prm_ref[1:2, :]) * _softplus(ba + prm_ref[0:1, :]))
    row = lax.broadcasted_iota(jnp.int32, (hd, hd), 0)
    col = lax.broadcasted_iota(jnp.int32, (hd, hd), 1)
    eye = jnp.where(row == col, 1.0, 0.0)
    for h in range(B_HEADS):
        q = y[:, h * hd:(h + 1) * hd]
        k = y[:, bw + h * hd:bw + (h + 1) * hd]
        v = y[:, 2 * bw + h * hd:2 * bw + (h + 1) * hd]
        q = q * lax.rsqrt(jnp.sum(q * q, axis=-1, keepdims=True) + EPS) * (hd ** -0.5)
        k = k * lax.rsqrt(jnp.sum(k * k, axis=-1, keepdims=True) + EPS)
        qk = jnp.sum(q * k, axis=-1, keepdims=True)
        k_t = _dot_nt(eye, k, HIGHEST)
        q_t = _dot_nt(eye, q, HIGHEST)
        o_rows = []
        for j in range(nb):
            s = st_ref[j, h]
            eg = eg_all[j:j + 1, B_HEADS + h:B_HEADS + h + 1]
            bt = beta[j:j + 1, h:h + 1]
            kc = k_t[:, j:j + 1]
            ks = jnp.sum(kc * s, axis=0, keepdims=True)
            qs = jnp.sum(q_t[:, j:j + 1] * s, axis=0, keepdims=True)
            u = bt * (v[j:j + 1, :] - eg * ks)
            o_rows.append(eg * qs + qk[j:j + 1, :] * u)
            sto_ref[j, h] = eg * s + kc * u
        o = jnp.concatenate(o_rows, axis=0)
        cs = slice(h * hd, (h + 1) * hd)
        o_ref[:, cs] = (_rms(o, onorm_ref[...]) * z_ref[:, cs].astype(F32)).astype(BF16)


def _gdn_step(qkv, cst, ba, z, state, cw, prm, onorm, *, nb):
    m, c3 = qkv.shape
    bw = c3 // 3
    hd = bw // B_HEADS
    row = lambda i: (i, 0)
    st = lambda i: (i, 0, 0, 0)
    return pl.pallas_call(
        functools.partial(_gdn_step_kernel, bw=bw),
        grid=(m // nb,),
        in_specs=[pl.BlockSpec((nb, c3), row), pl.BlockSpec((nb, 3 * c3), row),
                  pl.BlockSpec((nb, 128), row), pl.BlockSpec((nb, bw), row),
                  pl.BlockSpec((nb, B_HEADS, hd, hd), st), _const_spec(cw.shape),
                  _const_spec(prm.shape), _const_spec(onorm.shape)],
        out_specs=[pl.BlockSpec((nb, bw), row), pl.BlockSpec((nb, B_HEADS, hd, hd), st)],
        out_shape=[jax.ShapeDtypeStruct((m, bw), BF16),
                   jax.ShapeDtypeStruct((m, B_HEADS, hd, hd), F32)],
        compiler_params=_cparams(("parallel",)),
        name="gdn_step",
    )(qkv, cst, ba, z, state, cw, prm, onorm)


def _mix_out_kernel(*refs, n_in):
    ins = refs[:n_in]
    x_ref, w_ref, gpost_ref, x1_ref = refs[n_in:]
    mix = None
    r0 = 0
    for a_ref in ins:
        kk = a_ref.shape[1]
        part = _dot(a_ref[...], w_ref[r0:r0 + kk, :])
        mix = part if mix is None else mix + part
        r0 += kk
    x1_ref[...] = x_ref[...] + _rms(mix, gpost_ref[...])


def _mix_out(ins, x, w, gpost, *, tm):
    m, d = x.shape
    row = lambda i: (i, 0)
    return pl.pallas_call(
        functools.partial(_mix_out_kernel, n_in=len(ins)),
        grid=(m // tm,),
        in_specs=[pl.BlockSpec((tm, a.shape[1]), row) for a in ins]
        + [pl.BlockSpec((tm, d), row), _const_spec(w.shape), _const_spec(gpost.shape)],
        out_specs=pl.BlockSpec((tm, d), row),
        out_shape=jax.ShapeDtypeStruct((m, d), F32),
        compiler_params=_cparams(("parallel",)),
        name="mix_out",
    )(*ins, x, w, gpost)


def _ffn_kernel(x_ref, gpre_ref, wg_ref, wu_ref, wd_ref, wc_ref, bc_ref, gpost_ref, *rest,
                tiles_per_seq, fc):
    tm = x_ref.shape[0]
    ff = wg_ref.shape[1]
    if tiles_per_seq is None:
        hist_ref, x2_ref, gout_ref, a_ref = rest
    else:
        x2_ref, gout_ref, carry_ref, gbuf_ref, a_ref = rest

        @pl.when(pl.program_id(0) % tiles_per_seq == 0)
        def _():
            carry_ref[...] = jnp.zeros_like(carry_ref)

    hb = _rms(x_ref[...], gpre_ref[...]).astype(BF16)

    for c0 in range(0, ff, fc):
        cs = slice(c0, min(c0 + fc, ff))
        g = _dot(hb, wg_ref[:, cs])
        up = _dot(hb, wu_ref[:, cs])
        if tiles_per_seq is None:
            gout_ref[:, cs] = g
            sh2 = hist_ref[:, cs]
            sh1 = hist_ref[:, ff + cs.start:ff + cs.stop]
        else:
            gbuf_ref[0:8, cs] = carry_ref[:, cs]
            gbuf_ref[8:tm + 8, cs] = g
            sh2 = gbuf_ref[6:tm + 6, cs]
            sh1 = gbuf_ref[7:tm + 7, cs]
            tail = gbuf_ref[tm:tm + 8, cs]
            carry_ref[:, cs] = tail
            gout_ref[0, :, cs] = tail
        conv = (wc_ref[0:1, cs] * sh2 + wc_ref[1:2, cs] * sh1 + wc_ref[2:3, cs] * g
                + bc_ref[:, cs])
        a_ref[:, cs] = (_gelu(conv) * up).astype(BF16)
    x2_ref[...] = x_ref[...] + _rms(_dot(a_ref[...], wd_ref[...]), gpost_ref[...])


def _ffn(x1, gpre, wg, wu, wd, wc, bc, gpost, *, layer, tm, tiles_per_seq=None, hist=None, fc=256):
    m, d = x1.shape
    ff = wg.shape[2]
    row = lambda i: (i, 0)

    def layer_spec(w):
        return pl.BlockSpec((None,) + w.shape[1:], lambda i: (layer, 0, 0),
                            pipeline_mode=pl.Buffered(1))

    in_specs = [pl.BlockSpec((tm, d), row), _const_spec(gpre.shape), layer_spec(wg),
                layer_spec(wu), layer_spec(wd), _const_spec(wc.shape),
                _const_spec(bc.shape), _const_spec(gpost.shape)]
    args = [x1, gpre, wg, wu, wd, wc, bc, gpost]
    if tiles_per_seq is None:
        in_specs.append(pl.BlockSpec((tm, 2 * ff), row))
        args.append(hist)
        out_specs = [pl.BlockSpec((tm, d), row), pl.BlockSpec((tm, ff), row)]
        out_shape = [jax.ShapeDtypeStruct((m, d), F32), jax.ShapeDtypeStruct((m, ff), F32)]
        scratch = []
    else:
        out_specs = [pl.BlockSpec((tm, d), row), pl.BlockSpec((1, 8, ff), lambda i: (i, 0, 0))]
        out_shape = [jax.ShapeDtypeStruct((m, d), F32),
                     jax.ShapeDtypeStruct((m // tm, 8, ff), F32)]
        scratch = [pltpu.VMEM((8, ff), F32), pltpu.VMEM((tm + 8, ff), F32)]
    scratch.append(pltpu.VMEM((tm, ff), BF16))
    return pl.pallas_call(
        functools.partial(_ffn_kernel, tiles_per_seq=tiles_per_seq, fc=fc),
        grid=(m // tm,),
        in_specs=in_specs, out_specs=out_specs, out_shape=out_shape, scratch_shapes=scratch,
        compiler_params=_cparams(("arbitrary",)),
        name="conv_ffn",
    )(*args)


def _odd_in_kernel(x_ref, gpre_ref, w_ref, *outs, head_major, qscale):
    d = x_ref.shape[1]
    xb = _rms(x_ref[...], gpre_ref[...]).astype(BF16)
    q = _dot(xb, w_ref[:, 0:d]) * qscale
    k = _dot(xb, w_ref[:, d:2 * d])
    v = _dot(xb, w_ref[:, 2 * d:3 * d])
    if head_major:
        qh_ref, kh_ref, vt_ref, kt_ref, v_ref = outs
        tm = x_ref.shape[0]
        kt_ref[0] = k.T
        v_ref[...] = v
        vt = v.T
        hw = d // C_HEADS
        rowi = lax.broadcasted_iota(jnp.int32, (tm, hw), 0)
        lane = lax.broadcasted_iota(jnp.int32, (tm, hw), 1)
        pos = jnp.where((lane & 1) == 0, lax.shift_right_logical(rowi, 4), rowi & 15)
        feat = jnp.where(lane < 2 * POS_SPLITS, pos, 0).astype(F32).astype(BF16)
        ones_pad = jnp.where(lax.broadcasted_iota(jnp.int32, (16, tm), 0) == 0, 1.0, 0.0)
        for h in range(C_HEADS):
            cs = slice(h * hw, (h + 1) * hw)
            qh_ref[0, h] = q[:, cs].astype(BF16)
            kh_ref[0, h, :, 0:hw] = k[:, cs].astype(BF16)
            kh_ref[0, h, :, hw:2 * hw] = feat
            vt_ref[0, h, 0] = jnp.concatenate([vt[cs, :], ones_pad], axis=0).astype(BF16)
    else:
        q_ref, k_ref, v_ref = outs
        q_ref[...] = q
        k_ref[...] = k
        v_ref[...] = v


def _odd_in(x, gpre, w, *, tm, seq_len=None):
    m, d = x.shape
    qscale = (d // (2 * C_HEADS)) ** -0.5
    row = lambda i: (i, 0)
    head_major = seq_len is not None
    if head_major:
        assert tm <= 512
        qscale *= LOG2E
        tps = seq_len // tm
        hw = d // C_HEADS
        nb = m // seq_len
        hm = lambda i: (i // tps, 0, i % tps, 0)
        out_specs = [pl.BlockSpec((1, C_HEADS, tm, hw), hm),
                     pl.BlockSpec((1, C_HEADS, tm, 2 * hw), hm),
                     pl.BlockSpec((1, C_HEADS, 1, hw + 16, tm),
                                  lambda i: (i // tps, 0, i % tps, 0, 0)),
                     pl.BlockSpec((1, d, tm), lambda i: (i // tps, 0, i % tps)),
                     pl.BlockSpec((tm, d), row)]
        out_shape = [jax.ShapeDtypeStruct((nb, C_HEADS, seq_len, hw), BF16),
                     jax.ShapeDtypeStruct((nb, C_HEADS, seq_len, 2 * hw), BF16),
                     jax.ShapeDtypeStruct((nb, C_HEADS, tps, hw + 16, tm), BF16),
                     jax.ShapeDtypeStruct((nb, d, seq_len), F32),
                     jax.ShapeDtypeStruct((m, d), F32)]
    else:
        out_specs = [pl.BlockSpec((tm, d), row)] * 3
        out_shape = [jax.ShapeDtypeStruct((m, d), F32)] * 3
    return pl.pallas_call(
        functools.partial(_odd_in_kernel, head_major=head_major, qscale=qscale),
        grid=(m // tm,),
        in_specs=[pl.BlockSpec((tm, d), row), _const_spec(gpre.shape), _const_spec(w.shape)],
        out_specs=out_specs, out_shape=out_shape,
        compiler_params=_cparams(("parallel",)),
        name="odd_in",
    )(x, gpre, w)


def _lambda_value(lam_ref, lam_init):
    l = lam_ref[...]
    a = jnp.sum(l[0:1, :] * l[1:2, :], axis=-1, keepdims=True)
    b = jnp.sum(l[2:3, :] * l[3:4, :], axis=-1, keepdims=True)
    return jnp.exp(a) - jnp.exp(b) + lam_init


def _attn_prompt_kernel(slope2_ref, q_ref, k_ref, vt_ref, lam_ref, subln_ref, o_ref,
                        qs_ref, qn_ref, s_ref, m_ref, acc_ref, *, tq, tk, qb, lam_init,
                        side_work=None):
    h = pl.program_id(1)
    iq = pl.program_id(2)
    nq = pl.num_programs(2)
    hw = q_ref.shape[3]
    hd = hw // 2
    r = 2 * tq
    slope2 = slope2_ref[h]

    lane1 = lax.broadcasted_iota(jnp.int32, (1, hw), 1)
    rest = jnp.full((1, hw), slope2, F32)
    feat = jnp.zeros((1, hw), F32)
    for i in range(POS_SPLITS):
        piece = rest.astype(BF16).astype(F32)
        rest = rest - piece
        feat = jnp.where(lane1 == 2 * i, 16.0 * piece, jnp.where(lane1 == 2 * i + 1, piece, feat))
    feat = jnp.broadcast_to(feat, (r, hw)).astype(BF16)
    lane = lax.broadcasted_iota(jnp.int32, (tq, hw), 1)

    def stack_queries(dst_ref, tile):
        q = q_ref[0, 0, pl.ds(pl.multiple_of(tile * tq, tq), tq), :]
        dst_ref[0:tq, 0:hw] = jnp.where(lane < hd, q, jnp.zeros_like(q))
        dst_ref[tq:r, 0:hw] = jnp.where(lane >= hd, q, jnp.zeros_like(q))
        dst_ref[:, hw:2 * hw] = feat

    stack_queries(qs_ref, iq)
    stack_queries(qn_ref, jnp.minimum(iq + 1, nq - 1))
    m_ref[...] = jnp.full_like(m_ref, NEG_BIG)
    acc_ref[...] = jnp.zeros_like(acc_ref)

    def keys(kc):
        return k_ref[0, 0, pl.ds(pl.multiple_of(kc * tk, tk), tk), :]

    def step(kc, kb_next, qsrc_ref, masked, between=None):
        vtb = vt_ref[0, 0, kc]
        cshift = slope2 * (kc * tk).astype(F32)
        for c0 in range(0, r, qb):
            if between is not None and c0 == r // 2:
                between()
            cols = slice(c0, c0 + qb)
            kn = min(tk, (c0 % tq) + qb) if masked else tk
            s = s_ref[0:kn, cols]
            s_ref[:, cols] = _dot_nt(kb_next, qsrc_ref[cols, :])
            if masked:
                key_i = lax.broadcasted_iota(jnp.int32, (kn, qb), 0)
                qry_i = lax.broadcasted_iota(jnp.int32, (kn, qb), 1)
                s = jnp.where(key_i <= (c0 % tq) + qry_i, s, NEG_BIG)
            m_old = m_ref[:, cols]
            m_new = jnp.maximum(m_old, jnp.max(s, axis=0, keepdims=True) + cshift)
            p = jnp.exp2(s - (m_new - cshift)).astype(BF16)
            alpha = jnp.exp2(m_old - m_new)
            acc_ref[:, cols] = alpha * acc_ref[:, cols] + _dot(vtb[:, 0:kn], p)
            m_ref[:, cols] = m_new

    @pl.when(iq == 0)
    def _():
        s_ref[...] = _dot_nt(keys(0), qs_ref[...])

    done = 0
    for width in (8, 4, 2, 1):
        trips = lax.shift_right_logical(iq - done, width.bit_length() - 1)

        def body(j, carry, first=done, width=width):
            for u in range(width):
                kc = first + width * j + u
                step(kc, keys(kc + 1), qs_ref, False)
            return carry

        lax.fori_loop(0, trips, body, 0)
        done = done + width * trips
    before, between, after = side_work or (None, None, None)
    if before is not None:
        before()
    step(iq, keys(0), qn_ref, True, between)
    if after is not None:
        after()

    lam = _lambda_value(lam_ref, lam_init)
    o1 = acc_ref[0:hw, 0:tq] / acc_ref[hw:hw + 1, 0:tq]
    o2 = acc_ref[0:hw, tq:r] / acc_ref[hw:hw + 1, tq:r]
    att = (o1 - lam * o2).T
    o_ref[0] = (_rms(att, subln_ref[...]) * (1.0 - lam_init)).astype(BF16)


def _attn_prompt(slopes, qh, kh, vt, lam_p, subln, *, tq, lam_init):
    b, nh, t, hw = qh.shape
    tk = vt.shape[4]
    assert tq == tk
    return pl.pallas_call(
        functools.partial(_attn_prompt_kernel, tq=tq, tk=tk, qb=min(512, tq), lam_init=lam_init),
        grid=(b, nh, t // tq),
        in_specs=[pl.BlockSpec(memory_space=pltpu.SMEM),
                  pl.BlockSpec((1, 1, t, hw), lambda bi, h, iq: (bi, h, 0, 0)),
                  pl.BlockSpec((1, 1, t, 2 * hw), lambda bi, h, iq: (bi, h, 0, 0)),
                  pl.BlockSpec((1, 1, t // tk, hw + 16, tk), lambda bi, h, iq: (bi, h, 0, 0, 0)),
                  _const_spec(lam_p.shape), _const_spec(subln.shape)],
        out_specs=pl.BlockSpec((1, tq, hw), lambda bi, h, iq: (bi, iq, h)),
        out_shape=jax.ShapeDtypeStruct((b, t, nh * hw), BF16),
        scratch_shapes=[pltpu.VMEM((2 * tq, 2 * hw), BF16), pltpu.VMEM((2 * tq, 2 * hw), BF16),
                        pltpu.VMEM((tk, 2 * tq), F32), pltpu.VMEM((1, 2 * tq), F32),
                        pltpu.VMEM((hw + 16, 2 * tq), F32)],
        compiler_params=_cparams(("arbitrary", "arbitrary", "arbitrary")),
        name="attn_prompt",
    )(slopes * LOG2E, qh, kh, vt, lam_p, subln)


def _decode_stages(q_ref, kn_ref, vn_ref, slope_ref, expand_ref, lam_ref, subln_ref, k_refs,
                   v_refs, o_ref, m_ref, l_ref, acc_ref, *, j, live, n_past, lam_init):
    pg = len(k_refs)
    _, d, page = k_refs[0].shape
    nh = C_HEADS
    nr = 2 * nh
    hd = d // nr
    st = {}

    def scores():
        slope = slope_ref[:, 0:1]
        rowi = lax.broadcasted_iota(jnp.int32, (nr, d), 0)
        coli = lax.broadcasted_iota(jnp.int32, (nr, d), 1)
        lo = (jnp.where(rowi >= nh, rowi - nh, rowi) * 2 + jnp.where(rowi >= nh, 1, 0)) * hd
        qbd = jnp.where((coli >= lo) & (coli < lo + hd), q_ref[0], 0.0)
        qb = qbd.astype(BF16)
        tpos = lax.broadcasted_iota(jnp.int32, (1, page), 1)
        s_parts = []
        for p_i in range(pg):
            dist = (n_past - ((j * pg + p_i) * page + tpos)).astype(F32)
            s_parts.append(_dot(qb, k_refs[p_i][0].astype(BF16)) - slope * dist)
        s = jnp.where(live, jnp.concatenate(s_parts, axis=1), NEG_BIG)
        first = jnp.logical_and(j == 0, live)
        m_old = jnp.where(first, NEG_BIG, m_ref[...])
        m_new = jnp.maximum(m_old, jnp.max(s, axis=-1, keepdims=True))
        p = jnp.exp(s - m_new).astype(BF16)
        alpha = jnp.exp(m_old - m_new)
        l = alpha * jnp.where(first, 0.0, l_ref[...]) + jnp.sum(p.astype(F32), axis=-1,
                                                                  keepdims=True)
        st.update(qbd=qbd, p=p, alpha=alpha, first=first, m=m_new, l=l)
        m_ref[...] = m_new
        l_ref[...] = l

    def spread():
        rowe = lax.broadcasted_iota(jnp.int32, (nr, nh * page), 0)
        cole = lax.broadcasted_iota(jnp.int32, (nr, nh * page), 1)
        own_head = (cole & (nh - 1)) == jnp.where(rowe >= nh, rowe - nh, rowe)
        p = st["p"]
        stacked = jnp.concatenate([p[:, p_i * page:(p_i + 1) * page] for p_i in range(pg)], axis=0)
        wide = _dot(stacked, expand_ref[...])
        st["pbig"] = [jnp.where(own_head, wide[p_i * nr:(p_i + 1) * nr, :], 0.0).astype(BF16)
                      for p_i in range(pg)]

    def values():
        pv = [_dot(st["pbig"][p_i], v_refs[p_i][0].astype(BF16)) for p_i in range(pg)]
        acc = st["alpha"] * jnp.where(st["first"], 0.0, acc_ref[...]) + sum(pv[1:], pv[0])
        acc_ref[...] = acc
        s = jnp.sum(st["qbd"] * kn_ref[0], axis=-1, keepdims=True)
        m_new = jnp.maximum(st["m"], s)
        p = jnp.exp(s - m_new)
        alpha = jnp.exp(st["m"] - m_new)
        l = alpha * st["l"] + p
        vn = vn_ref[0]
        acc = alpha * acc + p * jnp.concatenate([vn, vn], axis=0)
        lam = _lambda_value(lam_ref, lam_init)
        att = acc[0:nh, :] / l[0:nh, :] - lam * (acc[nh:nr, :] / l[nh:nr, :])
        o_ref[0] = (_rms(att, subln_ref[...]) * (1.0 - lam_init)).astype(BF16)

    return scores, spread, values


def _decode_scratch_init(m_ref, l_ref, acc_ref):
    m_ref[...] = jnp.full_like(m_ref, NEG_BIG)
    l_ref[...] = jnp.zeros_like(l_ref)
    acc_ref[...] = jnp.zeros_like(acc_ref)


def _attn_decode_kernel(pt_ref, q_ref, kn_ref, vn_ref, slope_ref, expand_ref, lam_ref, subln_ref,
                        *rest, pg, n_past, lam_init):
    k_refs = rest[0:pg]
    v_refs = rest[pg:2 * pg]
    o_ref, m_ref, l_ref, acc_ref = rest[2 * pg:]

    @pl.when((pl.program_id(0) == 0) & (pl.program_id(1) == 0))
    def _():
        _decode_scratch_init(m_ref, l_ref, acc_ref)

    for stage in _decode_stages(q_ref, kn_ref, vn_ref, slope_ref, expand_ref, lam_ref, subln_ref,
                                k_refs, v_refs, o_ref, m_ref, l_ref, acc_ref,
                                j=pl.program_id(1), live=True, n_past=n_past, lam_init=lam_init):
        stage()


def _attn_fused_kernel(pt_ref, slope2_ref, q_ref, k_ref, vt_ref, lam_ref, subln_ref,
                       dq_ref, dkn_ref, dvn_ref, dslope_ref, expand_ref, *rest,
                       pg, steps_per_seq, n_dec, n_past, lam_init, **attn_kw):
    k_refs = rest[0:pg]
    v_refs = rest[pg:2 * pg]
    o_ref, do_ref = rest[2 * pg:2 * pg + 2]
    attn_scratch = rest[2 * pg + 2:-3]
    dm_ref, dl_ref, dacc_ref = rest[-3:]
    g = (pl.program_id(0) * pl.num_programs(1) + pl.program_id(1)) * pl.num_programs(2) \
        + pl.program_id(2)

    @pl.when(g == 0)
    def _():
        _decode_scratch_init(dm_ref, dl_ref, dacc_ref)

    side_work = _decode_stages(dq_ref, dkn_ref, dvn_ref, dslope_ref, expand_ref, lam_ref,
                               subln_ref, k_refs, v_refs, do_ref, dm_ref, dl_ref, dacc_ref,
                               j=jnp.minimum(g, n_dec - 1) % steps_per_seq, live=g < n_dec,
                               n_past=n_past, lam_init=lam_init)
    _attn_prompt_kernel(slope2_ref, q_ref, k_ref, vt_ref, lam_ref, subln_ref, o_ref,
                        *attn_scratch, lam_init=lam_init, side_work=side_work, **attn_kw)


def _attn_decode(page_table, q, kn, vn, slope_tile, lam_p, subln, cache_kt, cache_v2, *, pg,
                 lam_init):
    bs, n_pages = page_table.shape
    _, d, page = cache_kt.shape
    hw = d // C_HEADS
    q3, kn3 = (a.reshape(bs, 1, d) for a in (q, kn))
    vn3 = vn.reshape(bs, C_HEADS, hw)
    expand = (jnp.arange(page * C_HEADS)[None, :] // C_HEADS
              == jnp.arange(page)[:, None]).astype(BF16)
    row = lambda b, j, pt: (b, 0, 0)
    const2 = lambda b, j, pt: (0, 0)

    def page_map(p_i):
        return lambda b, j, pt: (pt[b, j * pg + p_i], 0, 0)

    k_specs = [pl.BlockSpec((1, d, page), page_map(p_i)) for p_i in range(pg)]
    v_specs = [pl.BlockSpec((1, page * C_HEADS, hw), page_map(p_i)) for p_i in range(pg)]
    grid_spec = pltpu.PrefetchScalarGridSpec(
        num_scalar_prefetch=1,
        grid=(bs, n_pages // pg),
        in_specs=[pl.BlockSpec((1, 1, d), row), pl.BlockSpec((1, 1, d), row),
                  pl.BlockSpec((1, C_HEADS, hw), row), pl.BlockSpec(slope_tile.shape, const2),
                  pl.BlockSpec(expand.shape, const2), pl.BlockSpec(lam_p.shape, const2),
                  pl.BlockSpec(subln.shape, const2)] + k_specs + v_specs,
        out_specs=pl.BlockSpec((1, C_HEADS, hw), row),
        scratch_shapes=[pltpu.VMEM((2 * C_HEADS, 1), F32), pltpu.VMEM((2 * C_HEADS, 1), F32),
                        pltpu.VMEM((2 * C_HEADS, hw), F32)],
    )
    out = pl.pallas_call(
        functools.partial(_attn_decode_kernel, pg=pg, n_past=n_pages * page, lam_init=lam_init),
        grid_spec=grid_spec,
        out_shape=jax.ShapeDtypeStruct((bs, C_HEADS, hw), BF16),
        compiler_params=_cparams(("arbitrary", "arbitrary")),
        name="attn_decode",
    )(page_table, q3, kn3, vn3, slope_tile, expand, lam_p, subln, *([cache_kt] * pg),
      *([cache_v2] * pg))
    return out.reshape(bs, d)


def _attn_fused(slopes, qh, kh, vt, lam_p, subln, page_table, q, kn, vn, slope_tile, cache_kt,
                cache_v2, *, tq, pg, lam_init):
    b, nh, t, hw = qh.shape
    tk = vt.shape[4]
    assert tq == tk
    nq = t // tq
    bs, n_pages = page_table.shape
    _, d, page = cache_kt.shape
    spp = n_pages // pg
    n_dec = bs * spp
    assert n_dec <= b * nh * nq
    q3, kn3 = (a.reshape(bs, 1, d) for a in (q, kn))
    vn3 = vn.reshape(bs, C_HEADS, hw)
    expand = (jnp.arange(page * C_HEADS)[None, :] // C_HEADS
              == jnp.arange(page)[:, None]).astype(BF16)

    def dec(bi, h, iq):
        ds = jnp.minimum((bi * nh + h) * nq + iq, n_dec - 1)
        return ds // spp, ds % spp

    seq_row = lambda bi, h, iq, pt: (dec(bi, h, iq)[0], 0, 0)
    const2 = lambda bi, h, iq, pt: (0, 0)
    head = lambda bi, h, iq, pt: (bi, h, 0, 0)

    def page_map(p_i):
        def index(bi, h, iq, pt):
            sb, j = dec(bi, h, iq)
            return pt[sb, j * pg + p_i], 0, 0
        return index

    k_specs = [pl.BlockSpec((1, d, page), page_map(p_i)) for p_i in range(pg)]
    v_specs = [pl.BlockSpec((1, page * C_HEADS, hw), page_map(p_i)) for p_i in range(pg)]
    grid_spec = pltpu.PrefetchScalarGridSpec(
        num_scalar_prefetch=1,
        grid=(b, nh, nq),
        in_specs=[pl.BlockSpec(memory_space=pltpu.SMEM),
                  pl.BlockSpec((1, 1, t, hw), head), pl.BlockSpec((1, 1, t, 2 * hw), head),
                  pl.BlockSpec((1, 1, t // tk, hw + 16, tk), lambda bi, h, iq, pt: (bi, h, 0, 0, 0)),
                  pl.BlockSpec(lam_p.shape, const2), pl.BlockSpec(subln.shape, const2),
                  pl.BlockSpec((1, 1, d), seq_row), pl.BlockSpec((1, 1, d), seq_row),
                  pl.BlockSpec((1, C_HEADS, hw), seq_row), pl.BlockSpec(slope_tile.shape, const2),
                  pl.BlockSpec(expand.shape, const2)] + k_specs + v_specs,
        out_specs=[pl.BlockSpec((1, tq, hw), lambda bi, h, iq, pt: (bi, iq, h)),
                   pl.BlockSpec((1, C_HEADS, hw), seq_row)],
        scratch_shapes=[pltpu.VMEM((2 * tq, 2 * hw), BF16), pltpu.VMEM((2 * tq, 2 * hw), BF16),
                        pltpu.VMEM((tk, 2 * tq), F32), pltpu.VMEM((1, 2 * tq), F32),
                        pltpu.VMEM((hw + 16, 2 * tq), F32),
                        pltpu.VMEM((2 * C_HEADS, 1), F32), pltpu.VMEM((2 * C_HEADS, 1), F32),
                        pltpu.VMEM((2 * C_HEADS, hw), F32)],
    )
    att, att_s = pl.pallas_call(
        functools.partial(_attn_fused_kernel, pg=pg, steps_per_seq=spp, n_dec=n_dec,
                          n_past=n_pages * page, lam_init=lam_init, tq=tq, tk=tk,
                          qb=min(512, tq)),
        grid_spec=grid_spec,
        out_shape=[jax.ShapeDtypeStruct((b, t, nh * hw), BF16),
                   jax.ShapeDtypeStruct((bs, C_HEADS, hw), BF16)],
        compiler_params=_cparams(("arbitrary", "arbitrary", "arbitrary")),
        name="attn_fused",
    )(page_table, slopes * LOG2E, qh, kh, vt, lam_p, subln, q3, kn3, vn3, slope_tile, expand,
      *([cache_kt] * pg), *([cache_v2] * pg))
    return att, att_s.reshape(bs, d)


def _row_tile(m, pref):
    tm = min(pref, m)
    assert m % tm == 0, (m, tm)
    return tm


def kernel(x_prompt, x_sample, cache_k, cache_v, page_table, state_gdn, state_gdn_conv, state_ffn_conv, norm_mix_pre, norm_mix_post, norm_ffn_pre, norm_ffn_post, w_in_even, a_v_norm, a_w_s, a_b_s, b_conv_w, b_a_log, b_dt_bias, b_out_norm, w_out_even, w_in_odd, c_lambda, c_subln, w_out_odd, w_ffn_gate, w_ffn_up, w_ffn_conv, b_ffn_conv, w_ffn_down):
    b, t, d = x_prompt.shape
    bs = x_sample.shape[0]
    assert x_sample.shape[1] == 1 and t % GDN_STEP == 0
    aw = a_v_norm.shape[-1]
    bw = b_conv_w.shape[-1] // 3
    ff = w_ffn_gate.shape[-1]
    n_split = 2 * aw + 4 * bw
    hw = d // C_HEADS
    _, n_pool, page, _, _, hd = cache_k.shape
    row2 = lambda a: a.reshape(1, -1)

    w_even = w_in_even[0].astype(BF16)
    w_ba = jnp.pad(w_in_even[0, :, n_split:], ((0, 0), (0, 128 - 2 * B_HEADS))).astype(BF16)
    prm = jnp.zeros((8, 128), F32)
    prm = prm.at[0, B_HEADS:2 * B_HEADS].set(b_dt_bias[0]).at[1, B_HEADS:2 * B_HEADS].set(b_a_log[0])
    onorm = row2(b_out_norm[0])
    w_oe = w_out_even[0].astype(BF16)
    w_odd = w_in_odd[0].astype(BF16)
    w_oo = w_out_odd[0].astype(BF16)
    wg = w_ffn_gate.astype(BF16)
    wu = w_ffn_up.astype(BF16)
    wd = w_ffn_down.astype(BF16)
    slopes = jnp.exp2(-8.0 * jnp.arange(1, C_HEADS + 1, dtype=F32) / C_HEADS)
    slope_tile = jnp.broadcast_to(jnp.tile(slopes, 2)[:, None], (2 * C_HEADS, 128))
    lam_init = 0.8 - 0.6 * math.exp(-0.3 * 1)
    ws_step = row2(jnp.repeat(a_w_s[0, :, 0, 0], aw // A_GROUPS))
    bs_step = row2(jnp.repeat(a_b_s[0, :, 0], aw // A_GROUPS))
    bs_t = a_b_s[0].T

    def ffn_layer(layer, x1, **kw):
        return _ffn(x1, row2(norm_ffn_pre[layer]), wg, wu, wd, w_ffn_conv[layer],
                    row2(b_ffn_conv[layer]), row2(norm_ffn_post[layer]), layer=layer, **kw)

    tm = _row_tile(b * t, 512)
    tps = t // tm
    xp = x_prompt.reshape(b * t, d)
    a_out, qkv, z, ba = _even_in(xp, row2(norm_mix_pre[0]), w_even, w_ba, row2(a_v_norm[0]),
                                 a_w_s[0], bs_t, chunk=A_CHUNK, tm=tm)
    qkv3 = qkv.reshape(b, t, 3 * bw)
    o, gdn_state_p = _gdn_prompt(qkv3, ba.reshape(b, t, 128), z.reshape(b, t, bw), b_conv_w[0],
                                 prm, onorm)
    gdn_conv_p = qkv3[:, t - (B_CONV - 1):, :]
    x1 = _mix_out([a_out, o.reshape(b * t, bw)], xp, w_oe, row2(norm_mix_post[0]), tm=tm)
    x2, gt0 = ffn_layer(0, x1, tm=tm, tiles_per_seq=tps)
    qh, kh, vt, kt_p, v_p = _odd_in(x2, row2(norm_mix_pre[1]), w_odd, tm=tm, seq_len=t)

    xs = x_sample.reshape(bs, d)
    a_out_s, v_s, qkv_s, z_s, ba_s = _even_in(xs, row2(norm_mix_pre[0]), w_even, w_ba,
                                              row2(a_v_norm[0]), ws_step, bs_step, chunk=1, tm=bs)
    o_s, gdn_state_s = _gdn_step(qkv_s, state_gdn_conv[0].reshape(bs, -1), ba_s, z_s, state_gdn[0],
                                 b_conv_w[0], prm, onorm, nb=min(16, bs))
    x1s = _mix_out([a_out_s, o_s], xs, w_oe, row2(norm_mix_post[0]), tm=bs)
    x2s, g0s = ffn_layer(0, x1s, tm=bs, hist=state_ffn_conv[0].reshape(bs, -1))
    q_s, k_s, v_sn = _odd_in(x2s, row2(norm_mix_pre[1]), w_odd, tm=bs)

    n_pages = page_table.shape[1]
    cache_kt = jnp.transpose(cache_k[0], (0, 2, 3, 4, 1)).reshape(n_pool, d, page)
    cache_v2 = cache_v[0].reshape(n_pool, page * C_HEADS, hw)
    lam_p, subln = c_lambda[0], row2(c_subln[0])
    ride = [c for c in (1, 2, 4, 8) if n_pages % c == 0
            and bs * (n_pages // c) <= b * C_HEADS * (t // tm)]
    if ride:
        att, att_s = _attn_fused(slopes, qh, kh, vt, lam_p, subln, page_table, q_s, k_s, v_sn,
                                 slope_tile, cache_kt, cache_v2, tq=tm, pg=ride[0],
                                 lam_init=lam_init)
    else:
        att = _attn_prompt(slopes, qh, kh, vt, lam_p, subln, tq=tm, lam_init=lam_init)
        pg = next(c for c in (16, 8, 4, 2, 1) if n_pages % c == 0)
        att_s = _attn_decode(page_table, q_s, k_s, v_sn, slope_tile, lam_p, subln, cache_kt,
                             cache_v2, pg=pg, lam_init=lam_init)

    x3 = _mix_out([att.reshape(b * t, d)], x2, w_oo, row2(norm_mix_post[1]), tm=tm)
    y_p, gt1 = ffn_layer(1, x3, tm=tm, tiles_per_seq=tps)
    ffn_conv_p = jnp.stack([g.reshape(b, tps, 8, ff)[:, -1, 8 - (FFN_CONV - 1):, :]
                            for g in (gt0, gt1)])
    x3s = _mix_out([att_s], x2s, w_oo, row2(norm_mix_post[1]), tm=bs)
    y_s, g1s = ffn_layer(1, x3s, tm=bs, hist=state_ffn_conv[1].reshape(bs, -1))
    ffn_conv_s = jnp.stack([jnp.concatenate([state_ffn_conv[l][:, 1:], g[:, None, :]], axis=1)
                            for l, g in ((0, g0s), (1, g1s))])

    return (y_p.reshape(b, t, d), y_s.reshape(bs, 1, d),
            gdn_state_p[None], gdn_state_s[None],
            gdn_conv_p[None],
            jnp.concatenate([state_gdn_conv[0][:, 1:], qkv_s[:, None, :]], axis=1)[None],
            v_s.reshape(1, bs, 1, aw),
            jnp.transpose(kt_p.reshape(1, b, C_HEADS, 2, hd, t), (0, 1, 5, 2, 3, 4)),
            v_p.reshape(1, b, t, C_HEADS, hw),
            k_s.reshape(1, bs, 1, C_HEADS, 2, hd), v_sn.reshape(1, bs, 1, C_HEADS, hw),
            ffn_conv_p, ffn_conv_s)
```

```python
import functools
import math

import jax
import jax.numpy as jnp
from jax import lax
from jax.experimental import pallas as pl
from jax.experimental.pallas import tpu as pltpu

F32 = jnp.float32
BF16 = jnp.bfloat16
EPS = 1e-6

A_GROUPS = 4
A_CHUNK = 128
B_HEADS = 4
B_CONV = 4
GDN_STEP = 128
GDN_CHUNK = 64
C_HEADS = 8
FFN_CONV = 3
NEG_BIG = -1e30
LOG2E = 1.4426950408889634
POS_SPLITS = 3

VMEM_LIMIT_BYTES = 56 * 1024 * 1024
HIGHEST = lax.Precision.HIGHEST


def _cparams(sem):
    return pltpu.CompilerParams(dimension_semantics=sem, vmem_limit_bytes=VMEM_LIMIT_BYTES)


def _gelu(x):
    return 0.5 * x * (1.0 + jnp.tanh(0.7978845608028654 * (x + 0.044715 * (x * x * x))))


def _sigmoid(x):
    return 1.0 / (1.0 + jnp.exp(-x))


def _softplus(x):
    return jnp.maximum(x, 0.0) + jnp.log(1.0 + jnp.exp(-jnp.abs(x)))


def _rms(x, gain):
    return x * lax.rsqrt(jnp.mean(x * x, axis=-1, keepdims=True) + EPS) * gain


def _dot(a, b, precision=None):
    return jnp.dot(a, b, preferred_element_type=F32, precision=precision)


def _dot_nt(a, b, precision=None):
    return lax.dot_general(a, b, (((1,), (1,)), ((), ())), preferred_element_type=F32,
                           precision=precision)


def _split(a):
    hi = a.astype(BF16)
    return hi, (a - hi.astype(F32)).astype(BF16)


def _dot3(a, b):
    (ah, al), (bh, bl) = a, b
    return _dot(ah, bh) + _dot(ah, bl) + _dot(al, bh)


def _const_spec(shape):
    nd = len(shape)
    return pl.BlockSpec(shape, lambda *_: (0,) * nd, pipeline_mode=pl.Buffered(1))


def _even_in_kernel(x_ref, gpre_ref, w_ref, wba_ref, avn_ref, ws_ref, bs_ref, *rest,
                    chunk, aw, bw):
    tm = x_ref.shape[0]
    if chunk == 1:
        aout_ref, v_ref, qkv_ref, z_ref, ba_ref = rest
    else:
        aout_ref, qkv_ref, z_ref, ba_ref = rest
    xb = _rms(x_ref[...], gpre_ref[...]).astype(BF16)
    u = _gelu(_dot(xb, w_ref[:, 0:aw]))
    v = _rms(_gelu(_dot(xb, w_ref[:, aw:2 * aw])), avn_ref[...])
    qkv_ref[...] = _dot(xb, w_ref[:, 2 * aw:2 * aw + 3 * bw])
    zz = _dot(xb, w_ref[:, 2 * aw + 3 * bw:2 * aw + 4 * bw])
    z_ref[...] = (zz * _sigmoid(zz)).astype(BF16)
    ba_ref[...] = _dot(xb, wba_ref[...])
    gd = aw // A_GROUPS
    if chunk == 1:
        v_ref[...] = v
        aout_ref[...] = (u * (v * ws_ref[...] + bs_ref[...])).astype(BF16)
    else:
        row = lax.broadcasted_iota(jnp.int32, (chunk, chunk), 0)
        col = lax.broadcasted_iota(jnp.int32, (chunk, chunk), 1)
        for g in range(A_GROUPS):
            wt = jnp.where(col <= row, ws_ref[g], 0.0).astype(BF16)
            bcol = bs_ref[:, g:g + 1]
            for n in range(tm // chunk):
                rs = slice(n * chunk, (n + 1) * chunk)
                cs = slice(g * gd, (g + 1) * gd)
                mixed = _dot(wt, v[rs, cs].astype(BF16)) + bcol
                aout_ref[rs, cs] = (u[rs, cs] * mixed).astype(BF16)


def _even_in(x, gpre, w_main, w_ba, avn, ws, bs, *, chunk, tm):
    m, d = x.shape
    aw = avn.shape[-1]
    bw = (w_main.shape[1] - 2 * aw - 2 * B_HEADS) // 4
    row = lambda i: (i, 0)
    in_specs = [pl.BlockSpec((tm, d), row), _const_spec(gpre.shape), _const_spec(w_main.shape),
                _const_spec(w_ba.shape), _const_spec(avn.shape), _const_spec(ws.shape),
                _const_spec(bs.shape)]
    args = [x, gpre, w_main, w_ba, avn, ws, bs]
    tail_specs = [pl.BlockSpec((tm, 3 * bw), row), pl.BlockSpec((tm, bw), row),
                  pl.BlockSpec((tm, 128), row)]
    tail_shape = [jax.ShapeDtypeStruct((m, 3 * bw), F32), jax.ShapeDtypeStruct((m, bw), BF16),
                  jax.ShapeDtypeStruct((m, 128), F32)]
    if chunk == 1:
        out_specs = [pl.BlockSpec((tm, aw), row), pl.BlockSpec((tm, aw), row)] + tail_specs
        out_shape = ([jax.ShapeDtypeStruct((m, aw), BF16), jax.ShapeDtypeStruct((m, aw), F32)]
                     + tail_shape)
    else:
        out_specs = [pl.BlockSpec((tm, aw), row)] + tail_specs
        out_shape = [jax.ShapeDtypeStruct((m, aw), BF16)] + tail_shape
    return pl.pallas_call(
        functools.partial(_even_in_kernel, chunk=chunk, aw=aw, bw=bw),
        grid=(m // tm,),
        in_specs=in_specs, out_specs=out_specs, out_shape=out_shape,
        compiler_params=_cparams(("parallel",)),
        name="even_in",
    )(*args)


def _gdn_prompt_kernel(qkv_ref, ba_ref, z_ref, cw_ref, prm_ref, onorm_ref,
                       o_ref, st_ref,
                       ext_ref, wm_ref, u0_ref, qe_ref, qk_ref, kwt_ref, egl_ref, *, bw):
    i = pl.program_id(1)
    n = GDN_STEP
    c = GDN_CHUNK
    hd = bw // B_HEADS

    @pl.when(i == 0)
    def _():
        st_ref[...] = jnp.zeros_like(st_ref)
        ext_ref[0:8, :] = jnp.zeros((8, ext_ref.shape[1]), F32)
        ext_ref[n:n + 8, :] = jnp.zeros((8, ext_ref.shape[1]), F32)
        wm_ref[...] = jnp.zeros_like(wm_ref)
        u0_ref[...] = jnp.zeros_like(u0_ref)
        qe_ref[...] = jnp.zeros_like(qe_ref)
        qk_ref[...] = jnp.zeros_like(qk_ref)
        kwt_ref[...] = jnp.zeros_like(kwt_ref)
        egl_ref[...] = jnp.zeros_like(egl_ref)

    heads = range(B_HEADS)
    zero_half = jnp.zeros((c, hd), F32)
    egl = [(egl_ref[0:1, B_HEADS + h:B_HEADS + h + 1], egl_ref[c:c + 1, B_HEADS + h:B_HEADS + h + 1])
           for h in heads]
    state = {"s": [st_ref[0, h] for h in heads]}

    def advance_stage_u(r0):
        state["sb"] = [s.astype(BF16) for s in state["s"]]
        state["u"] = [u0_ref[h, r0:r0 + c, :] - _dot(wm_ref[h, r0:r0 + c, :], state["sb"][h])
                      for h in heads]

    def advance_stage_s(r0, part):
        halves = [[u, zero_half] if part == 0 else [zero_half, u] for u in state["u"]]
        uf = [jnp.concatenate(hv, axis=0).astype(BF16) for hv in halves]
        o = [_dot(qe_ref[h, r0:r0 + c, :], state["sb"][h]) + _dot(qk_ref[h, r0:r0 + c, :], uf[h])
             for h in heads]
        state["s"] = [egl[h][part] * state["s"][h] + _dot(kwt_ref[h], uf[h]) for h in heads]
        return o

    ext_ref[5:8, :] = ext_ref[n + 5:n + 8, :]
    cur = qkv_ref[0]
    ext_ref[8:n + 8, :] = cur
    ext = ext_ref[...]
    y = cw_ref[B_CONV - 1:B_CONV, :] * cur
    for back in range(1, B_CONV):
        y = y + cw_ref[B_CONV - 1 - back:B_CONV - back, :] * pltpu.roll(ext, back, 0)[8:n + 8, :]
    y = y * _sigmoid(y)

    ba = ba_ref[0]
    beta = _sigmoid(ba)
    gfull = -jnp.exp(prm_ref[1:2, :]) * _softplus(ba + prm_ref[0:1, :])
    row = lax.broadcasted_iota(jnp.int32, (n, n), 0)
    col = lax.broadcasted_iota(jnp.int32, (n, n), 1)
    same = (row < c) == (col < c)
    incl = same & (col <= row)
    strict = same & (col < row)
    eye = jnp.where(row == col, 1.0, 0.0)

    advance_stage_u(0)
    g_hi, g_lo = _split(gfull)
    ones_incl = jnp.where(incl, 1.0, 0.0).astype(BF16)
    ones_same = jnp.where(same, 1.0, 0.0).astype(BF16)
    gcum = _dot(ones_incl, g_hi) + _dot(ones_incl, g_lo)
    glast = _dot(ones_same, g_hi) + _dot(ones_same, g_lo)
    gcum_t = gcum.T

    qs, ks, vs, kbs, decays = [], [], [], [], []
    for h in heads:
        q = y[:, h * hd:(h + 1) * hd]
        k = y[:, bw + h * hd:bw + (h + 1) * hd]
        qs.append(q * lax.rsqrt(jnp.sum(q * q, axis=-1, keepdims=True) + EPS) * (hd ** -0.5))
        ks.append(k * lax.rsqrt(jnp.sum(k * k, axis=-1, keepdims=True) + EPS))
        vs.append(y[:, 2 * bw + h * hd:2 * bw + (h + 1) * hd])
        kbs.append(ks[h].astype(BF16))
        gcol = gcum[:, B_HEADS + h:B_HEADS + h + 1]
        grow = gcum_t[B_HEADS + h:B_HEADS + h + 1, :]
        decays.append(jnp.where(incl, jnp.exp(jnp.where(incl, gcol - grow, 0.0)), 0.0))
    bcols = [beta[:, h:h + 1] for h in heads]
    gcols = [gcum[:, B_HEADS + h:B_HEADS + h + 1] for h in heads]
    glcols = [glast[:, B_HEADS + h:B_HEADS + h + 1] for h in heads]

    kk = [_dot_nt(kbs[h], kbs[h]) for h in heads]
    o_a = advance_stage_s(0, 0)
    x = [-jnp.where(strict, bcols[h] * decays[h] * kk[h], 0.0) for h in heads]
    p = [eye + x[h] for h in heads]
    xs = [_split(x[h]) for h in heads]
    o_b = None
    for it in range(int(math.log2(c)) - 1):
        xs = [_split(_dot3(xs[h], xs[h])) for h in heads]
        if it == 0:
            advance_stage_u(c)
        p = [p[h] + _dot3(_split(p[h]), xs[h]) for h in heads]
        if it == 1:
            o_b = advance_stage_s(c, 1)
    egs = [jnp.exp(gcols[h]) for h in heads]
    ps = [_split(p[h]) for h in heads]
    wm = [_dot3(ps[h], _split(bcols[h] * egs[h] * ks[h])) for h in heads]
    u0 = [_dot3(ps[h], _split(bcols[h] * vs[h])) for h in heads]
    qk = [_dot_nt(qs[h].astype(BF16), kbs[h]) for h in heads]

    for h in heads:
        st_ref[0, h] = state["s"][h]
        o = jnp.concatenate([o_a[h], o_b[h]], axis=0)
        cs = slice(h * hd, (h + 1) * hd)
        o_ref[0, :, cs] = (_rms(o, onorm_ref[...]) * z_ref[0, :, cs].astype(F32)).astype(BF16)
        wm_ref[h] = wm[h].astype(BF16)
        u0_ref[h] = u0[h]
        qe_ref[h] = (qs[h] * egs[h]).astype(BF16)
        qk_ref[h] = (qk[h] * decays[h]).astype(BF16)
        kwt_ref[h] = (ks[h] * jnp.exp(glcols[h] - gcols[h])).T.astype(BF16)
    egl_ref[...] = jnp.exp(glast)


def _gdn_prompt(qkv, ba, z, cw, prm, onorm):
    b, t, c3 = qkv.shape
    bw = c3 // 3
    hd = bw // B_HEADS
    n = GDN_STEP
    nt = t // n
    cur = lambda bi, i: (bi, jnp.minimum(i, nt - 1), 0)
    prev = lambda bi, i: (bi, jnp.maximum(i - 1, 0), 0)
    return pl.pallas_call(
        functools.partial(_gdn_prompt_kernel, bw=bw),
        grid=(b, nt + 1),
        in_specs=[pl.BlockSpec((1, n, c3), cur), pl.BlockSpec((1, n, 128), cur),
                  pl.BlockSpec((1, n, bw), prev), _const_spec(cw.shape), _const_spec(prm.shape),
                  _const_spec(onorm.shape)],
        out_specs=[pl.BlockSpec((1, n, bw), prev),
                   pl.BlockSpec((1, B_HEADS, hd, hd), lambda bi, i: (bi, 0, 0, 0))],
        out_shape=[jax.ShapeDtypeStruct((b, t, bw), BF16),
                   jax.ShapeDtypeStruct((b, B_HEADS, hd, hd), F32)],
        scratch_shapes=[pltpu.VMEM((n + 8, c3), F32),
                        pltpu.VMEM((B_HEADS, n, hd), BF16), pltpu.VMEM((B_HEADS, n, hd), F32),
                        pltpu.VMEM((B_HEADS, n, hd), BF16), pltpu.VMEM((B_HEADS, n, n), BF16),
                        pltpu.VMEM((B_HEADS, hd, n), BF16), pltpu.VMEM((n, 128), F32)],
        compiler_params=_cparams(("arbitrary", "arbitrary")),
        name="gdn_prompt",
    )(qkv, ba, z, cw, prm, onorm)


def _gdn_step_kernel(qkv_ref, cst_ref, ba_ref, z_ref, st_ref, cw_ref, prm_ref, onorm_ref,
                     o_ref, sto_ref, *, bw):
    nb = qkv_ref.shape[0]
    hd = bw // B_HEADS
    c3 = 3 * bw
    y = (cw_ref[0:1, :] * cst_ref[:, 0:c3] + cw_ref[1:2, :] * cst_ref[:, c3:2 * c3]
         + cw_ref[2:3, :] * cst_ref[:, 2 * c3:3 * c3] + cw_ref[3:4, :] * qkv_ref[...])
    y = y * _sigmoid(y)
    ba = ba_ref[...]
    beta = _sigmoid(ba)
    eg_all = jnp.exp(-jnp.exp(prm_ref[1:2, :]) * _softplus(ba + prm_ref[0:1, :]))
    row = lax.broadcasted_iota(jnp.int32, (hd, hd), 0)
    col = lax.broadcasted_iota(jnp.int32, (hd, hd), 1)
    eye = jnp.where(row == col, 1.0, 0.0)
    for h in range(B_HEADS):
        q = y[:, h * hd:(h + 1) * hd]
        k = y[:, bw + h * hd:bw + (h + 1) * hd]
        v = y[:, 2 * bw + h * hd:2 * bw + (h + 1) * hd]
        q = q * lax.rsqrt(jnp.sum(q * q, axis=-1, keepdims=True) + EPS) * (hd ** -0.5)
        k = k * lax.rsqrt(jnp.sum(k * k, axis=-1, keepdims=True) + EPS)
        qk = jnp.sum(q * k, axis=-1, keepdims=True)
        k_t = _dot_nt(eye, k, HIGHEST)
        q_t = _dot_nt(eye, q, HIGHEST)
        o_rows = []
        for j in range(nb):
            s = st_ref[j, h]
            eg = eg_all[j:j + 1, B_HEADS + h:B_HEADS + h + 1]
            bt = beta[j:j + 1, h:h + 1]
            kc = k_t[:, j:j + 1]
            ks = jnp.sum(kc * s, axis=0, keepdims=True)
            qs = jnp.sum(q_t[:, j:j + 1] * s, axis=0, keepdims=True)
            u = bt * (v[j:j + 1, :] - eg * ks)
            o_rows.append(eg * qs + qk[j:j + 1, :] * u)
            sto_ref[j, h] = eg * s + kc * u
        o = jnp.concatenate(o_rows, axis=0)
        cs = slice(h * hd, (h + 1) * hd)
        o_ref[:, cs] = (_rms(o, onorm_ref[...]) * z_ref[:, cs].astype(F32)).astype(BF16)


def _gdn_step(qkv, cst, ba, z, state, cw, prm, onorm, *, nb):
    m, c3 = qkv.shape
    bw = c3 // 3
    hd = bw // B_HEADS
    row = lambda i: (i, 0)
    st = lambda i: (i, 0, 0, 0)
    return pl.pallas_call(
        functools.partial(_gdn_step_kernel, bw=bw),
        grid=(m // nb,),
        in_specs=[pl.BlockSpec((nb, c3), row), pl.BlockSpec((nb, 3 * c3), row),
                  pl.BlockSpec((nb, 128), row), pl.BlockSpec((nb, bw), row),
                  pl.BlockSpec((nb, B_HEADS, hd, hd), st), _const_spec(cw.shape),
                  _const_spec(prm.shape), _const_spec(onorm.shape)],
        out_specs=[pl.BlockSpec((nb, bw), row), pl.BlockSpec((nb, B_HEADS, hd, hd), st)],
        out_shape=[jax.ShapeDtypeStruct((m, bw), BF16),
                   jax.ShapeDtypeStruct((m, B_HEADS, hd, hd), F32)],
        compiler_params=_cparams(("parallel",)),
        name="gdn_step",
    )(qkv, cst, ba, z, state, cw, prm, onorm)


def _mix_out_kernel(*refs, n_in):
    ins = refs[:n_in]
    x_ref, w_ref, gpost_ref, x1_ref = refs[n_in:]
    mix = None
    r0 = 0
    for a_ref in ins:
        kk = a_ref.shape[1]
        part = _dot(a_ref[...], w_ref[r0:r0 + kk, :])
        mix = part if mix is None else mix + part
        r0 += kk
    x1_ref[...] = x_ref[...] + _rms(mix, gpost_ref[...])


def _mix_out(ins, x, w, gpost, *, tm):
    m, d = x.shape
    row = lambda i: (i, 0)
    return pl.pallas_call(
        functools.partial(_mix_out_kernel, n_in=len(ins)),
        grid=(m // tm,),
        in_specs=[pl.BlockSpec((tm, a.shape[1]), row) for a in ins]
        + [pl.BlockSpec((tm, d), row), _const_spec(w.shape), _const_spec(gpost.shape)],
        out_specs=pl.BlockSpec((tm, d), row),
        out_shape=jax.ShapeDtypeStruct((m, d), F32),
        compiler_params=_cparams(("parallel",)),
        name="mix_out",
    )(*ins, x, w, gpost)


def _ffn_kernel(x_ref, gpre_ref, wg_ref, wu_ref, wd_ref, wc_ref, bc_ref, gpost_ref, *rest,
                tiles_per_seq, fc):
    tm = x_ref.shape[0]
    ff = wg_ref.shape[1]
    if tiles_per_seq is None:
        hist_ref, x2_ref, gout_ref, a_ref = rest
    else:
        x2_ref, gout_ref, carry_ref, gbuf_ref, a_ref = rest

        @pl.when(pl.program_id(0) % tiles_per_seq == 0)
        def _():
            carry_ref[...] = jnp.zeros_like(carry_ref)

    hb = _rms(x_ref[...], gpre_ref[...]).astype(BF16)

    for c0 in range(0, ff, fc):
        cs = slice(c0, min(c0 + fc, ff))
        g = _dot(hb, wg_ref[:, cs])
        up = _dot(hb, wu_ref[:, cs])
        if tiles_per_seq is None:
            gout_ref[:, cs] = g
            sh2 = hist_ref[:, cs]
            sh1 = hist_ref[:, ff + cs.start:ff + cs.stop]
        else:
            gbuf_ref[0:8, cs] = carry_ref[:, cs]
            gbuf_ref[8:tm + 8, cs] = g
            sh2 = gbuf_ref[6:tm + 6, cs]
            sh1 = gbuf_ref[7:tm + 7, cs]
            tail = gbuf_ref[tm:tm + 8, cs]
            carry_ref[:, cs] = tail
            gout_ref[0, :, cs] = tail
        conv = (wc_ref[0:1, cs] * sh2 + wc_ref[1:2, cs] * sh1 + wc_ref[2:3, cs] * g
                + bc_ref[:, cs])
        a_ref[:, cs] = (_gelu(conv) * up).astype(BF16)
    x2_ref[...] = x_ref[...] + _rms(_dot(a_ref[...], wd_ref[...]), gpost_ref[...])


def _ffn(x1, gpre, wg, wu, wd, wc, bc, gpost, *, layer, tm, tiles_per_seq=None, hist=None, fc=256):
    m, d = x1.shape
    ff = wg.shape[2]
    row = lambda i: (i, 0)

    def layer_spec(w):
        return pl.BlockSpec((None,) + w.shape[1:], lambda i: (layer, 0, 0),
                            pipeline_mode=pl.Buffered(1))

    in_specs = [pl.BlockSpec((tm, d), row), _const_spec(gpre.shape), layer_spec(wg),
                layer_spec(wu), layer_spec(wd), _const_spec(wc.shape),
                _const_spec(bc.shape), _const_spec(gpost.shape)]
    args = [x1, gpre, wg, wu, wd, wc, bc, gpost]
    if tiles_per_seq is None:
        in_specs.append(pl.BlockSpec((tm, 2 * ff), row))
        args.append(hist)
        out_specs = [pl.BlockSpec((tm, d), row), pl.BlockSpec((tm, ff), row)]
        out_shape = [jax.ShapeDtypeStruct((m, d), F32), jax.ShapeDtypeStruct((m, ff), F32)]
        scratch = []
    else:
        out_specs = [pl.BlockSpec((tm, d), row), pl.BlockSpec((1, 8, ff), lambda i: (i, 0, 0))]
        out_shape = [jax.ShapeDtypeStruct((m, d), F32),
                     jax.ShapeDtypeStruct((m // tm, 8, ff), F32)]
        scratch = [pltpu.VMEM((8, ff), F32), pltpu.VMEM((tm + 8, ff), F32)]
    scratch.append(pltpu.VMEM((tm, ff), BF16))
    return pl.pallas_call(
        functools.partial(_ffn_kernel, tiles_per_seq=tiles_per_seq, fc=fc),
        grid=(m // tm,),
        in_specs=in_specs, out_specs=out_specs, out_shape=out_shape, scratch_shapes=scratch,
        compiler_params=_cparams(("arbitrary",)),
        name="conv_ffn",
    )(*args)


def _odd_in_kernel(x_ref, gpre_ref, w_ref, *outs, head_major, qscale):
    d = x_ref.shape[1]
    xb = _rms(x_ref[...], gpre_ref[...]).astype(BF16)
    q = _dot(xb, w_ref[:, 0:d]) * qscale
    k = _dot(xb, w_ref[:, d:2 * d])
    v = _dot(xb, w_ref[:, 2 * d:3 * d])
    if head_major:
        qh_ref, kh_ref, vt_ref, kt_ref, v_ref = outs
        tm = x_ref.shape[0]
        kt_ref[0] = k.T
        v_ref[...] = v
        vt = v.T
        hw = d // C_HEADS
        rowi = lax.broadcasted_iota(jnp.int32, (tm, hw), 0)
        lane = lax.broadcasted_iota(jnp.int32, (tm, hw), 1)
        pos = jnp.where((lane & 1) == 0, lax.shift_right_logical(rowi, 4), rowi & 15)
        feat = jnp.where(lane < 2 * POS_SPLITS, pos, 0).astype(F32).astype(BF16)
        ones_pad = jnp.where(lax.broadcasted_iota(jnp.int32, (16, tm), 0) == 0, 1.0, 0.0)
        for h in range(C_HEADS):
            cs = slice(h * hw, (h + 1) * hw)
            qh_ref[0, h] = q[:, cs].astype(BF16)
            kh_ref[0, h, :, 0:hw] = k[:, cs].astype(BF16)
            kh_ref[0, h, :, hw:2 * hw] = feat
            vt_ref[0, h, 0] = jnp.concatenate([vt[cs, :], ones_pad], axis=0).astype(BF16)
    else:
        q_ref, k_ref, v_ref = outs
        q_ref[...] = q
        k_ref[...] = k
        v_ref[...] = v


def _odd_in(x, gpre, w, *, tm, seq_len=None):
    m, d = x.shape
    qscale = (d // (2 * C_HEADS)) ** -0.5
    row = lambda i: (i, 0)
    head_major = seq_len is not None
    if head_major:
        assert tm <= 512
        qscale *= LOG2E
        tps = seq_len // tm
        hw = d // C_HEADS
        nb = m // seq_len
        hm = lambda i: (i // tps, 0, i % tps, 0)
        out_specs = [pl.BlockSpec((1, C_HEADS, tm, hw), hm),
                     pl.BlockSpec((1, C_HEADS, tm, 2 * hw), hm),
                     pl.BlockSpec((1, C_HEADS, 1, hw + 16, tm),
                                  lambda i: (i // tps, 0, i % tps, 0, 0)),
                     pl.BlockSpec((1, d, tm), lambda i: (i // tps, 0, i % tps)),
                     pl.BlockSpec((tm, d), row)]
        out_shape = [jax.ShapeDtypeStruct((nb, C_HEADS, seq_len, hw), BF16),
                     jax.ShapeDtypeStruct((nb, C_HEADS, seq_len, 2 * hw), BF16),
                     jax.ShapeDtypeStruct((nb, C_HEADS, tps, hw + 16, tm), BF16),
                     jax.ShapeDtypeStruct((nb, d, seq_len), F32),
                     jax.ShapeDtypeStruct((m, d), F32)]
    else:
        out_specs = [pl.BlockSpec((tm, d), row)] * 3
        out_shape = [jax.ShapeDtypeStruct((m, d), F32)] * 3
    return pl.pallas_call(
        functools.partial(_odd_in_kernel, head_major=head_major, qscale=qscale),
        grid=(m // tm,),
        in_specs=[pl.BlockSpec((tm, d), row), _const_spec(gpre.shape), _const_spec(w.shape)],
        out_specs=out_specs, out_shape=out_shape,
        compiler_params=_cparams(("parallel",)),
        name="odd_in",
    )(x, gpre, w)


def _lambda_value(lam_ref, lam_init):
    l = lam_ref[...]
    a = jnp.sum(l[0:1, :] * l[1:2, :], axis=-1, keepdims=True)
    b = jnp.sum(l[2:3, :] * l[3:4, :], axis=-1, keepdims=True)
    return jnp.exp(a) - jnp.exp(b) + lam_init


def _attn_prompt_kernel(slope2_ref, q_ref, k_ref, vt_ref, lam_ref, subln_ref, o_ref,
                        qs_ref, qn_ref, s_ref, m_ref, acc_ref, *, tq, tk, qb, lam_init,
                        side_work=None):
    h = pl.program_id(1)
    iq = pl.program_id(2)
    nq = pl.num_programs(2)
    hw = q_ref.shape[3]
    hd = hw // 2
    r = 2 * tq
    slope2 = slope2_ref[h]

    lane1 = lax.broadcasted_iota(jnp.int32, (1, hw), 1)
    rest = jnp.full((1, hw), slope2, F32)
    feat = jnp.zeros((1, hw), F32)
    for i in range(POS_SPLITS):
        piece = rest.astype(BF16).astype(F32)
        rest = rest - piece
        feat = jnp.where(lane1 == 2 * i, 16.0 * piece, jnp.where(lane1 == 2 * i + 1, piece, feat))
    feat = jnp.broadcast_to(feat, (r, hw)).astype(BF16)
    lane = lax.broadcasted_iota(jnp.int32, (tq, hw), 1)

    def stack_queries(dst_ref, tile):
        q = q_ref[0, 0, pl.ds(pl.multiple_of(tile * tq, tq), tq), :]
        dst_ref[0:tq, 0:hw] = jnp.where(lane < hd, q, jnp.zeros_like(q))
        dst_ref[tq:r, 0:hw] = jnp.where(lane >= hd, q, jnp.zeros_like(q))
        dst_ref[:, hw:2 * hw] = feat

    stack_queries(qs_ref, iq)
    stack_queries(qn_ref, jnp.minimum(iq + 1, nq - 1))
    m_ref[...] = jnp.full_like(m_ref, NEG_BIG)
    acc_ref[...] = jnp.zeros_like(acc_ref)

    def keys(kc):
        return k_ref[0, 0, pl.ds(pl.multiple_of(kc * tk, tk), tk), :]

    def step(kc, kb_next, qsrc_ref, masked, between=None):
        vtb = vt_ref[0, 0, kc]
        cshift = slope2 * (kc * tk).astype(F32)

        def take_and_refill(c0):
            kn = min(tk, (c0 % tq) + qb) if masked else tk
            s = s_ref[0:kn, c0:c0 + qb]
            s_ref[:, c0:c0 + qb] = _dot_nt(kb_next, qsrc_ref[c0:c0 + qb, :])
            return kn, s

        ahead = take_and_refill(0)
        for c0 in range(0, r, qb):
            if between is not None and c0 == r // 2:
                between()
            cols = slice(c0, c0 + qb)
            kn, s = ahead
            if c0 + qb < r:
                ahead = take_and_refill(c0 + qb)
            if masked:
                key_i = lax.broadcasted_iota(jnp.int32, (kn, qb), 0)
                qry_i = lax.broadcasted_iota(jnp.int32, (kn, qb), 1)
                s = jnp.where(key_i <= (c0 % tq) + qry_i, s, NEG_BIG)
            m_old = m_ref[:, cols]
            m_new = jnp.maximum(m_old, jnp.max(s, axis=0, keepdims=True) + cshift)
            p = jnp.exp2(s - (m_new - cshift)).astype(BF16)
            alpha = jnp.exp2(m_old - m_new)
            acc_ref[:, cols] = alpha * acc_ref[:, cols] + _dot(vtb[:, 0:kn], p)
            m_ref[:, cols] = m_new

    @pl.when(iq == 0)
    def _():
        s_ref[...] = _dot_nt(keys(0), qs_ref[...])

    done = 0
    for width in (8, 4, 2, 1):
        trips = lax.shift_right_logical(iq - done, width.bit_length() - 1)

        def body(j, carry, first=done, width=width):
            for u in range(width):
                kc = first + width * j + u
                step(kc, keys(kc + 1), qs_ref, False)
            return carry

        lax.fori_loop(0, trips, body, 0)
        done = done + width * trips
    before, between, after = side_work or (None, None, None)
    if before is not None:
        before()
    step(iq, keys(0), qn_ref, True, between)
    if after is not None:
        after()

    lam = _lambda_value(lam_ref, lam_init)
    o1 = acc_ref[0:hw, 0:tq] / acc_ref[hw:hw + 1, 0:tq]
    o2 = acc_ref[0:hw, tq:r] / acc_ref[hw:hw + 1, tq:r]
    att = (o1 - lam * o2).T
    o_ref[0] = (_rms(att, subln_ref[...]) * (1.0 - lam_init)).astype(BF16)


def _attn_prompt(slopes, qh, kh, vt, lam_p, subln, *, tq, lam_init):
    b, nh, t, hw = qh.shape
    tk = vt.shape[4]
    assert tq == tk
    return pl.pallas_call(
        functools.partial(_attn_prompt_kernel, tq=tq, tk=tk, qb=min(256, tq), lam_init=lam_init),
        grid=(b, nh, t // tq),
        in_specs=[pl.BlockSpec(memory_space=pltpu.SMEM),
                  pl.BlockSpec((1, 1, t, hw), lambda bi, h, iq: (bi, h, 0, 0)),
                  pl.BlockSpec((1, 1, t, 2 * hw), lambda bi, h, iq: (bi, h, 0, 0)),
                  pl.BlockSpec((1, 1, t // tk, hw + 16, tk), lambda bi, h, iq: (bi, h, 0, 0, 0)),
                  _const_spec(lam_p.shape), _const_spec(subln.shape)],
        out_specs=pl.BlockSpec((1, tq, hw), lambda bi, h, iq: (bi, iq, h)),
        out_shape=jax.ShapeDtypeStruct((b, t, nh * hw), BF16),
        scratch_shapes=[pltpu.VMEM((2 * tq, 2 * hw), BF16), pltpu.VMEM((2 * tq, 2 * hw), BF16),
                        pltpu.VMEM((tk, 2 * tq), F32), pltpu.VMEM((1, 2 * tq), F32),
                        pltpu.VMEM((hw + 16, 2 * tq), F32)],
        compiler_params=_cparams(("arbitrary", "arbitrary", "arbitrary")),
        name="attn_prompt",
    )(slopes * LOG2E, qh, kh, vt, lam_p, subln)


def _decode_stages(q_ref, kn_ref, vn_ref, slope_ref, expand_ref, lam_ref, subln_ref, k_refs,
                   v_refs, o_ref, m_ref, l_ref, acc_ref, *, j, live, n_past, lam_init):
    pg = len(k_refs)
    _, d, page = k_refs[0].shape
    nh = C_HEADS
    nr = 2 * nh
    hd = d // nr
    st = {}

    def scores():
        slope = slope_ref[:, 0:1]
        rowi = lax.broadcasted_iota(jnp.int32, (nr, d), 0)
        coli = lax.broadcasted_iota(jnp.int32, (nr, d), 1)
        lo = (jnp.where(rowi >= nh, rowi - nh, rowi) * 2 + jnp.where(rowi >= nh, 1, 0)) * hd
        qbd = jnp.where((coli >= lo) & (coli < lo + hd), q_ref[0], 0.0)
        qb = qbd.astype(BF16)
        tpos = lax.broadcasted_iota(jnp.int32, (1, page), 1)
        s_parts = []
        for p_i in range(pg):
            dist = (n_past - ((j * pg + p_i) * page + tpos)).astype(F32)
            s_parts.append(_dot(qb, k_refs[p_i][0].astype(BF16)) - slope * dist)
        s = jnp.where(live, jnp.concatenate(s_parts, axis=1), NEG_BIG)
        first = jnp.logical_and(j == 0, live)
        m_old = jnp.where(first, NEG_BIG, m_ref[...])
        m_new = jnp.maximum(m_old, jnp.max(s, axis=-1, keepdims=True))
        p = jnp.exp(s - m_new).astype(BF16)
        alpha = jnp.exp(m_old - m_new)
        l = alpha * jnp.where(first, 0.0, l_ref[...]) + jnp.sum(p.astype(F32), axis=-1,
                                                                  keepdims=True)
        st.update(qbd=qbd, p=p, alpha=alpha, first=first, m=m_new, l=l)
        m_ref[...] = m_new
        l_ref[...] = l

    def spread():
        rowe = lax.broadcasted_iota(jnp.int32, (nr, nh * page), 0)
        cole = lax.broadcasted_iota(jnp.int32, (nr, nh * page), 1)
        own_head = (cole & (nh - 1)) == jnp.where(rowe >= nh, rowe - nh, rowe)
        p = st["p"]
        stacked = jnp.concatenate([p[:, p_i * page:(p_i + 1) * page] for p_i in range(pg)], axis=0)
        wide = _dot(stacked, expand_ref[...])
        st["pbig"] = [jnp.where(own_head, wide[p_i * nr:(p_i + 1) * nr, :], 0.0).astype(BF16)
                      for p_i in range(pg)]

    def values():
        pv = [_dot(st["pbig"][p_i], v_refs[p_i][0].astype(BF16)) for p_i in range(pg)]
        acc = st["alpha"] * jnp.where(st["first"], 0.0, acc_ref[...]) + sum(pv[1:], pv[0])
        acc_ref[...] = acc
        s = jnp.sum(st["qbd"] * kn_ref[0], axis=-1, keepdims=True)
        m_new = jnp.maximum(st["m"], s)
        p = jnp.exp(s - m_new)
        alpha = jnp.exp(st["m"] - m_new)
        l = alpha * st["l"] + p
        vn = vn_ref[0]
        acc = alpha * acc + p * jnp.concatenate([vn, vn], axis=0)
        lam = _lambda_value(lam_ref, lam_init)
        att = acc[0:nh, :] / l[0:nh, :] - lam * (acc[nh:nr, :] / l[nh:nr, :])
        o_ref[0] = (_rms(att, subln_ref[...]) * (1.0 - lam_init)).astype(BF16)

    return scores, spread, values


def _decode_scratch_init(m_ref, l_ref, acc_ref):
    m_ref[...] = jnp.full_like(m_ref, NEG_BIG)
    l_ref[...] = jnp.zeros_like(l_ref)
    acc_ref[...] = jnp.zeros_like(acc_ref)


def _attn_decode_kernel(pt_ref, q_ref, kn_ref, vn_ref, slope_ref, expand_ref, lam_ref, subln_ref,
                        *rest, pg, n_past, lam_init):
    k_refs = rest[0:pg]
    v_refs = rest[pg:2 * pg]
    o_ref, m_ref, l_ref, acc_ref = rest[2 * pg:]

    @pl.when((pl.program_id(0) == 0) & (pl.program_id(1) == 0))
    def _():
        _decode_scratch_init(m_ref, l_ref, acc_ref)

    for stage in _decode_stages(q_ref, kn_ref, vn_ref, slope_ref, expand_ref, lam_ref, subln_ref,
                                k_refs, v_refs, o_ref, m_ref, l_ref, acc_ref,
                                j=pl.program_id(1), live=True, n_past=n_past, lam_init=lam_init):
        stage()


def _attn_fused_kernel(pt_ref, slope2_ref, q_ref, k_ref, vt_ref, lam_ref, subln_ref,
                       dq_ref, dkn_ref, dvn_ref, dslope_ref, expand_ref, *rest,
                       pg, steps_per_seq, n_dec, n_past, lam_init, **attn_kw):
    k_refs = rest[0:pg]
    v_refs = rest[pg:2 * pg]
    o_ref, do_ref = rest[2 * pg:2 * pg + 2]
    attn_scratch = rest[2 * pg + 2:-3]
    dm_ref, dl_ref, dacc_ref = rest[-3:]
    g = (pl.program_id(0) * pl.num_programs(1) + pl.program_id(1)) * pl.num_programs(2) \
        + pl.program_id(2)

    @pl.when(g == 0)
    def _():
        _decode_scratch_init(dm_ref, dl_ref, dacc_ref)

    side_work = _decode_stages(dq_ref, dkn_ref, dvn_ref, dslope_ref, expand_ref, lam_ref,
                               subln_ref, k_refs, v_refs, do_ref, dm_ref, dl_ref, dacc_ref,
                               j=jnp.minimum(g, n_dec - 1) % steps_per_seq, live=g < n_dec,
                               n_past=n_past, lam_init=lam_init)
    _attn_prompt_kernel(slope2_ref, q_ref, k_ref, vt_ref, lam_ref, subln_ref, o_ref,
                        *attn_scratch, lam_init=lam_init, side_work=side_work, **attn_kw)


def _attn_decode(page_table, q, kn, vn, slope_tile, lam_p, subln, cache_kt, cache_v2, *, pg,
                 lam_init):
    bs, n_pages = page_table.shape
    _, d, page = cache_kt.shape
    hw = d // C_HEADS
    q3, kn3 = (a.reshape(bs, 1, d) for a in (q, kn))
    vn3 = vn.reshape(bs, C_HEADS, hw)
    expand = (jnp.arange(page * C_HEADS)[None, :] // C_HEADS
              == jnp.arange(page)[:, None]).astype(BF16)
    row = lambda b, j, pt: (b, 0, 0)
    const2 = lambda b, j, pt: (0, 0)

    def page_map(p_i):
        return lambda b, j, pt: (pt[b, j * pg + p_i], 0, 0)

    k_specs = [pl.BlockSpec((1, d, page), page_map(p_i)) for p_i in range(pg)]
    v_specs = [pl.BlockSpec((1, page * C_HEADS, hw), page_map(p_i)) for p_i in range(pg)]
    grid_spec = pltpu.PrefetchScalarGridSpec(
        num_scalar_prefetch=1,
        grid=(bs, n_pages // pg),
        in_specs=[pl.BlockSpec((1, 1, d), row), pl.BlockSpec((1, 1, d), row),
                  pl.BlockSpec((1, C_HEADS, hw), row), pl.BlockSpec(slope_tile.shape, const2),
                  pl.BlockSpec(expand.shape, const2), pl.BlockSpec(lam_p.shape, const2),
                  pl.BlockSpec(subln.shape, const2)] + k_specs + v_specs,
        out_specs=pl.BlockSpec((1, C_HEADS, hw), row),
        scratch_shapes=[pltpu.VMEM((2 * C_HEADS, 1), F32), pltpu.VMEM((2 * C_HEADS, 1), F32),
                        pltpu.VMEM((2 * C_HEADS, hw), F32)],
    )
    out = pl.pallas_call(
        functools.partial(_attn_decode_kernel, pg=pg, n_past=n_pages * page, lam_init=lam_init),
        grid_spec=grid_spec,
        out_shape=jax.ShapeDtypeStruct((bs, C_HEADS, hw), BF16),
        compiler_params=_cparams(("arbitrary", "arbitrary")),
        name="attn_decode",
    )(page_table, q3, kn3, vn3, slope_tile, expand, lam_p, subln, *([cache_kt] * pg),
      *([cache_v2] * pg))
    return out.reshape(bs, d)


def _attn_fused(slopes, qh, kh, vt, lam_p, subln, page_table, q, kn, vn, slope_tile, cache_kt,
                cache_v2, *, tq, pg, lam_init):
    b, nh, t, hw = qh.shape
    tk = vt.shape[4]
    assert tq == tk
    nq = t // tq
    bs, n_pages = page_table.shape
    _, d, page = cache_kt.shape
    spp = n_pages // pg
    n_dec = bs * spp
    assert n_dec <= b * nh * nq
    q3, kn3 = (a.reshape(bs, 1, d) for a in (q, kn))
    vn3 = vn.reshape(bs, C_HEADS, hw)
    expand = (jnp.arange(page * C_HEADS)[None, :] // C_HEADS
              == jnp.arange(page)[:, None]).astype(BF16)

    def dec(bi, h, iq):
        ds = jnp.minimum((bi * nh + h) * nq + iq, n_dec - 1)
        return ds // spp, ds % spp

    seq_row = lambda bi, h, iq, pt: (dec(bi, h, iq)[0], 0, 0)
    const2 = lambda bi, h, iq, pt: (0, 0)
    head = lambda bi, h, iq, pt: (bi, h, 0, 0)

    def page_map(p_i):
        def index(bi, h, iq, pt):
            sb, j = dec(bi, h, iq)
            return pt[sb, j * pg + p_i], 0, 0
        return index

    k_specs = [pl.BlockSpec((1, d, page), page_map(p_i)) for p_i in range(pg)]
    v_specs = [pl.BlockSpec((1, page * C_HEADS, hw), page_map(p_i)) for p_i in range(pg)]
    grid_spec = pltpu.PrefetchScalarGridSpec(
        num_scalar_prefetch=1,
        grid=(b, nh, nq),
        in_specs=[pl.BlockSpec(memory_space=pltpu.SMEM),
                  pl.BlockSpec((1, 1, t, hw), head), pl.BlockSpec((1, 1, t, 2 * hw), head),
                  pl.BlockSpec((1, 1, t // tk, hw + 16, tk), lambda bi, h, iq, pt: (bi, h, 0, 0, 0)),
                  pl.BlockSpec(lam_p.shape, const2), pl.BlockSpec(subln.shape, const2),
                  pl.BlockSpec((1, 1, d), seq_row), pl.BlockSpec((1, 1, d), seq_row),
                  pl.BlockSpec((1, C_HEADS, hw), seq_row), pl.BlockSpec(slope_tile.shape, const2),
                  pl.BlockSpec(expand.shape, const2)] + k_specs + v_specs,
        out_specs=[pl.BlockSpec((1, tq, hw), lambda bi, h, iq, pt: (bi, iq, h)),
                   pl.BlockSpec((1, C_HEADS, hw), seq_row)],
        scratch_shapes=[pltpu.VMEM((2 * tq, 2 * hw), BF16), pltpu.VMEM((2 * tq, 2 * hw), BF16),
                        pltpu.VMEM((tk, 2 * tq), F32), pltpu.VMEM((1, 2 * tq), F32),
                        pltpu.VMEM((hw + 16, 2 * tq), F32),
                        pltpu.VMEM((2 * C_HEADS, 1), F32), pltpu.VMEM((2 * C_HEADS, 1), F32),
                        pltpu.VMEM((2 * C_HEADS, hw), F32)],
    )
    att, att_s = pl.pallas_call(
        functools.partial(_attn_fused_kernel, pg=pg, steps_per_seq=spp, n_dec=n_dec,
                          n_past=n_pages * page, lam_init=lam_init, tq=tq, tk=tk,
                          qb=min(256, tq)),
        grid_spec=grid_spec,
        out_shape=[jax.ShapeDtypeStruct((b, t, nh * hw), BF16),
                   jax.ShapeDtypeStruct((bs, C_HEADS, hw), BF16)],
        compiler_params=_cparams(("arbitrary", "arbitrary", "arbitrary")),
        name="attn_fused",
    )(page_table, slopes * LOG2E, qh, kh, vt, lam_p, subln, q3, kn3, vn3, slope_tile, expand,
      *([cache_kt] * pg), *([cache_v2] * pg))
    return att, att_s.reshape(bs, d)


def _row_tile(m, pref):
    tm = min(pref, m)
    assert m % tm == 0, (m, tm)
    return tm


def kernel(x_prompt, x_sample, cache_k, cache_v, page_table, state_gdn, state_gdn_conv, state_ffn_conv, norm_mix_pre, norm_mix_post, norm_ffn_pre, norm_ffn_post, w_in_even, a_v_norm, a_w_s, a_b_s, b_conv_w, b_a_log, b_dt_bias, b_out_norm, w_out_even, w_in_odd, c_lambda, c_subln, w_out_odd, w_ffn_gate, w_ffn_up, w_ffn_conv, b_ffn_conv, w_ffn_down):
    b, t, d = x_prompt.shape
    bs = x_sample.shape[0]
    assert x_sample.shape[1] == 1 and t % GDN_STEP == 0
    aw = a_v_norm.shape[-1]
    bw = b_conv_w.shape[-1] // 3
    ff = w_ffn_gate.shape[-1]
    n_split = 2 * aw + 4 * bw
    hw = d // C_HEADS
    _, n_pool, page, _, _, hd = cache_k.shape
    row2 = lambda a: a.reshape(1, -1)

    w_even = w_in_even[0].astype(BF16)
    w_ba = jnp.pad(w_in_even[0, :, n_split:], ((0, 0), (0, 128 - 2 * B_HEADS))).astype(BF16)
    prm = jnp.zeros((8, 128), F32)
    prm = prm.at[0, B_HEADS:2 * B_HEADS].set(b_dt_bias[0]).at[1, B_HEADS:2 * B_HEADS].set(b_a_log[0])
    onorm = row2(b_out_norm[0])
    w_oe = w_out_even[0].astype(BF16)
    w_odd = w_in_odd[0].astype(BF16)
    w_oo = w_out_odd[0].astype(BF16)
    wg = w_ffn_gate.astype(BF16)
    wu = w_ffn_up.astype(BF16)
    wd = w_ffn_down.astype(BF16)
    slopes = jnp.exp2(-8.0 * jnp.arange(1, C_HEADS + 1, dtype=F32) / C_HEADS)
    slope_tile = jnp.broadcast_to(jnp.tile(slopes, 2)[:, None], (2 * C_HEADS, 128))
    lam_init = 0.8 - 0.6 * math.exp(-0.3 * 1)
    ws_step = row2(jnp.repeat(a_w_s[0, :, 0, 0], aw // A_GROUPS))
    bs_step = row2(jnp.repeat(a_b_s[0, :, 0], aw // A_GROUPS))
    bs_t = a_b_s[0].T

    def ffn_layer(layer, x1, **kw):
        return _ffn(x1, row2(norm_ffn_pre[layer]), wg, wu, wd, w_ffn_conv[layer],
                    row2(b_ffn_conv[layer]), row2(norm_ffn_post[layer]), layer=layer, **kw)

    tm = _row_tile(b * t, 512)
    tps = t // tm
    xp = x_prompt.reshape(b * t, d)
    a_out, qkv, z, ba = _even_in(xp, row2(norm_mix_pre[0]), w_even, w_ba, row2(a_v_norm[0]),
                                 a_w_s[0], bs_t, chunk=A_CHUNK, tm=tm)
    qkv3 = qkv.reshape(b, t, 3 * bw)
    o, gdn_state_p = _gdn_prompt(qkv3, ba.reshape(b, t, 128), z.reshape(b, t, bw), b_conv_w[0],
                                 prm, onorm)
    gdn_conv_p = qkv3[:, t - (B_CONV - 1):, :]
    x1 = _mix_out([a_out, o.reshape(b * t, bw)], xp, w_oe, row2(norm_mix_post[0]), tm=tm)
    x2, gt0 = ffn_layer(0, x1, tm=tm, tiles_per_seq=tps)
    qh, kh, vt, kt_p, v_p = _odd_in(x2, row2(norm_mix_pre[1]), w_odd, tm=tm, seq_len=t)

    xs = x_sample.reshape(bs, d)
    a_out_s, v_s, qkv_s, z_s, ba_s = _even_in(xs, row2(norm_mix_pre[0]), w_even, w_ba,
                                              row2(a_v_norm[0]), ws_step, bs_step, chunk=1, tm=bs)
    o_s, gdn_state_s = _gdn_step(qkv_s, state_gdn_conv[0].reshape(bs, -1), ba_s, z_s, state_gdn[0],
                                 b_conv_w[0], prm, onorm, nb=min(16, bs))
    x1s = _mix_out([a_out_s, o_s], xs, w_oe, row2(norm_mix_post[0]), tm=bs)
    x2s, g0s = ffn_layer(0, x1s, tm=bs, hist=state_ffn_conv[0].reshape(bs, -1))
    q_s, k_s, v_sn = _odd_in(x2s, row2(norm_mix_pre[1]), w_odd, tm=bs)

    n_pages = page_table.shape[1]
    cache_kt = jnp.transpose(cache_k[0], (0, 2, 3, 4, 1)).reshape(n_pool, d, page)
    cache_v2 = cache_v[0].reshape(n_pool, page * C_HEADS, hw)
    lam_p, subln = c_lambda[0], row2(c_subln[0])
    ride = [c for c in (1, 2, 4, 8) if n_pages % c == 0
            and bs * (n_pages // c) <= b * C_HEADS * (t // tm)]
    if ride:
        att, att_s = _attn_fused(slopes, qh, kh, vt, lam_p, subln, page_table, q_s, k_s, v_sn,
                                 slope_tile, cache_kt, cache_v2, tq=tm, pg=ride[0],
                                 lam_init=lam_init)
    else:
        att = _attn_prompt(slopes, qh, kh, vt, lam_p, subln, tq=tm, lam_init=lam_init)
        pg = next(c for c in (16, 8, 4, 2, 1) if n_pages % c == 0)
        att_s = _attn_decode(page_table, q_s, k_s, v_sn, slope_tile, lam_p, subln, cache_kt,
                             cache_v2, pg=pg, lam_init=lam_init)

    x3 = _mix_out([att.reshape(b * t, d)], x2, w_oo, row2(norm_mix_post[1]), tm=tm)
    y_p, gt1 = ffn_layer(1, x3, tm=tm, tiles_per_seq=tps)
    ffn_conv_p = jnp.stack([g.reshape(b, tps, 8, ff)[:, -1, 8 - (FFN_CONV - 1):, :]
                            for g in (gt0, gt1)])
    x3s = _mix_out([att_s], x2s, w_oo, row2(norm_mix_post[1]), tm=bs)
    y_s, g1s = ffn_layer(1, x3s, tm=bs, hist=state_ffn_conv[1].reshape(bs, -1))
    ffn_conv_s = jnp.stack([jnp.concatenate([state_ffn_conv[l][:, 1:], g[:, None, :]], axis=1)
                            for l, g in ((0, g0s), (1, g1s))])

    return (y_p.reshape(b, t, d), y_s.reshape(bs, 1, d),
            gdn_state_p[None], gdn_state_s[None],
            gdn_conv_p[None],
            jnp.concatenate([state_gdn_conv[0][:, 1:], qkv_s[:, None, :]], axis=1)[None],
            v_s.reshape(1, bs, 1, aw),
            jnp.transpose(kt_p.reshape(1, b, C_HEADS, 2, hd, t), (0, 1, 5, 2, 3, 4)),
            v_p.reshape(1, b, t, C_HEADS, hw),
            k_s.reshape(1, bs, 1, C_HEADS, 2, hd), v_sn.reshape(1, bs, 1, C_HEADS, hw),
            ffn_conv_p, ffn_conv_s)
```

```python
import functools
import math

import jax
import jax.numpy as jnp
from jax import lax
from jax.experimental import pallas as pl
from jax.experimental.pallas import tpu as pltpu

F32 = jnp.float32
BF16 = jnp.bfloat16
EPS = 1e-6

A_GROUPS = 4
A_CHUNK = 128
B_HEADS = 4
B_CONV = 4
GDN_STEP = 128
GDN_CHUNK = 64
C_HEADS = 8
FFN_CONV = 3
NEG_BIG = -1e30
LOG2E = 1.4426950408889634
POS_SPLITS = 3

VMEM_LIMIT_BYTES = 56 * 1024 * 1024
HIGHEST = lax.Precision.HIGHEST


def _cparams(sem):
    return pltpu.CompilerParams(dimension_semantics=sem, vmem_limit_bytes=VMEM_LIMIT_BYTES)


def _gelu(x):
    return 0.5 * x * (1.0 + jnp.tanh(0.7978845608028654 * (x + 0.044715 * (x * x * x))))


def _sigmoid(x):
    return 1.0 / (1.0 + jnp.exp(-x))


def _softplus(x):
    return jnp.maximum(x, 0.0) + jnp.log(1.0 + jnp.exp(-jnp.abs(x)))


def _rms(x, gain):
    return x * lax.rsqrt(jnp.mean(x * x, axis=-1, keepdims=True) + EPS) * gain


def _dot(a, b, precision=None):
    return jnp.dot(a, b, preferred_element_type=F32, precision=precision)


def _dot_nt(a, b, precision=None):
    return lax.dot_general(a, b, (((1,), (1,)), ((), ())), preferred_element_type=F32,
                           precision=precision)


def _split(a):
    hi = a.astype(BF16)
    return hi, (a - hi.astype(F32)).astype(BF16)


def _dot3(a, b):
    (ah, al), (bh, bl) = a, b
    return _dot(ah, bh) + _dot(ah, bl) + _dot(al, bh)


def _const_spec(shape):
    nd = len(shape)
    return pl.BlockSpec(shape, lambda *_: (0,) * nd, pipeline_mode=pl.Buffered(1))


def _even_in_kernel(x_ref, gpre_ref, w_ref, wba_ref, avn_ref, ws_ref, bs_ref, *rest,
                    chunk, aw, bw):
    tm = x_ref.shape[0]
    if chunk == 1:
        aout_ref, v_ref, qkv_ref, z_ref, ba_ref = rest
    else:
        aout_ref, qkv_ref, z_ref, ba_ref = rest
    xb = _rms(x_ref[...], gpre_ref[...]).astype(BF16)
    u = _gelu(_dot(xb, w_ref[:, 0:aw]))
    v = _rms(_gelu(_dot(xb, w_ref[:, aw:2 * aw])), avn_ref[...])
    qkv_ref[...] = _dot(xb, w_ref[:, 2 * aw:2 * aw + 3 * bw])
    zz = _dot(xb, w_ref[:, 2 * aw + 3 * bw:2 * aw + 4 * bw])
    z_ref[...] = (zz * _sigmoid(zz)).astype(BF16)
    ba_ref[...] = _dot(xb, wba_ref[...])
    gd = aw // A_GROUPS
    if chunk == 1:
        v_ref[...] = v
        aout_ref[...] = (u * (v * ws_ref[...] + bs_ref[...])).astype(BF16)
    else:
        row = lax.broadcasted_iota(jnp.int32, (chunk, chunk), 0)
        col = lax.broadcasted_iota(jnp.int32, (chunk, chunk), 1)
        for g in range(A_GROUPS):
            wt = jnp.where(col <= row, ws_ref[g], 0.0).astype(BF16)
            bcol = bs_ref[:, g:g + 1]
            for n in range(tm // chunk):
                rs = slice(n * chunk, (n + 1) * chunk)
                cs = slice(g * gd, (g + 1) * gd)
                mixed = _dot(wt, v[rs, cs].astype(BF16)) + bcol
                aout_ref[rs, cs] = (u[rs, cs] * mixed).astype(BF16)


def _even_in(x, gpre, w_main, w_ba, avn, ws, bs, *, chunk, tm):
    m, d = x.shape
    aw = avn.shape[-1]
    bw = (w_main.shape[1] - 2 * aw - 2 * B_HEADS) // 4
    row = lambda i: (i, 0)
    in_specs = [pl.BlockSpec((tm, d), row), _const_spec(gpre.shape), _const_spec(w_main.shape),
                _const_spec(w_ba.shape), _const_spec(avn.shape), _const_spec(ws.shape),
                _const_spec(bs.shape)]
    args = [x, gpre, w_main, w_ba, avn, ws, bs]
    tail_specs = [pl.BlockSpec((tm, 3 * bw), row), pl.BlockSpec((tm, bw), row),
                  pl.BlockSpec((tm, 128), row)]
    tail_shape = [jax.ShapeDtypeStruct((m, 3 * bw), F32), jax.ShapeDtypeStruct((m, bw), BF16),
                  jax.ShapeDtypeStruct((m, 128), F32)]
    if chunk == 1:
        out_specs = [pl.BlockSpec((tm, aw), row), pl.BlockSpec((tm, aw), row)] + tail_specs
        out_shape = ([jax.ShapeDtypeStruct((m, aw), BF16), jax.ShapeDtypeStruct((m, aw), F32)]
                     + tail_shape)
    else:
        out_specs = [pl.BlockSpec((tm, aw), row)] + tail_specs
        out_shape = [jax.ShapeDtypeStruct((m, aw), BF16)] + tail_shape
    return pl.pallas_call(
        functools.partial(_even_in_kernel, chunk=chunk, aw=aw, bw=bw),
        grid=(m // tm,),
        in_specs=in_specs, out_specs=out_specs, out_shape=out_shape,
        compiler_params=_cparams(("parallel",)),
        name="even_in",
    )(*args)


def _gdn_prompt_kernel(qkv_ref, ba_ref, z_ref, cw_ref, prm_ref, onorm_ref,
                       o_ref, st_ref,
                       ext_ref, wm_ref, u0_ref, qe_ref, qk_ref, kwt_ref, egl_ref, *, bw):
    i = pl.program_id(1)
    n = GDN_STEP
    c = GDN_CHUNK
    hd = bw // B_HEADS

    @pl.when(i == 0)
    def _():
        st_ref[...] = jnp.zeros_like(st_ref)
        ext_ref[0:8, :] = jnp.zeros((8, ext_ref.shape[1]), F32)
        ext_ref[n:n + 8, :] = jnp.zeros((8, ext_ref.shape[1]), F32)
        wm_ref[...] = jnp.zeros_like(wm_ref)
        u0_ref[...] = jnp.zeros_like(u0_ref)
        qe_ref[...] = jnp.zeros_like(qe_ref)
        qk_ref[...] = jnp.zeros_like(qk_ref)
        kwt_ref[...] = jnp.zeros_like(kwt_ref)
        egl_ref[...] = jnp.zeros_like(egl_ref)

    heads = range(B_HEADS)
    zero_half = jnp.zeros((c, hd), F32)
    egl = [(egl_ref[0:1, B_HEADS + h:B_HEADS + h + 1], egl_ref[c:c + 1, B_HEADS + h:B_HEADS + h + 1])
           for h in heads]
    state = {"s": [st_ref[0, h] for h in heads]}

    def advance_stage_u(r0):
        state["sb"] = [s.astype(BF16) for s in state["s"]]
        state["u"] = [u0_ref[h, r0:r0 + c, :] - _dot(wm_ref[h, r0:r0 + c, :], state["sb"][h])
                      for h in heads]

    def advance_stage_s(r0, part):
        halves = [[u, zero_half] if part == 0 else [zero_half, u] for u in state["u"]]
        uf = [jnp.concatenate(hv, axis=0).astype(BF16) for hv in halves]
        o = [_dot(qe_ref[h, r0:r0 + c, :], state["sb"][h]) + _dot(qk_ref[h, r0:r0 + c, :], uf[h])
             for h in heads]
        state["s"] = [egl[h][part] * state["s"][h] + _dot(kwt_ref[h], uf[h]) for h in heads]
        return o

    ext_ref[5:8, :] = ext_ref[n + 5:n + 8, :]
    cur = qkv_ref[0]
    ext_ref[8:n + 8, :] = cur
    ext = ext_ref[...]
    y = cw_ref[B_CONV - 1:B_CONV, :] * cur
    for back in range(1, B_CONV):
        y = y + cw_ref[B_CONV - 1 - back:B_CONV - back, :] * pltpu.roll(ext, back, 0)[8:n + 8, :]
    y = y * _sigmoid(y)

    ba = ba_ref[0]
    beta = _sigmoid(ba)
    gfull = -jnp.exp(prm_ref[1:2, :]) * _softplus(ba + prm_ref[0:1, :])
    row = lax.broadcasted_iota(jnp.int32, (n, n), 0)
    col = lax.broadcasted_iota(jnp.int32, (n, n), 1)
    same = (row < c) == (col < c)
    incl = same & (col <= row)
    strict = same & (col < row)
    eye = jnp.where(row == col, 1.0, 0.0)

    advance_stage_u(0)
    g_hi, g_lo = _split(gfull)
    ones_incl = jnp.where(incl, 1.0, 0.0).astype(BF16)
    ones_same = jnp.where(same, 1.0, 0.0).astype(BF16)
    gcum = _dot(ones_incl, g_hi) + _dot(ones_incl, g_lo)
    glast = _dot(ones_same, g_hi) + _dot(ones_same, g_lo)
    gcum_t = gcum.T

    qs, ks, vs, kbs, decays = [], [], [], [], []
    for h in heads:
        q = y[:, h * hd:(h + 1) * hd]
        k = y[:, bw + h * hd:bw + (h + 1) * hd]
        qs.append(q * lax.rsqrt(jnp.sum(q * q, axis=-1, keepdims=True) + EPS) * (hd ** -0.5))
        ks.append(k * lax.rsqrt(jnp.sum(k * k, axis=-1, keepdims=True) + EPS))
        vs.append(y[:, 2 * bw + h * hd:2 * bw + (h + 1) * hd])
        kbs.append(ks[h].astype(BF16))
        gcol = gcum[:, B_HEADS + h:B_HEADS + h + 1]
        grow = gcum_t[B_HEADS + h:B_HEADS + h + 1, :]
        decays.append(jnp.where(incl, jnp.exp(jnp.where(incl, gcol - grow, 0.0)), 0.0))
    bcols = [beta[:, h:h + 1] for h in heads]
    gcols = [gcum[:, B_HEADS + h:B_HEADS + h + 1] for h in heads]
    glcols = [glast[:, B_HEADS + h:B_HEADS + h + 1] for h in heads]

    kk = [_dot_nt(kbs[h], kbs[h]) for h in heads]
    o_a = advance_stage_s(0, 0)
    x = [-jnp.where(strict, bcols[h] * decays[h] * kk[h], 0.0) for h in heads]
    p = [eye + x[h] for h in heads]
    xs = [_split(x[h]) for h in heads]
    o_b = None
    for it in range(int(math.log2(c)) - 1):
        xs = [_split(_dot3(xs[h], xs[h])) for h in heads]
        if it == 0:
            advance_stage_u(c)
        p = [p[h] + _dot3(_split(p[h]), xs[h]) for h in heads]
        if it == 1:
            o_b = advance_stage_s(c, 1)
    egs = [jnp.exp(gcols[h]) for h in heads]
    ps = [_split(p[h]) for h in heads]
    wm = [_dot3(ps[h], _split(bcols[h] * egs[h] * ks[h])) for h in heads]
    u0 = [_dot3(ps[h], _split(bcols[h] * vs[h])) for h in heads]
    qk = [_dot_nt(qs[h].astype(BF16), kbs[h]) for h in heads]

    for h in heads:
        st_ref[0, h] = state["s"][h]
        o = jnp.concatenate([o_a[h], o_b[h]], axis=0)
        cs = slice(h * hd, (h + 1) * hd)
        o_ref[0, :, cs] = (_rms(o, onorm_ref[...]) * z_ref[0, :, cs].astype(F32)).astype(BF16)
        wm_ref[h] = wm[h].astype(BF16)
        u0_ref[h] = u0[h]
        qe_ref[h] = (qs[h] * egs[h]).astype(BF16)
        qk_ref[h] = (qk[h] * decays[h]).astype(BF16)
        kwt_ref[h] = (ks[h] * jnp.exp(glcols[h] - gcols[h])).T.astype(BF16)
    egl_ref[...] = jnp.exp(glast)


def _gdn_prompt(qkv, ba, z, cw, prm, onorm):
    b, t, c3 = qkv.shape
    bw = c3 // 3
    hd = bw // B_HEADS
    n = GDN_STEP
    nt = t // n
    cur = lambda bi, i: (bi, jnp.minimum(i, nt - 1), 0)
    prev = lambda bi, i: (bi, jnp.maximum(i - 1, 0), 0)
    return pl.pallas_call(
        functools.partial(_gdn_prompt_kernel, bw=bw),
        grid=(b, nt + 1),
        in_specs=[pl.BlockSpec((1, n, c3), cur), pl.BlockSpec((1, n, 128), cur),
                  pl.BlockSpec((1, n, bw), prev), _const_spec(cw.shape), _const_spec(prm.shape),
                  _const_spec(onorm.shape)],
        out_specs=[pl.BlockSpec((1, n, bw), prev),
                   pl.BlockSpec((1, B_HEADS, hd, hd), lambda bi, i: (bi, 0, 0, 0))],
        out_shape=[jax.ShapeDtypeStruct((b, t, bw), BF16),
                   jax.ShapeDtypeStruct((b, B_HEADS, hd, hd), F32)],
        scratch_shapes=[pltpu.VMEM((n + 8, c3), F32),
                        pltpu.VMEM((B_HEADS, n, hd), BF16), pltpu.VMEM((B_HEADS, n, hd), F32),
                        pltpu.VMEM((B_HEADS, n, hd), BF16), pltpu.VMEM((B_HEADS, n, n), BF16),
                        pltpu.VMEM((B_HEADS, hd, n), BF16), pltpu.VMEM((n, 128), F32)],
        compiler_params=_cparams(("arbitrary", "arbitrary")),
        name="gdn_prompt",
    )(qkv, ba, z, cw, prm, onorm)


def _gdn_step_kernel(qkv_ref, cst_ref, ba_ref, z_ref, st_ref, cw_ref, prm_ref, onorm_ref,
                     o_ref, sto_ref, *, bw):
    nb = qkv_ref.shape[0]
    hd = bw // B_HEADS
    c3 = 3 * bw
    y = (cw_ref[0:1, :] * cst_ref[:, 0:c3] + cw_ref[1:2, :] * cst_ref[:, c3:2 * c3]
         + cw_ref[2:3, :] * cst_ref[:, 2 * c3:3 * c3] + cw_ref[3:4, :] * qkv_ref[...])
    y = y * _sigmoid(y)
    ba = ba_ref[...]
    beta = _sigmoid(ba)
    eg_all = jnp.exp(-jnp.exp(prm_ref[1:2, :]) * _softplus(ba + prm_ref[0:1, :]))
    row = lax.broadcasted_iota(jnp.int32, (hd, hd), 0)
    col = lax.broadcasted_iota(jnp.int32, (hd, hd), 1)
    eye = jnp.where(row == col, 1.0, 0.0)
    for h in range(B_HEADS):
        q = y[:, h * hd:(h + 1) * hd]
        k = y[:, bw + h * hd:bw + (h + 1) * hd]
        v = y[:, 2 * bw + h * hd:2 * bw + (h + 1) * hd]
        q = q * lax.rsqrt(jnp.sum(q * q, axis=-1, keepdims=True) + EPS) * (hd ** -0.5)
        k = k * lax.rsqrt(jnp.sum(k * k, axis=-1, keepdims=True) + EPS)
        qk = jnp.sum(q * k, axis=-1, keepdims=True)
        k_t = _dot_nt(eye, k, HIGHEST)
        q_t = _dot_nt(eye, q, HIGHEST)
        o_rows = []
        for j in range(nb):
            s = st_ref[j, h]
            eg = eg_all[j:j + 1, B_HEADS + h:B_HEADS + h + 1]
            bt = beta[j:j + 1, h:h + 1]
            kc = k_t[:, j:j + 1]
            ks = jnp.sum(kc * s, axis=0, keepdims=True)
            qs = jnp.sum(q_t[:, j:j + 1] * s, axis=0, keepdims=True)
            u = bt * (v[j:j + 1, :] - eg * ks)
            o_rows.append(eg * qs + qk[j:j + 1, :] * u)
            sto_ref[j, h] = eg * s + kc * u
        o = jnp.concatenate(o_rows, axis=0)
        cs = slice(h * hd, (h + 1) * hd)
        o_ref[:, cs] = (_rms(o, onorm_ref[...]) * z_ref[:, cs].astype(F32)).astype(BF16)


def _gdn_step(qkv, cst, ba, z, state, cw, prm, onorm, *, nb):
    m, c3 = qkv.shape
    bw = c3 // 3
    hd = bw // B_HEADS
    row = lambda i: (i, 0)
    st = lambda i: (i, 0, 0, 0)
    return pl.pallas_call(
        functools.partial(_gdn_step_kernel, bw=bw),
        grid=(m // nb,),
        in_specs=[pl.BlockSpec((nb, c3), row), pl.BlockSpec((nb, 3 * c3), row),
                  pl.BlockSpec((nb, 128), row), pl.BlockSpec((nb, bw), row),
                  pl.BlockSpec((nb, B_HEADS, hd, hd), st), _const_spec(cw.shape),
                  _const_spec(prm.shape), _const_spec(onorm.shape)],
        out_specs=[pl.BlockSpec((nb, bw), row), pl.BlockSpec((nb, B_HEADS, hd, hd), st)],
        out_shape=[jax.ShapeDtypeStruct((m, bw), BF16),
                   jax.ShapeDtypeStruct((m, B_HEADS, hd, hd), F32)],
        compiler_params=_cparams(("parallel",)),
        name="gdn_step",
    )(qkv, cst, ba, z, state, cw, prm, onorm)


def _mix_out_kernel(*refs, n_in):
    ins = refs[:n_in]
    x_ref, w_ref, gpost_ref, x1_ref = refs[n_in:]
    mix = None
    r0 = 0
    for a_ref in ins:
        kk = a_ref.shape[1]
        part = _dot(a_ref[...], w_ref[r0:r0 + kk, :])
        mix = part if mix is None else mix + part
        r0 += kk
    x1_ref[...] = x_ref[...] + _rms(mix, gpost_ref[...])


def _mix_out(ins, x, w, gpost, *, tm):
    m, d = x.shape
    row = lambda i: (i, 0)
    return pl.pallas_call(
        functools.partial(_mix_out_kernel, n_in=len(ins)),
        grid=(m // tm,),
        in_specs=[pl.BlockSpec((tm, a.shape[1]), row) for a in ins]
        + [pl.BlockSpec((tm, d), row), _const_spec(w.shape), _const_spec(gpost.shape)],
        out_specs=pl.BlockSpec((tm, d), row),
        out_shape=jax.ShapeDtypeStruct((m, d), F32),
        compiler_params=_cparams(("parallel",)),
        name="mix_out",
    )(*ins, x, w, gpost)


def _ffn_kernel(x_ref, gpre_ref, wg_ref, wu_ref, wd_ref, wc_ref, bc_ref, gpost_ref, *rest,
                tiles_per_seq, fc):
    tm = x_ref.shape[0]
    ff = wg_ref.shape[1]
    if tiles_per_seq is None:
        hist_ref, x2_ref, gout_ref, a_ref = rest
    else:
        x2_ref, gout_ref, carry_ref, gbuf_ref, a_ref = rest

        @pl.when(pl.program_id(0) % tiles_per_seq == 0)
        def _():
            carry_ref[...] = jnp.zeros_like(carry_ref)

    hb = _rms(x_ref[...], gpre_ref[...]).astype(BF16)

    for c0 in range(0, ff, fc):
        cs = slice(c0, min(c0 + fc, ff))
        g = _dot(hb, wg_ref[:, cs])
        up = _dot(hb, wu_ref[:, cs])
        if tiles_per_seq is None:
            gout_ref[:, cs] = g
            sh2 = hist_ref[:, cs]
            sh1 = hist_ref[:, ff + cs.start:ff + cs.stop]
        else:
            gbuf_ref[0:8, cs] = carry_ref[:, cs]
            gbuf_ref[8:tm + 8, cs] = g
            staged = gbuf_ref[:, cs]
            sh2 = pltpu.roll(staged, 2, 0)[8:tm + 8, :]
            sh1 = pltpu.roll(staged, 1, 0)[8:tm + 8, :]
            tail = gbuf_ref[tm:tm + 8, cs]
            carry_ref[:, cs] = tail
            gout_ref[0, :, cs] = tail
        conv = (wc_ref[0:1, cs] * sh2 + wc_ref[1:2, cs] * sh1 + wc_ref[2:3, cs] * g
                + bc_ref[:, cs])
        a_ref[:, cs] = (_gelu(conv) * up).astype(BF16)
    x2_ref[...] = x_ref[...] + _rms(_dot(a_ref[...], wd_ref[...]), gpost_ref[...])


def _ffn(x1, gpre, wg, wu, wd, wc, bc, gpost, *, layer, tm, tiles_per_seq=None, hist=None, fc=256):
    m, d = x1.shape
    ff = wg.shape[2]
    row = lambda i: (i, 0)

    def layer_spec(w):
        return pl.BlockSpec((None,) + w.shape[1:], lambda i: (layer, 0, 0),
                            pipeline_mode=pl.Buffered(1))

    in_specs = [pl.BlockSpec((tm, d), row), _const_spec(gpre.shape), layer_spec(wg),
                layer_spec(wu), layer_spec(wd), _const_spec(wc.shape),
                _const_spec(bc.shape), _const_spec(gpost.shape)]
    args = [x1, gpre, wg, wu, wd, wc, bc, gpost]
    if tiles_per_seq is None:
        in_specs.append(pl.BlockSpec((tm, 2 * ff), row))
        args.append(hist)
        out_specs = [pl.BlockSpec((tm, d), row), pl.BlockSpec((tm, ff), row)]
        out_shape = [jax.ShapeDtypeStruct((m, d), F32), jax.ShapeDtypeStruct((m, ff), F32)]
        scratch = []
    else:
        out_specs = [pl.BlockSpec((tm, d), row), pl.BlockSpec((1, 8, ff), lambda i: (i, 0, 0))]
        out_shape = [jax.ShapeDtypeStruct((m, d), F32),
                     jax.ShapeDtypeStruct((m // tm, 8, ff), F32)]
        scratch = [pltpu.VMEM((8, ff), F32), pltpu.VMEM((tm + 8, ff), F32)]
    scratch.append(pltpu.VMEM((tm, ff), BF16))
    return pl.pallas_call(
        functools.partial(_ffn_kernel, tiles_per_seq=tiles_per_seq, fc=fc),
        grid=(m // tm,),
        in_specs=in_specs, out_specs=out_specs, out_shape=out_shape, scratch_shapes=scratch,
        compiler_params=_cparams(("arbitrary",)),
        name="conv_ffn",
    )(*args)


def _odd_in_kernel(x_ref, gpre_ref, w_ref, *outs, head_major, qscale):
    d = x_ref.shape[1]
    xb = _rms(x_ref[...], gpre_ref[...]).astype(BF16)
    q = _dot(xb, w_ref[:, 0:d]) * qscale
    k = _dot(xb, w_ref[:, d:2 * d])
    v = _dot(xb, w_ref[:, 2 * d:3 * d])
    if head_major:
        qh_ref, kh_ref, vt_ref, kt_ref, v_ref = outs
        tm = x_ref.shape[0]
        kt_ref[0] = k.T
        v_ref[...] = v
        vt = v.T
        hw = d // C_HEADS
        rowi = lax.broadcasted_iota(jnp.int32, (tm, hw), 0)
        lane = lax.broadcasted_iota(jnp.int32, (tm, hw), 1)
        pos = jnp.where((lane & 1) == 0, lax.shift_right_logical(rowi, 4), rowi & 15)
        feat = jnp.where(lane < 2 * POS_SPLITS, pos, 0).astype(F32).astype(BF16)
        ones_pad = jnp.where(lax.broadcasted_iota(jnp.int32, (16, tm), 0) == 0, 1.0, 0.0)
        for h in range(C_HEADS):
            cs = slice(h * hw, (h + 1) * hw)
            qh_ref[0, h] = q[:, cs].astype(BF16)
            kh_ref[0, h, :, 0:hw] = k[:, cs].astype(BF16)
            kh_ref[0, h, :, hw:2 * hw] = feat
            vt_ref[0, h, 0] = jnp.concatenate([vt[cs, :], ones_pad], axis=0).astype(BF16)
    else:
        q_ref, k_ref, v_ref = outs
        q_ref[...] = q
        k_ref[...] = k
        v_ref[...] = v


def _odd_in(x, gpre, w, *, tm, seq_len=None):
    m, d = x.shape
    qscale = (d // (2 * C_HEADS)) ** -0.5
    row = lambda i: (i, 0)
    head_major = seq_len is not None
    if head_major:
        assert tm <= 512
        qscale *= LOG2E
        tps = seq_len // tm
        hw = d // C_HEADS
        nb = m // seq_len
        hm = lambda i: (i // tps, 0, i % tps, 0)
        out_specs = [pl.BlockSpec((1, C_HEADS, tm, hw), hm),
                     pl.BlockSpec((1, C_HEADS, tm, 2 * hw), hm),
                     pl.BlockSpec((1, C_HEADS, 1, hw + 16, tm),
                                  lambda i: (i // tps, 0, i % tps, 0, 0)),
                     pl.BlockSpec((1, d, tm), lambda i: (i // tps, 0, i % tps)),
                     pl.BlockSpec((tm, d), row)]
        out_shape = [jax.ShapeDtypeStruct((nb, C_HEADS, seq_len, hw), BF16),
                     jax.ShapeDtypeStruct((nb, C_HEADS, seq_len, 2 * hw), BF16),
                     jax.ShapeDtypeStruct((nb, C_HEADS, tps, hw + 16, tm), BF16),
                     jax.ShapeDtypeStruct((nb, d, seq_len), F32),
                     jax.ShapeDtypeStruct((m, d), F32)]
    else:
        out_specs = [pl.BlockSpec((tm, d), row)] * 3
        out_shape = [jax.ShapeDtypeStruct((m, d), F32)] * 3
    return pl.pallas_call(
        functools.partial(_odd_in_kernel, head_major=head_major, qscale=qscale),
        grid=(m // tm,),
        in_specs=[pl.BlockSpec((tm, d), row), _const_spec(gpre.shape), _const_spec(w.shape)],
        out_specs=out_specs, out_shape=out_shape,
        compiler_params=_cparams(("parallel",)),
        name="odd_in",
    )(x, gpre, w)


def _lambda_value(lam_ref, lam_init):
    l = lam_ref[...]
    a = jnp.sum(l[0:1, :] * l[1:2, :], axis=-1, keepdims=True)
    b = jnp.sum(l[2:3, :] * l[3:4, :], axis=-1, keepdims=True)
    return jnp.exp(a) - jnp.exp(b) + lam_init


def _attn_prompt_kernel(slope2_ref, q_ref, k_ref, vt_ref, lam_ref, subln_ref, o_ref,
                        qs_ref, qn_ref, s_ref, m_ref, acc_ref, *, tq, tk, qb, lam_init,
                        side_work=None):
    h = pl.program_id(1)
    iq = pl.program_id(2)
    nq = pl.num_programs(2)
    hw = q_ref.shape[3]
    hd = hw // 2
    r = 2 * tq
    slope2 = slope2_ref[h]

    lane1 = lax.broadcasted_iota(jnp.int32, (1, hw), 1)
    rest = jnp.full((1, hw), slope2, F32)
    feat = jnp.zeros((1, hw), F32)
    for i in range(POS_SPLITS):
        piece = rest.astype(BF16).astype(F32)
        rest = rest - piece
        feat = jnp.where(lane1 == 2 * i, 16.0 * piece, jnp.where(lane1 == 2 * i + 1, piece, feat))
    feat = jnp.broadcast_to(feat, (r, hw)).astype(BF16)
    lane = lax.broadcasted_iota(jnp.int32, (tq, hw), 1)

    def stack_queries(dst_ref, tile):
        q = q_ref[0, 0, pl.ds(pl.multiple_of(tile * tq, tq), tq), :]
        dst_ref[0:tq, 0:hw] = jnp.where(lane < hd, q, jnp.zeros_like(q))
        dst_ref[tq:r, 0:hw] = jnp.where(lane >= hd, q, jnp.zeros_like(q))
        dst_ref[:, hw:2 * hw] = feat

    stack_queries(qs_ref, iq)
    stack_queries(qn_ref, jnp.minimum(iq + 1, nq - 1))
    m_ref[...] = jnp.full_like(m_ref, NEG_BIG)
    acc_ref[...] = jnp.zeros_like(acc_ref)

    def keys(kc):
        return k_ref[0, 0, pl.ds(pl.multiple_of(kc * tk, tk), tk), :]

    def step(kc, kb_next, qsrc_ref, masked, between=None):
        vtb = vt_ref[0, 0, kc]
        cshift = slope2 * (kc * tk).astype(F32)

        def take_and_refill(c0):
            kn = min(tk, (c0 % tq) + qb) if masked else tk
            s = s_ref[0:kn, c0:c0 + qb]
            s_ref[:, c0:c0 + qb] = _dot_nt(kb_next, qsrc_ref[c0:c0 + qb, :])
            return kn, s

        ahead = take_and_refill(0)
        for c0 in range(0, r, qb):
            if between is not None and c0 == r // 2:
                between()
            cols = slice(c0, c0 + qb)
            kn, s = ahead
            if c0 + qb < r:
                ahead = take_and_refill(c0 + qb)
            if masked:
                key_i = lax.broadcasted_iota(jnp.int32, (kn, qb), 0)
                qry_i = lax.broadcasted_iota(jnp.int32, (kn, qb), 1)
                s = jnp.where(key_i <= (c0 % tq) + qry_i, s, NEG_BIG)
            m_old = m_ref[:, cols]
            m_new = jnp.maximum(m_old, jnp.max(s, axis=0, keepdims=True) + cshift)
            p = jnp.exp2(s - (m_new - cshift)).astype(BF16)
            alpha = jnp.exp2(m_old - m_new)
            acc_ref[:, cols] = alpha * acc_ref[:, cols] + _dot(vtb[:, 0:kn], p)
            m_ref[:, cols] = m_new

    @pl.when(iq == 0)
    def _():
        s_ref[...] = _dot_nt(keys(0), qs_ref[...])

    done = 0
    for width in (8, 4, 2, 1):
        trips = lax.shift_right_logical(iq - done, width.bit_length() - 1)

        def body(j, carry, first=done, width=width):
            for u in range(width):
                kc = first + width * j + u
                step(kc, keys(kc + 1), qs_ref, False)
            return carry

        lax.fori_loop(0, trips, body, 0)
        done = done + width * trips
    before, between, after = side_work or (None, None, None)
    if before is not None:
        before()
    step(iq, keys(0), qn_ref, True, between)
    if after is not None:
        after()

    lam = _lambda_value(lam_ref, lam_init)
    o1 = acc_ref[0:hw, 0:tq] / acc_ref[hw:hw + 1, 0:tq]
    o2 = acc_ref[0:hw, tq:r] / acc_ref[hw:hw + 1, tq:r]
    att = (o1 - lam * o2).T
    o_ref[0] = (_rms(att, subln_ref[...]) * (1.0 - lam_init)).astype(BF16)


def _attn_prompt(slopes, qh, kh, vt, lam_p, subln, *, tq, lam_init):
    b, nh, t, hw = qh.shape
    tk = vt.shape[4]
    assert tq == tk
    return pl.pallas_call(
        functools.partial(_attn_prompt_kernel, tq=tq, tk=tk, qb=min(256, tq), lam_init=lam_init),
        grid=(b, nh, t // tq),
        in_specs=[pl.BlockSpec(memory_space=pltpu.SMEM),
                  pl.BlockSpec((1, 1, t, hw), lambda bi, h, iq: (bi, h, 0, 0)),
                  pl.BlockSpec((1, 1, t, 2 * hw), lambda bi, h, iq: (bi, h, 0, 0)),
                  pl.BlockSpec((1, 1, t // tk, hw + 16, tk), lambda bi, h, iq: (bi, h, 0, 0, 0)),
                  _const_spec(lam_p.shape), _const_spec(subln.shape)],
        out_specs=pl.BlockSpec((1, tq, hw), lambda bi, h, iq: (bi, iq, h)),
        out_shape=jax.ShapeDtypeStruct((b, t, nh * hw), BF16),
        scratch_shapes=[pltpu.VMEM((2 * tq, 2 * hw), BF16), pltpu.VMEM((2 * tq, 2 * hw), BF16),
                        pltpu.VMEM((tk, 2 * tq), F32), pltpu.VMEM((1, 2 * tq), F32),
                        pltpu.VMEM((hw + 16, 2 * tq), F32)],
        compiler_params=_cparams(("arbitrary", "arbitrary", "arbitrary")),
        name="attn_prompt",
    )(slopes * LOG2E, qh, kh, vt, lam_p, subln)


def _decode_stages(q_ref, kn_ref, vn_ref, slope_ref, expand_ref, lam_ref, subln_ref, k_refs,
                   v_refs, o_ref, m_ref, l_ref, acc_ref, *, j, live, n_past, lam_init):
    pg = len(k_refs)
    _, d, page = k_refs[0].shape
    nh = C_HEADS
    nr = 2 * nh
    hd = d // nr
    st = {}

    def scores():
        slope = slope_ref[:, 0:1]
        rowi = lax.broadcasted_iota(jnp.int32, (nr, d), 0)
        coli = lax.broadcasted_iota(jnp.int32, (nr, d), 1)
        lo = (jnp.where(rowi >= nh, rowi - nh, rowi) * 2 + jnp.where(rowi >= nh, 1, 0)) * hd
        qbd = jnp.where((coli >= lo) & (coli < lo + hd), q_ref[0], 0.0)
        qb = qbd.astype(BF16)
        tpos = lax.broadcasted_iota(jnp.int32, (1, page), 1)
        s_parts = []
        for p_i in range(pg):
            dist = (n_past - ((j * pg + p_i) * page + tpos)).astype(F32)
            s_parts.append(_dot(qb, k_refs[p_i][0].astype(BF16)) - slope * dist)
        s = jnp.where(live, jnp.concatenate(s_parts, axis=1), NEG_BIG)
        first = jnp.logical_and(j == 0, live)
        m_old = jnp.where(first, NEG_BIG, m_ref[...])
        m_new = jnp.maximum(m_old, jnp.max(s, axis=-1, keepdims=True))
        p = jnp.exp(s - m_new).astype(BF16)
        alpha = jnp.exp(m_old - m_new)
        l = alpha * jnp.where(first, 0.0, l_ref[...]) + jnp.sum(p.astype(F32), axis=-1,
                                                                  keepdims=True)
        st.update(qbd=qbd, p=p, alpha=alpha, first=first, m=m_new, l=l)
        m_ref[...] = m_new
        l_ref[...] = l

    def spread():
        rowe = lax.broadcasted_iota(jnp.int32, (nr, nh * page), 0)
        cole = lax.broadcasted_iota(jnp.int32, (nr, nh * page), 1)
        own_head = (cole & (nh - 1)) == jnp.where(rowe >= nh, rowe - nh, rowe)
        p = st["p"]
        stacked = jnp.concatenate([p[:, p_i * page:(p_i + 1) * page] for p_i in range(pg)], axis=0)
        wide = _dot(stacked, expand_ref[...])
        st["pbig"] = [jnp.where(own_head, wide[p_i * nr:(p_i + 1) * nr, :], 0.0).astype(BF16)
                      for p_i in range(pg)]

    def values():
        pv = [_dot(st["pbig"][p_i], v_refs[p_i][0].astype(BF16)) for p_i in range(pg)]
        acc = st["alpha"] * jnp.where(st["first"], 0.0, acc_ref[...]) + sum(pv[1:], pv[0])
        acc_ref[...] = acc
        s = jnp.sum(st["qbd"] * kn_ref[0], axis=-1, keepdims=True)
        m_new = jnp.maximum(st["m"], s)
        p = jnp.exp(s - m_new)
        alpha = jnp.exp(st["m"] - m_new)
        l = alpha * st["l"] + p
        vn = vn_ref[0]
        acc = alpha * acc + p * jnp.concatenate([vn, vn], axis=0)
        lam = _lambda_value(lam_ref, lam_init)
        att = acc[0:nh, :] / l[0:nh, :] - lam * (acc[nh:nr, :] / l[nh:nr, :])
        o_ref[0] = (_rms(att, subln_ref[...]) * (1.0 - lam_init)).astype(BF16)

    return scores, spread, values


def _decode_scratch_init(m_ref, l_ref, acc_ref):
    m_ref[...] = jnp.full_like(m_ref, NEG_BIG)
    l_ref[...] = jnp.zeros_like(l_ref)
    acc_ref[...] = jnp.zeros_like(acc_ref)


def _attn_decode_kernel(pt_ref, q_ref, kn_ref, vn_ref, slope_ref, expand_ref, lam_ref, subln_ref,
                        *rest, pg, n_past, lam_init):
    k_refs = rest[0:pg]
    v_refs = rest[pg:2 * pg]
    o_ref, m_ref, l_ref, acc_ref = rest[2 * pg:]

    @pl.when((pl.program_id(0) == 0) & (pl.program_id(1) == 0))
    def _():
        _decode_scratch_init(m_ref, l_ref, acc_ref)

    for stage in _decode_stages(q_ref, kn_ref, vn_ref, slope_ref, expand_ref, lam_ref, subln_ref,
                                k_refs, v_refs, o_ref, m_ref, l_ref, acc_ref,
                                j=pl.program_id(1), live=True, n_past=n_past, lam_init=lam_init):
        stage()


def _attn_fused_kernel(pt_ref, slope2_ref, q_ref, k_ref, vt_ref, lam_ref, subln_ref,
                       dq_ref, dkn_ref, dvn_ref, dslope_ref, expand_ref, *rest,
                       pg, steps_per_seq, n_dec, n_past, lam_init, **attn_kw):
    k_refs = rest[0:pg]
    v_refs = rest[pg:2 * pg]
    o_ref, do_ref = rest[2 * pg:2 * pg + 2]
    attn_scratch = rest[2 * pg + 2:-3]
    dm_ref, dl_ref, dacc_ref = rest[-3:]
    g = (pl.program_id(0) * pl.num_programs(1) + pl.program_id(1)) * pl.num_programs(2) \
        + pl.program_id(2)

    @pl.when(g == 0)
    def _():
        _decode_scratch_init(dm_ref, dl_ref, dacc_ref)

    side_work = _decode_stages(dq_ref, dkn_ref, dvn_ref, dslope_ref, expand_ref, lam_ref,
                               subln_ref, k_refs, v_refs, do_ref, dm_ref, dl_ref, dacc_ref,
                               j=jnp.minimum(g, n_dec - 1) % steps_per_seq, live=g < n_dec,
                               n_past=n_past, lam_init=lam_init)
    _attn_prompt_kernel(slope2_ref, q_ref, k_ref, vt_ref, lam_ref, subln_ref, o_ref,
                        *attn_scratch, lam_init=lam_init, side_work=side_work, **attn_kw)


def _attn_decode(page_table, q, kn, vn, slope_tile, lam_p, subln, cache_kt, cache_v2, *, pg,
                 lam_init):
    bs, n_pages = page_table.shape
    _, d, page = cache_kt.shape
    hw = d // C_HEADS
    q3, kn3 = (a.reshape(bs, 1, d) for a in (q, kn))
    vn3 = vn.reshape(bs, C_HEADS, hw)
    expand = (jnp.arange(page * C_HEADS)[None, :] // C_HEADS
              == jnp.arange(page)[:, None]).astype(BF16)
    row = lambda b, j, pt: (b, 0, 0)
    const2 = lambda b, j, pt: (0, 0)

    def page_map(p_i):
        return lambda b, j, pt: (pt[b, j * pg + p_i], 0, 0)

    k_specs = [pl.BlockSpec((1, d, page), page_map(p_i)) for p_i in range(pg)]
    v_specs = [pl.BlockSpec((1, page * C_HEADS, hw), page_map(p_i)) for p_i in range(pg)]
    grid_spec = pltpu.PrefetchScalarGridSpec(
        num_scalar_prefetch=1,
        grid=(bs, n_pages // pg),
        in_specs=[pl.BlockSpec((1, 1, d), row), pl.BlockSpec((1, 1, d), row),
                  pl.BlockSpec((1, C_HEADS, hw), row), pl.BlockSpec(slope_tile.shape, const2),
                  pl.BlockSpec(expand.shape, const2), pl.BlockSpec(lam_p.shape, const2),
                  pl.BlockSpec(subln.shape, const2)] + k_specs + v_specs,
        out_specs=pl.BlockSpec((1, C_HEADS, hw), row),
        scratch_shapes=[pltpu.VMEM((2 * C_HEADS, 1), F32), pltpu.VMEM((2 * C_HEADS, 1), F32),
                        pltpu.VMEM((2 * C_HEADS, hw), F32)],
    )
    out = pl.pallas_call(
        functools.partial(_attn_decode_kernel, pg=pg, n_past=n_pages * page, lam_init=lam_init),
        grid_spec=grid_spec,
        out_shape=jax.ShapeDtypeStruct((bs, C_HEADS, hw), BF16),
        compiler_params=_cparams(("arbitrary", "arbitrary")),
        name="attn_decode",
    )(page_table, q3, kn3, vn3, slope_tile, expand, lam_p, subln, *([cache_kt] * pg),
      *([cache_v2] * pg))
    return out.reshape(bs, d)


def _attn_fused(slopes, qh, kh, vt, lam_p, subln, page_table, q, kn, vn, slope_tile, cache_kt,
                cache_v2, *, tq, pg, lam_init):
    b, nh, t, hw = qh.shape
    tk = vt.shape[4]
    assert tq == tk
    nq = t // tq
    bs, n_pages = page_table.shape
    _, d, page = cache_kt.shape
    spp = n_pages // pg
    n_dec = bs * spp
    assert n_dec <= b * nh * nq
    q3, kn3 = (a.reshape(bs, 1, d) for a in (q, kn))
    vn3 = vn.reshape(bs, C_HEADS, hw)
    expand = (jnp.arange(page * C_HEADS)[None, :] // C_HEADS
              == jnp.arange(page)[:, None]).astype(BF16)

    def dec(bi, h, iq):
        ds = jnp.minimum((bi * nh + h) * nq + iq, n_dec - 1)
        return ds // spp, ds % spp

    seq_row = lambda bi, h, iq, pt: (dec(bi, h, iq)[0], 0, 0)
    const2 = lambda bi, h, iq, pt: (0, 0)
    head = lambda bi, h, iq, pt: (bi, h, 0, 0)

    def page_map(p_i):
        def index(bi, h, iq, pt):
            sb, j = dec(bi, h, iq)
            return pt[sb, j * pg + p_i], 0, 0
        return index

    k_specs = [pl.BlockSpec((1, d, page), page_map(p_i)) for p_i in range(pg)]
    v_specs = [pl.BlockSpec((1, page * C_HEADS, hw), page_map(p_i)) for p_i in range(pg)]
    grid_spec = pltpu.PrefetchScalarGridSpec(
        num_scalar_prefetch=1,
        grid=(b, nh, nq),
        in_specs=[pl.BlockSpec(memory_space=pltpu.SMEM),
                  pl.BlockSpec((1, 1, t, hw), head), pl.BlockSpec((1, 1, t, 2 * hw), head),
                  pl.BlockSpec((1, 1, t // tk, hw + 16, tk), lambda bi, h, iq, pt: (bi, h, 0, 0, 0)),
                  pl.BlockSpec(lam_p.shape, const2), pl.BlockSpec(subln.shape, const2),
                  pl.BlockSpec((1, 1, d), seq_row), pl.BlockSpec((1, 1, d), seq_row),
                  pl.BlockSpec((1, C_HEADS, hw), seq_row), pl.BlockSpec(slope_tile.shape, const2),
                  pl.BlockSpec(expand.shape, const2)] + k_specs + v_specs,
        out_specs=[pl.BlockSpec((1, tq, hw), lambda bi, h, iq, pt: (bi, iq, h)),
                   pl.BlockSpec((1, C_HEADS, hw), seq_row)],
        scratch_shapes=[pltpu.VMEM((2 * tq, 2 * hw), BF16), pltpu.VMEM((2 * tq, 2 * hw), BF16),
                        pltpu.VMEM((tk, 2 * tq), F32), pltpu.VMEM((1, 2 * tq), F32),
                        pltpu.VMEM((hw + 16, 2 * tq), F32),
                        pltpu.VMEM((2 * C_HEADS, 1), F32), pltpu.VMEM((2 * C_HEADS, 1), F32),
                        pltpu.VMEM((2 * C_HEADS, hw), F32)],
    )
    att, att_s = pl.pallas_call(
        functools.partial(_attn_fused_kernel, pg=pg, steps_per_seq=spp, n_dec=n_dec,
                          n_past=n_pages * page, lam_init=lam_init, tq=tq, tk=tk,
                          qb=min(256, tq)),
        grid_spec=grid_spec,
        out_shape=[jax.ShapeDtypeStruct((b, t, nh * hw), BF16),
                   jax.ShapeDtypeStruct((bs, C_HEADS, hw), BF16)],
        compiler_params=_cparams(("arbitrary", "arbitrary", "arbitrary")),
        name="attn_fused",
    )(page_table, slopes * LOG2E, qh, kh, vt, lam_p, subln, q3, kn3, vn3, slope_tile, expand,
      *([cache_kt] * pg), *([cache_v2] * pg))
    return att, att_s.reshape(bs, d)


def _row_tile(m, pref):
    tm = min(pref, m)
    assert m % tm == 0, (m, tm)
    return tm


def kernel(x_prompt, x_sample, cache_k, cache_v, page_table, state_gdn, state_gdn_conv, state_ffn_conv, norm_mix_pre, norm_mix_post, norm_ffn_pre, norm_ffn_post, w_in_even, a_v_norm, a_w_s, a_b_s, b_conv_w, b_a_log, b_dt_bias, b_out_norm, w_out_even, w_in_odd, c_lambda, c_subln, w_out_odd, w_ffn_gate, w_ffn_up, w_ffn_conv, b_ffn_conv, w_ffn_down):
    b, t, d = x_prompt.shape
    bs = x_sample.shape[0]
    assert x_sample.shape[1] == 1 and t % GDN_STEP == 0
    aw = a_v_norm.shape[-1]
    bw = b_conv_w.shape[-1] // 3
    ff = w_ffn_gate.shape[-1]
    n_split = 2 * aw + 4 * bw
    hw = d // C_HEADS
    _, n_pool, page, _, _, hd = cache_k.shape
    row2 = lambda a: a.reshape(1, -1)

    w_even = w_in_even[0].astype(BF16)
    w_ba = jnp.pad(w_in_even[0, :, n_split:], ((0, 0), (0, 128 - 2 * B_HEADS))).astype(BF16)
    prm = jnp.zeros((8, 128), F32)
    prm = prm.at[0, B_HEADS:2 * B_HEADS].set(b_dt_bias[0]).at[1, B_HEADS:2 * B_HEADS].set(b_a_log[0])
    onorm = row2(b_out_norm[0])
    w_oe = w_out_even[0].astype(BF16)
    w_odd = w_in_odd[0].astype(BF16)
    w_oo = w_out_odd[0].astype(BF16)
    wg = w_ffn_gate.astype(BF16)
    wu = w_ffn_up.astype(BF16)
    wd = w_ffn_down.astype(BF16)
    slopes = jnp.exp2(-8.0 * jnp.arange(1, C_HEADS + 1, dtype=F32) / C_HEADS)
    slope_tile = jnp.broadcast_to(jnp.tile(slopes, 2)[:, None], (2 * C_HEADS, 128))
    lam_init = 0.8 - 0.6 * math.exp(-0.3 * 1)
    ws_step = row2(jnp.repeat(a_w_s[0, :, 0, 0], aw // A_GROUPS))
    bs_step = row2(jnp.repeat(a_b_s[0, :, 0], aw // A_GROUPS))
    bs_t = a_b_s[0].T

    def ffn_layer(layer, x1, **kw):
        return _ffn(x1, row2(norm_ffn_pre[layer]), wg, wu, wd, w_ffn_conv[layer],
                    row2(b_ffn_conv[layer]), row2(norm_ffn_post[layer]), layer=layer, **kw)

    tm = _row_tile(b * t, 512)
    tps = t // tm
    xp = x_prompt.reshape(b * t, d)
    a_out, qkv, z, ba = _even_in(xp, row2(norm_mix_pre[0]), w_even, w_ba, row2(a_v_norm[0]),
                                 a_w_s[0], bs_t, chunk=A_CHUNK, tm=tm)
    qkv3 = qkv.reshape(b, t, 3 * bw)
    o, gdn_state_p = _gdn_prompt(qkv3, ba.reshape(b, t, 128), z.reshape(b, t, bw), b_conv_w[0],
                                 prm, onorm)
    gdn_conv_p = qkv3[:, t - (B_CONV - 1):, :]
    x1 = _mix_out([a_out, o.reshape(b * t, bw)], xp, w_oe, row2(norm_mix_post[0]), tm=tm)
    x2, gt0 = ffn_layer(0, x1, tm=tm, tiles_per_seq=tps)
    qh, kh, vt, kt_p, v_p = _odd_in(x2, row2(norm_mix_pre[1]), w_odd, tm=tm, seq_len=t)

    xs = x_sample.reshape(bs, d)
    a_out_s, v_s, qkv_s, z_s, ba_s = _even_in(xs, row2(norm_mix_pre[0]), w_even, w_ba,
                                              row2(a_v_norm[0]), ws_step, bs_step, chunk=1, tm=bs)
    o_s, gdn_state_s = _gdn_step(qkv_s, state_gdn_conv[0].reshape(bs, -1), ba_s, z_s, state_gdn[0],
                                 b_conv_w[0], prm, onorm, nb=min(16, bs))
    x1s = _mix_out([a_out_s, o_s], xs, w_oe, row2(norm_mix_post[0]), tm=bs)
    x2s, g0s = ffn_layer(0, x1s, tm=bs, hist=state_ffn_conv[0].reshape(bs, -1))
    q_s, k_s, v_sn = _odd_in(x2s, row2(norm_mix_pre[1]), w_odd, tm=bs)

    n_pages = page_table.shape[1]
    cache_kt = jnp.transpose(cache_k[0], (0, 2, 3, 4, 1)).reshape(n_pool, d, page)
    cache_v2 = cache_v[0].reshape(n_pool, page * C_HEADS, hw)
    lam_p, subln = c_lambda[0], row2(c_subln[0])
    ride = [c for c in (1, 2, 4, 8) if n_pages % c == 0
            and bs * (n_pages // c) <= b * C_HEADS * (t // tm)]
    if ride:
        att, att_s = _attn_fused(slopes, qh, kh, vt, lam_p, subln, page_table, q_s, k_s, v_sn,
                                 slope_tile, cache_kt, cache_v2, tq=tm, pg=ride[0],
                                 lam_init=lam_init)
    else:
        att = _attn_prompt(slopes, qh, kh, vt, lam_p, subln, tq=tm, lam_init=lam_init)
        pg = next(c for c in (16, 8, 4, 2, 1) if n_pages % c == 0)
        att_s = _attn_decode(page_table, q_s, k_s, v_sn, slope_tile, lam_p, subln, cache_kt,
                             cache_v2, pg=pg, lam_init=lam_init)

    x3 = _mix_out([att.reshape(b * t, d)], x2, w_oo, row2(norm_mix_post[1]), tm=tm)
    y_p, gt1 = ffn_layer(1, x3, tm=tm, tiles_per_seq=tps)
    ffn_conv_p = jnp.stack([g.reshape(b, tps, 8, ff)[:, -1, 8 - (FFN_CONV - 1):, :]
                            for g in (gt0, gt1)])
    x3s = _mix_out([att_s], x2s, w_oo, row2(norm_mix_post[1]), tm=bs)
    y_s, g1s = ffn_layer(1, x3s, tm=bs, hist=state_ffn_conv[1].reshape(bs, -1))
    ffn_conv_s = jnp.stack([jnp.concatenate([state_ffn_conv[l][:, 1:], g[:, None, :]], axis=1)
                            for l, g in ((0, g0s), (1, g1s))])

    return (y_p.reshape(b, t, d), y_s.reshape(bs, 1, d),
            gdn_state_p[None], gdn_state_s[None],
            gdn_conv_p[None],
            jnp.concatenate([state_gdn_conv[0][:, 1:], qkv_s[:, None, :]], axis=1)[None],
            v_s.reshape(1, bs, 1, aw),
            jnp.transpose(kt_p.reshape(1, b, C_HEADS, 2, hd, t), (0, 1, 5, 2, 3, 4)),
            v_p.reshape(1, b, t, C_HEADS, hw),
            k_s.reshape(1, bs, 1, C_HEADS, 2, hd), v_sn.reshape(1, bs, 1, C_HEADS, hw),
            ffn_conv_p, ffn_conv_s)
```
